```python
import jax, jax.numpy as jnp
from jax import lax
import numpy as np

D_MODEL = 1024
BATCH = 8
SEQ = 2048
DEPTH = 4

N_META = 16
EPS = 1e-6
RNN_WIDTH = D_MODEL
RNN_BLOCKS = 8
RNN_BLOCK = RNN_WIDTH // RNN_BLOCKS
CONV_WIDTH = 4
RGLRU_C = 8.0
GLA_HEADS = 4
GLA_DK = D_MODEL // 2 // GLA_HEADS
GLA_DV = D_MODEL // GLA_HEADS
GLA_GATE_RANK = 16
GLA_TAU = 16.0
GLA_CHUNK = 64
MLA_HEADS = 16
MLA_NOPE = 64
MLA_ROPE = 32
MLA_V = 64
MLA_Q_RANK = D_MODEL // 2
MLA_KV_RANK = D_MODEL // 4
ROPE_BASE = 10000.0
Q_BLOCK = 128

N_AB = (DEPTH + 1) // 2
N_C = DEPTH // 2

EVEN_SPLITS = (RNN_WIDTH, RNN_WIDTH, GLA_HEADS * GLA_DK, GLA_HEADS * GLA_DK,
               GLA_HEADS * GLA_DV, GLA_GATE_RANK, GLA_HEADS * GLA_DV)
EVEN_IN = sum(EVEN_SPLITS)
EVEN_MIX = RNN_WIDTH + GLA_HEADS * GLA_DV
ODD_SPLITS = (MLA_Q_RANK, MLA_KV_RANK, MLA_ROPE, MLA_HEADS * MLA_V)
ODD_IN = sum(ODD_SPLITS)
ODD_MIX = MLA_HEADS * MLA_V

kernel_name = "hybrid_rglru_gla_mla_meta"


def _split(t, sizes):
    idx = np.cumsum(sizes)[:-1].tolist()
    return jnp.split(t, idx, axis=-1)


def rmsnorm(x, g):
    xf = x.astype(jnp.float32)
    y = xf * lax.rsqrt(jnp.mean(xf * xf, axis=-1, keepdims=True) + EPS)
    return (y * g.astype(jnp.float32)).astype(x.dtype)


def apply_rope(x, cos, sin):
    xf = x.astype(jnp.float32)
    x1, x2 = jnp.split(xf, 2, axis=-1)
    return jnp.concatenate([x1 * cos - x2 * sin, x2 * cos + x1 * sin], axis=-1).astype(x.dtype)


def causal_conv(x, w, b):
    T = x.shape[1]
    xp = jnp.pad(x, ((0, 0), (CONV_WIDTH - 1, 0), (0, 0)))
    y = xp[:, 0:T] * w[0]
    for k in range(1, CONV_WIDTH):
        y = y + xp[:, k:k + T] * w[k]
    return y + b


def rglru(x, gate_a_w, gate_a_b, gate_x_w, gate_x_b, lam):
    B, T, _ = x.shape
    xf = x.astype(jnp.float32)
    xb = xf.reshape(B, T, RNN_BLOCKS, RNN_BLOCK)
    r = jax.nn.sigmoid(jnp.einsum('btgi,gij->btgj', xb, gate_a_w.astype(jnp.float32)).reshape(B, T, RNN_WIDTH)
                       + gate_a_b.astype(jnp.float32))
    i = jax.nn.sigmoid(jnp.einsum('btgi,gij->btgj', xb, gate_x_w.astype(jnp.float32)).reshape(B, T, RNN_WIDTH)
                       + gate_x_b.astype(jnp.float32))
    log_a = -RGLRU_C * r * jax.nn.softplus(-lam.astype(jnp.float32))
    a = jnp.exp(log_a)
    u = jnp.sqrt(-jnp.expm1(2.0 * log_a)) * (i * xf)

    def combine(e1, e2):
        a1, b1 = e1
        a2, b2 = e2
        return a1 * a2, a2 * b1 + b2

    _, h = lax.associative_scan(combine, (a, u), axis=1)
    return h.astype(x.dtype)


def gla_chunked(q, k, v, log_alpha):
    B, T, H, DK = q.shape
    DV = v.shape[-1]
    dtype = v.dtype
    pad = (-N_META) % GLA_CHUNK
    padf = lambda t: jnp.pad(t.astype(jnp.float32), ((0, 0), (pad, 0), (0, 0), (0, 0)))
    q, k, v, la = padf(q), padf(k), padf(v), padf(log_alpha)
    Tp = T + pad
    N = Tp // GLA_CHUNK

    def chunks(t):
        return t.reshape(B, N, GLA_CHUNK, H, t.shape[-1]).transpose(1, 0, 3, 2, 4)

    q, k, v, la = chunks(q), chunks(k), chunks(v), chunks(la)
    b = jnp.cumsum(la, axis=3)
    b_last = b[:, :, :, -1:, :]
    q_dec = q * jnp.exp(b)
    k_inv = k * jnp.exp(-b)
    k_end = k * jnp.exp(b_last - b)
    causal = jnp.tril(jnp.ones((GLA_CHUNK, GLA_CHUNK), dtype=bool))
    s = jnp.where(causal, jnp.einsum('nbhcd,nbhsd->nbhcs', q_dec, k_inv), 0.0)
    o_intra = jnp.einsum('nbhcs,nbhse->nbhce', s, v)

    def step(S, inp):
        qd, ke, vv, bl = inp
        o = jnp.einsum('bhcd,bhde->bhce', qd, S)
        S = jnp.exp(bl)[:, :, 0, :, None] * S + jnp.einsum('bhcd,bhce->bhde', ke, vv)
        return S, o

    S0 = jnp.zeros((B, H, DK, DV), jnp.float32)
    _, o_inter = lax.scan(step, S0, (q_dec, k_end, v, b_last))
    o = (o_intra + o_inter).transpose(1, 0, 3, 2, 4).reshape(B, Tp, H, DV)[:, pad:]
    return o.astype(dtype)


def ab_mixer(h, w_in, conv_w, conv_b, gate_a_w, gate_a_b, gate_x_w, gate_x_b, lam,
             alpha_w, alpha_b, gla_norm, w_out):
    B, T, _ = h.shape
    xa, ga, q, k, v, ad, gb = _split(h @ w_in, EVEN_SPLITS)
    ya = rglru(causal_conv(xa, conv_w, conv_b), gate_a_w, gate_a_b, gate_x_w, gate_x_b, lam)
    ya = ya * jax.nn.silu(ga)
    q = q.reshape(B, T, GLA_HEADS, GLA_DK) * (GLA_DK ** -0.5)
    k = k.reshape(B, T, GLA_HEADS, GLA_DK)
    v = v.reshape(B, T, GLA_HEADS, GLA_DV)
    log_alpha = jax.nn.log_sigmoid((ad @ alpha_w + alpha_b).astype(jnp.float32)) / GLA_TAU
    log_alpha = log_alpha.reshape(B, T, GLA_HEADS, GLA_DK)
    ob = rmsnorm(gla_chunked(q, k, v, log_alpha), gla_norm).reshape(B, T, GLA_HEADS * GLA_DV)
    ob = ob * jax.nn.silu(gb)
    return jnp.concatenate([ya, ob], axis=-1) @ w_out


def causal_mla_attention(q_nope, q_rope, k_nope, k_rope, v):
    T = q_nope.shape[1]
    Tp = -(-T // Q_BLOCK) * Q_BLOCK
    padt = lambda t: jnp.pad(t, ((0, 0), (0, Tp - T)) + ((0, 0),) * (t.ndim - 2))
    q_nope, q_rope, k_nope, k_rope, v = map(padt, (q_nope, q_rope, k_nope, k_rope, v))
    scale = (MLA_NOPE + MLA_ROPE) ** -0.5
    outs = []
    for n in range(Tp // Q_BLOCK):
        q0, q1 = n * Q_BLOCK, (n + 1) * Q_BLOCK
        s = (jnp.einsum('bqhd,bkhd->bhqk', q_nope[:, q0:q1], k_nope[:, :q1])
             + jnp.einsum('bqhr,bkr->bhqk', q_rope[:, q0:q1], k_rope[:, :q1]))
        s = s.astype(jnp.float32) * scale
        qi = jnp.arange(q0, q1)[:, None]
        ki = jnp.arange(q1)[None, :]
        s = jnp.where(ki <= qi, s, -jnp.inf)
        p = jax.nn.softmax(s, axis=-1).astype(v.dtype)
        outs.append(jnp.einsum('bhqk,bkhd->bqhd', p, v[:, :q1]))
    return jnp.concatenate(outs, axis=1)[:, :T]


def mla_mixer(h, w_in, q_norm, w_q_up, kv_norm, w_kv_up, w_out, cos, sin):
    B, T, _ = h.shape
    cq, ckv, k_rope, gate = _split(h @ w_in, ODD_SPLITS)
    q = (rmsnorm(cq, q_norm) @ w_q_up).reshape(B, T, MLA_HEADS, MLA_NOPE + MLA_ROPE)
    q_nope, q_rope = q[..., :MLA_NOPE], q[..., MLA_NOPE:]
    kv = (rmsnorm(ckv, kv_norm) @ w_kv_up).reshape(B, T, MLA_HEADS, MLA_NOPE + MLA_V)
    k_nope, v = kv[..., :MLA_NOPE], kv[..., MLA_NOPE:]
    q_rope = apply_rope(q_rope, cos[:, :, None, :], sin[:, :, None, :])
    k_rope = apply_rope(k_rope, cos, sin)
    o = causal_mla_attention(q_nope, q_rope, k_nope, k_rope, v).reshape(B, T, ODD_MIX)
    return (o * jax.nn.silu(gate)) @ w_out


def setup_inputs(seed: int = 0) -> dict:
    key = jax.random.key(seed)
    ks = iter(jax.random.split(key, 32))
    nrm = lambda shape, scale: jax.random.normal(next(ks), shape, jnp.float32) * scale
    u = jax.random.uniform(next(ks), (N_AB, RNN_WIDTH), jnp.float32, 0.9, 0.999)
    s = u ** (1.0 / RGLRU_C)
    lam = jnp.log(s) - jnp.log1p(-s)
    return {
        "x": nrm((BATCH, SEQ, D_MODEL), 1.0),
        "positions": jnp.broadcast_to(jnp.arange(SEQ, dtype=jnp.int32)[None], (BATCH, SEQ)),
        "meta_tokens": nrm((N_META, D_MODEL), 1.0),
        "ab_norm": 1.0 + nrm((N_AB, D_MODEL), 0.1),
        "ab_w_in": nrm((N_AB, D_MODEL, EVEN_IN), D_MODEL ** -0.5),
        "ab_conv_w": nrm((N_AB, CONV_WIDTH, RNN_WIDTH), CONV_WIDTH ** -0.5),
        "ab_conv_b": nrm((N_AB, RNN_WIDTH), 0.01),
        "ab_gate_a_w": nrm((N_AB, RNN_BLOCKS, RNN_BLOCK, RNN_BLOCK), RNN_BLOCK ** -0.5),
        "ab_gate_a_b": nrm((N_AB, RNN_WIDTH), 0.01),
        "ab_gate_x_w": nrm((N_AB, RNN_BLOCKS, RNN_BLOCK, RNN_BLOCK), RNN_BLOCK ** -0.5),
        "ab_gate_x_b": nrm((N_AB, RNN_WIDTH), 0.01),
        "ab_lru_lambda": lam,
        "ab_alpha_w": nrm((N_AB, GLA_GATE_RANK, GLA_HEADS * GLA_DK), GLA_GATE_RANK ** -0.5),
        "ab_alpha_b": nrm((N_AB, GLA_HEADS * GLA_DK), 0.1),
        "ab_gla_norm": 1.0 + nrm((N_AB, GLA_DV), 0.1),
        "ab_w_out": nrm((N_AB, EVEN_MIX, D_MODEL), EVEN_MIX ** -0.5),
        "c_norm": 1.0 + nrm((N_C, D_MODEL), 0.1),
        "c_w_in": nrm((N_C, D_MODEL, ODD_IN), D_MODEL ** -0.5),
        "c_q_norm": 1.0 + nrm((N_C, MLA_Q_RANK), 0.1),
        "c_w_q_up": nrm((N_C, MLA_Q_RANK, MLA_HEADS * (MLA_NOPE + MLA_ROPE)), MLA_Q_RANK ** -0.5),
        "c_kv_norm": 1.0 + nrm((N_C, MLA_KV_RANK), 0.1),
        "c_w_kv_up": nrm((N_C, MLA_KV_RANK, MLA_HEADS * (MLA_NOPE + MLA_V)), MLA_KV_RANK ** -0.5),
        "c_w_out": nrm((N_C, ODD_MIX, D_MODEL), ODD_MIX ** -0.5),
        "final_norm": 1.0 + nrm((D_MODEL,), 0.1),
    }


def reference(x, positions, meta_tokens, ab_norm, ab_w_in, ab_conv_w, ab_conv_b, ab_gate_a_w,
              ab_gate_a_b, ab_gate_x_w, ab_gate_x_b, ab_lru_lambda, ab_alpha_w, ab_alpha_b,
              ab_gla_norm, ab_w_out, c_norm, c_w_in, c_q_norm, c_w_q_up, c_kv_norm, c_w_kv_up,
              c_w_out, final_norm):
    B = x.shape[0]
    meta = jnp.broadcast_to(meta_tokens.astype(x.dtype)[None], (B, N_META, D_MODEL))
    h = jnp.concatenate([meta, x], axis=1)
    meta_pos = jnp.broadcast_to(jnp.arange(N_META, dtype=positions.dtype)[None], (B, N_META))
    pos = jnp.concatenate([meta_pos, positions + N_META], axis=1)
    inv_freq = ROPE_BASE ** (-jnp.arange(0, MLA_ROPE, 2, dtype=jnp.float32) / MLA_ROPE)
    ang = pos.astype(jnp.float32)[..., None] * inv_freq
    cos, sin = jnp.cos(ang), jnp.sin(ang)
    for layer in range(DEPTH):
        j = layer // 2
        if layer % 2 == 0:
            h = h + ab_mixer(rmsnorm(h, ab_norm[j]), ab_w_in[j], ab_conv_w[j], ab_conv_b[j],
                             ab_gate_a_w[j], ab_gate_a_b[j], ab_gate_x_w[j], ab_gate_x_b[j],
                             ab_lru_lambda[j], ab_alpha_w[j], ab_alpha_b[j], ab_gla_norm[j],
                             ab_w_out[j])
        else:
            h = h + mla_mixer(rmsnorm(h, c_norm[j]), c_w_in[j], c_q_norm[j], c_w_q_up[j],
                              c_kv_norm[j], c_w_kv_up[j], c_w_out[j], cos, sin)
    h = rmsnorm(h, final_norm)
    return h[:, N_META:]
```

```python
import functools

import jax
import jax.numpy as jnp
import numpy as np
from jax import lax
from jax.experimental import pallas as pl
from jax.experimental.pallas import tpu as pltpu

F32 = jnp.float32
BF16 = jnp.bfloat16

D_MODEL = 1024
N_META = 16
EPS = 1e-6
RNN_WIDTH = D_MODEL
RNN_BLOCKS = 8
RNN_BLOCK = RNN_WIDTH // RNN_BLOCKS
CONV_WIDTH = 4
RGLRU_C = 8.0
GLA_HEADS = 4
GLA_DK = 128
GLA_DV = 256
GLA_GATE_RANK = 16
GLA_TAU = 16.0
GLA_CHUNK = 64
MLA_HEADS = 16
MLA_NOPE = 64
MLA_ROPE = 32
MLA_V = 64
MLA_Q_RANK = 512
MLA_KV_RANK = 256
ROPE_BASE = 10000.0

LANES = 128
HEAD_PAD = 128
ATT_TILE = 256
NEG_BIG = -1e30
VMEM_LIMIT = 56 * 1024 * 1024


def _pick(n, candidates):
    for c in candidates:
        if n % c == 0:
            return c
    raise ValueError(f"no tile in {candidates} divides {n}")


def _rms(x, g):
    var = jnp.mean(x * x, axis=-1, keepdims=True)
    return x * lax.rsqrt(var + EPS) * g


def _sigmoid(x):
    return 1.0 / (1.0 + jnp.exp(-x))


def _const_spec(shape):
    nd = len(shape)
    return pl.BlockSpec(shape, lambda *_: (0,) * nd)


def _params(sem):
    return pltpu.CompilerParams(dimension_semantics=sem, vmem_limit_bytes=VMEM_LIMIT)


EVEN_OUT_WIDTHS = (RNN_WIDTH, RNN_WIDTH, GLA_HEADS * GLA_DK, GLA_HEADS * GLA_DK,
                   GLA_HEADS * GLA_DV, GLA_HEADS * GLA_DV, LANES)


def _even_in_kernel(h_ref, g_ref, w_ref, *out_refs):
    xn = _rms(h_ref[...], g_ref[...]).astype(BF16)
    off = 0
    for ref in out_refs:
        n = ref.shape[-1]
        ref[...] = jnp.dot(xn, w_ref[:, off:off + n], preferred_element_type=F32).astype(ref.dtype)
        off += n


def _even_in(h2, g, w):
    M = h2.shape[0]
    tm = _pick(M, (512, 256, 128))
    outs = tuple(jax.ShapeDtypeStruct((M, n), BF16) for n in EVEN_OUT_WIDTHS)
    return pl.pallas_call(
        _even_in_kernel,
        out_shape=outs,
        grid=(M // tm,),
        in_specs=[pl.BlockSpec((tm, D_MODEL), lambda i: (i, 0)),
                  _const_spec((1, D_MODEL)),
                  _const_spec(w.shape)],
        out_specs=tuple(pl.BlockSpec((tm, n), lambda i: (i, 0)) for n in EVEN_OUT_WIDTHS),
        compiler_params=_params(("parallel",)),
        name="even_in",
    )(h2, g, w)


LRU_HIST = 4
LRU_LEAD = 8


def _rglru_kernel(xa_ref, ga_ref, cw_ref, cb_ref, wg_ref, ba_ref, bx_ref, lam_ref, y_ref,
                  xflat, a_s, u_s, h_s, *, tt, pitch, nblk):
    B = xa_ref.shape[0]
    R = B * pitch
    t_idx = pl.program_id(1)

    @pl.when(t_idx == 0)
    def _():
        xflat[...] = jnp.zeros_like(xflat)
        h_s[...] = jnp.zeros_like(h_s)

    @pl.when(t_idx > 0)
    def _():
        for b in range(B):
            base = LRU_LEAD + b * pitch
            xflat[base:base + LRU_HIST, :] = xflat[base + tt:base + tt + LRU_HIST, :]

    for b in range(B):
        base = LRU_LEAD + b * pitch + LRU_HIST
        xflat[base:base + tt, :] = xa_ref[b].astype(F32)

    for j in range(nblk):
        lanes = slice(j * LANES, (j + 1) * LANES)
        y = cb_ref[:, lanes] + xflat[LRU_LEAD - 3:LRU_LEAD - 3 + R, lanes] * cw_ref[0:1, lanes]
        for k in range(1, CONV_WIDTH):
            lo = LRU_LEAD - 3 + k
            y = y + xflat[lo:lo + R, lanes] * cw_ref[k:k + 1, lanes]
        g = jnp.dot(y.astype(BF16), wg_ref[j], preferred_element_type=F32)
        r = _sigmoid(g[:, :LANES] + ba_ref[:, lanes])
        i = _sigmoid(g[:, LANES:] + bx_ref[:, lanes])
        lam = lam_ref[:, lanes]
        softplus_neg_lam = jnp.maximum(-lam, 0.0) + jnp.log(1.0 + jnp.exp(-jnp.abs(lam)))
        a = jnp.exp((-RGLRU_C) * r * softplus_neg_lam)
        a_s[j] = a
        u_s[j] = jnp.sqrt(1.0 - a * a) * (i * y)

    def step(t, hs):
        out = []
        for j in range(nblk):
            idx = pl.ds(LRU_HIST + t, B, stride=pitch)
            h = a_s[j, idx, :] * hs[j] + u_s[j, idx, :]
            u_s[j, idx, :] = h
            out.append(h)
        return tuple(out)

    hs = lax.fori_loop(0, tt, step, tuple(h_s[j] for j in range(nblk)))
    for j in range(nblk):
        h_s[j] = hs[j]

    for j in range(nblk):
        lanes = slice(j * LANES, (j + 1) * LANES)
        for b in range(B):
            base = b * pitch + LRU_HIST
            ga = ga_ref[b, :, lanes].astype(F32)
            y_ref[b, :, lanes] = (u_s[j, base:base + tt, :] * (ga * _sigmoid(ga))).astype(y_ref.dtype)


def _rglru(xa, ga, cw, cb, wg, ba, bx, lam):
    B, Tp, C = xa.shape
    tt = _pick(Tp, (272, 256, 192, 128))
    pitch = tt + LRU_HIST
    cblk = 512
    nblk = cblk // LANES
    blk = pl.BlockSpec((B, tt, cblk), lambda c, t: (0, t, c))
    vec = lambda rows: pl.BlockSpec((rows, cblk), lambda c, t: (0, c))
    return pl.pallas_call(
        functools.partial(_rglru_kernel, tt=tt, pitch=pitch, nblk=nblk),
        out_shape=jax.ShapeDtypeStruct((B, Tp, C), BF16),
        grid=(C // cblk, Tp // tt),
        in_specs=[blk, blk, vec(CONV_WIDTH), vec(1),
                  pl.BlockSpec((nblk, RNN_BLOCK, 2 * RNN_BLOCK), lambda c, t: (c, 0, 0)),
                  vec(1), vec(1), vec(1)],
        out_specs=blk,
        scratch_shapes=[pltpu.VMEM((LRU_LEAD + B * pitch, cblk), F32),
                        pltpu.VMEM((nblk, B * pitch, LANES), F32),
                        pltpu.VMEM((nblk, B * pitch, LANES), F32),
                        pltpu.VMEM((nblk, B, LANES), F32)],
        compiler_params=_params(("parallel", "arbitrary")),
        name="rglru",
    )(xa, ga, cw, cb, wg, ba, bx, lam)


def _gla_kernel(q_ref, k_ref, v_ref, gb_ref, ad_ref, aw_ref, ab_ref, gn_ref, o_ref, st_ref, *, nchunk):
    C = GLA_CHUNK

    @pl.when(pl.program_id(1) == 0)
    def _():
        st_ref[...] = jnp.zeros_like(st_ref)

    row = lax.broadcasted_iota(jnp.int32, (C, C), 0)
    col = lax.broadcasted_iota(jnp.int32, (C, C), 1)
    causal = row >= col
    tril = causal.astype(BF16)
    scale = GLA_DK ** -0.5

    def chunk(c, carry):
        rows = pl.ds(pl.multiple_of(c * C, C), C)
        pre = jnp.dot(ad_ref[0, rows, :], aw_ref[...], preferred_element_type=F32) + ab_ref[...]
        la = (jnp.minimum(pre, 0.0) - jnp.log(1.0 + jnp.exp(-jnp.abs(pre)))) * (1.0 / GLA_TAU)
        la_hi = la.astype(BF16)
        la_lo = (la - la_hi.astype(F32)).astype(BF16)
        bc = (jnp.dot(tril, la_hi, preferred_element_type=F32)
              + jnp.dot(tril, la_lo, preferred_element_type=F32))
        b_last = bc[C - 1:C, :]
        q = q_ref[0, rows, :].astype(F32)
        k = k_ref[0, rows, :].astype(F32)
        q_dec = (q * (jnp.exp(bc) * scale)).astype(BF16)
        k_inv = (k * jnp.exp(-bc)).astype(BF16)
        k_end = (k * jnp.exp(b_last - bc)).astype(BF16)
        decay = jnp.exp(b_last)
        for h in range(GLA_HEADS):
            kl = slice(h * GLA_DK, (h + 1) * GLA_DK)
            vl = slice(h * GLA_DV, (h + 1) * GLA_DV)
            qd, ki, ke = q_dec[:, kl], k_inv[:, kl], k_end[:, kl]
            vh = v_ref[0, rows, vl]
            st = st_ref[h]
            s = lax.dot_general(qd, ki, (((1,), (1,)), ((), ())), preferred_element_type=F32)
            s = jnp.where(causal, s, 0.0).astype(BF16)
            o = (jnp.dot(s, vh, preferred_element_type=F32)
                 + lax.dot_general(qd, st.astype(BF16), (((1,), (1,)), ((), ())),
                                   preferred_element_type=F32))
            vk = lax.dot_general(vh, ke, (((0,), (0,)), ((), ())), preferred_element_type=F32)
            st_ref[h] = st * decay[:, kl] + vk
            gb = gb_ref[0, rows, vl].astype(F32)
            o_ref[0, rows, vl] = (_rms(o, gn_ref[...]) * (gb * _sigmoid(gb))).astype(o_ref.dtype)
        return carry

    lax.fori_loop(0, nchunk, chunk, 0)


def _gla(q, k, v, gb, ad, aw, ab, gn):
    B, Tp, _ = q.shape
    tc = Tp // 2 if (Tp // 2) % GLA_CHUNK == 0 else Tp
    kw, vw = GLA_HEADS * GLA_DK, GLA_HEADS * GLA_DV
    spec = lambda w: pl.BlockSpec((1, tc, w), lambda b, t: (b, t, 0))
    return pl.pallas_call(
        functools.partial(_gla_kernel, nchunk=tc // GLA_CHUNK),
        out_shape=jax.ShapeDtypeStruct((B, Tp, vw), BF16),
        grid=(B, Tp // tc),
        in_specs=[spec(kw), spec(kw), spec(vw), spec(vw), spec(LANES),
                  _const_spec(aw.shape), _const_spec(ab.shape), _const_spec(gn.shape)],
        out_specs=spec(vw),
        scratch_shapes=[pltpu.VMEM((GLA_HEADS, GLA_DV, GLA_DK), F32)],
        compiler_params=_params(("parallel", "arbitrary")),
        name="gla",
    )(q, k, v, gb, ad, aw, ab, gn)


def _even_out_kernel(h_ref, ya_ref, ob_ref, w_ref, o_ref):
    n = ya_ref.shape[-1]
    o_ref[...] = (h_ref[...]
                  + jnp.dot(ya_ref[...], w_ref[:n, :], preferred_element_type=F32)
                  + jnp.dot(ob_ref[...], w_ref[n:, :], preferred_element_type=F32))


def _even_out(h2, ya, ob, w):
    M = h2.shape[0]
    tm = _pick(M, (512, 256, 128))
    row = lambda n: pl.BlockSpec((tm, n), lambda i: (i, 0))
    return pl.pallas_call(
        _even_out_kernel,
        out_shape=jax.ShapeDtypeStruct((M, D_MODEL), F32),
        grid=(M // tm,),
        in_specs=[row(D_MODEL), row(ya.shape[-1]), row(ob.shape[-1]), _const_spec(w.shape)],
        out_specs=row(D_MODEL),
        compiler_params=_params(("parallel",)),
        name="even_out",
    )(h2, ya, ob, w)


def _rope_table_kernel(pos_ref, invf_ref, c_ref, s_ref):
    ang = pos_ref[...] * invf_ref[...]
    lane = lax.broadcasted_iota(jnp.int32, ang.shape, 1)
    half = MLA_ROPE // 2
    c_ref[...] = jnp.where(lane < MLA_NOPE, 1.0, jnp.where(lane < MLA_NOPE + MLA_ROPE, jnp.cos(ang), 0.0))
    sn = jnp.sin(ang)
    s_ref[...] = jnp.where(lane < MLA_NOPE, 0.0,
                           jnp.where(lane < MLA_NOPE + half, -sn,
                                     jnp.where(lane < MLA_NOPE + MLA_ROPE, sn, 0.0)))


def _rope_tables(posf, invf):
    M = posf.shape[0]
    tm = _pick(M, (512, 256, 128))
    out = jax.ShapeDtypeStruct((M, HEAD_PAD), F32)
    return pl.pallas_call(
        _rope_table_kernel,
        out_shape=(out, out),
        grid=(M // tm,),
        in_specs=[pl.BlockSpec((tm, 1), lambda i: (i, 0)), _const_spec((1, HEAD_PAD))],
        out_specs=(pl.BlockSpec((tm, HEAD_PAD), lambda i: (i, 0)),) * 2,
        compiler_params=_params(("parallel",)),
        name="rope_tables",
    )(posf, invf)


def _rope(x, cos_t, sin_t):
    lane = lax.broadcasted_iota(jnp.int32, x.shape, 1)
    half = MLA_ROPE // 2
    partner = jnp.where(lane < MLA_NOPE + half,
                        pltpu.roll(x, HEAD_PAD - half, axis=1),
                        pltpu.roll(x, half, axis=1))
    return x * cos_t + partner * sin_t


def _odd_in_kernel(h_ref, g_ref, w_ref, qn_ref, wq_ref, kvn_ref, wk_ref, wv_ref, cos_ref, sin_ref,
                   q_ref, k_ref, v_ref, gate_ref):
    xn = _rms(h_ref[...], g_ref[...]).astype(BF16)
    c0, c1, c2 = MLA_Q_RANK, MLA_Q_RANK + MLA_KV_RANK, MLA_Q_RANK + MLA_KV_RANK + HEAD_PAD
    gate_ref[...] = jnp.dot(xn, w_ref[:, c2:], preferred_element_type=F32).astype(gate_ref.dtype)
    cq = jnp.dot(xn, w_ref[:, :c0], preferred_element_type=F32)
    ckv = jnp.dot(xn, w_ref[:, c0:c1], preferred_element_type=F32)
    kr = jnp.dot(xn, w_ref[:, c1:c2], preferred_element_type=F32)
    cos_t, sin_t = cos_ref[...], sin_ref[...]
    kr = _rope(kr, cos_t, sin_t)
    cqn = _rms(cq, qn_ref[...]).astype(BF16)
    ckvn = _rms(ckv, kvn_ref[...]).astype(BF16)
    v_ref[...] = jnp.dot(ckvn, wv_ref[...], preferred_element_type=F32).astype(v_ref.dtype)
    scale = (MLA_NOPE + MLA_ROPE) ** -0.5
    for hd in range(MLA_HEADS):
        lanes = slice(hd * HEAD_PAD, (hd + 1) * HEAD_PAD)
        qh = jnp.dot(cqn, wq_ref[:, lanes], preferred_element_type=F32)
        q_ref[:, lanes] = (_rope(qh, cos_t, sin_t) * scale).astype(q_ref.dtype)
        kh = jnp.dot(ckvn, wk_ref[:, lanes], preferred_element_type=F32)
        k_ref[:, lanes] = (kh + kr).astype(k_ref.dtype)


def _odd_in(h2, g, w, qn, wq, kvn, wk, wv, cos_t, sin_t):
    M = h2.shape[0]
    tm = _pick(M, (512, 256, 128))
    row = lambda n: pl.BlockSpec((tm, n), lambda i: (i, 0))
    hw = MLA_HEADS * HEAD_PAD
    vw = MLA_HEADS * MLA_V
    outs = (jax.ShapeDtypeStruct((M, hw), BF16), jax.ShapeDtypeStruct((M, hw), BF16),
            jax.ShapeDtypeStruct((M, vw), BF16), jax.ShapeDtypeStruct((M, vw), BF16))
    return pl.pallas_call(
        _odd_in_kernel,
        out_shape=outs,
        grid=(M // tm,),
        in_specs=[row(D_MODEL), _const_spec(g.shape), _const_spec(w.shape), _const_spec(qn.shape),
                  _const_spec(wq.shape), _const_spec(kvn.shape), _const_spec(wk.shape),
                  _const_spec(wv.shape), row(HEAD_PAD), row(HEAD_PAD)],
        out_specs=(row(hw), row(hw), row(vw), row(vw)),
        compiler_params=_params(("parallel",)),
        name="odd_in",
    )(h2, g, w, qn, wq, kvn, wk, wv, cos_t, sin_t)


def _attn_kernel(q_ref, k_ref, v_ref, o_ref, *, n_full, rem):
    T = ATT_TILE
    NH = 2

    def tile_update(carry, qh, koff, kw, masked):
        new = []
        for h in range(NH):
            m, l, acc = carry[h]
            kt = k_ref[0, pl.ds(koff, kw), h * HEAD_PAD:(h + 1) * HEAD_PAD]
            vt = v_ref[0, pl.ds(koff, kw), h * MLA_V:(h + 1) * MLA_V]
            st = lax.dot_general(kt, qh[h], (((1,), (1,)), ((), ())), preferred_element_type=F32)
            if masked:
                kk = lax.broadcasted_iota(jnp.int32, st.shape, 0)
                qq = lax.broadcasted_iota(jnp.int32, st.shape, 1)
                st = jnp.where(kk <= qq, st, NEG_BIG)
            m_new = jnp.maximum(m, jnp.max(st, axis=0, keepdims=True))
            alpha = jnp.exp(m - m_new)
            p = jnp.exp(st - m_new)
            l = alpha * l + jnp.sum(p, axis=0, keepdims=True)
            pv = lax.dot_general(vt, p.astype(BF16), (((0,), (0,)), ((), ())),
                                 preferred_element_type=F32)
            new.append((m_new, l, alpha * acc + pv))
        return tuple(new)

    def q_tile(qoff, qw, nfull_k):
        qh = [q_ref[0, pl.ds(qoff, qw), h * HEAD_PAD:(h + 1) * HEAD_PAD] for h in range(NH)]
        init = tuple((jnp.full((1, qw), NEG_BIG, F32), jnp.zeros((1, qw), F32),
                      jnp.zeros((MLA_V, qw), F32)) for _ in range(NH))

        def kstep(i, carry):
            return tile_update(carry, qh, pl.multiple_of(i * T, T), T, False)

        carry = lax.fori_loop(0, nfull_k, kstep, init)
        carry = tile_update(carry, qh, qoff, qw, True)
        out_t = jnp.concatenate([acc / l for (_, l, acc) in carry], axis=0)
        o_ref[0, pl.ds(qoff, qw), :] = out_t.T.astype(o_ref.dtype)

    def q_loop(j, c):
        q_tile(pl.multiple_of(j * T, T), T, j)
        return c

    lax.fori_loop(0, n_full, q_loop, 0)
    if rem:
        q_tile(n_full * T, rem, n_full)


def _attention(q, k, v):
    B, Tp, _ = q.shape
    pairs = MLA_HEADS // 2
    n_full, rem = Tp // ATT_TILE, Tp % ATT_TILE
    qk_spec = pl.BlockSpec((1, Tp, 2 * HEAD_PAD), lambda b, p: (b, 0, p))
    v_spec = pl.BlockSpec((1, Tp, 2 * MLA_V), lambda b, p: (b, 0, p))
    return pl.pallas_call(
        functools.partial(_attn_kernel, n_full=n_full, rem=rem),
        out_shape=jax.ShapeDtypeStruct((B, Tp, MLA_HEADS * MLA_V), BF16),
        grid=(B, pairs),
        in_specs=[qk_spec, qk_spec, v_spec],
        out_specs=v_spec,
        compiler_params=_params(("parallel", "parallel")),
        name="mla_attention",
    )(q, k, v)


def _odd_out_kernel(h_ref, o_ref, gate_ref, w_ref, fn_ref, out_ref, *, final_norm):
    gate = gate_ref[...].astype(F32)
    x = (o_ref[...].astype(F32) * (gate * _sigmoid(gate))).astype(BF16)
    h = h_ref[...] + jnp.dot(x, w_ref[...], preferred_element_type=F32)
    out_ref[...] = _rms(h, fn_ref[...]) if final_norm else h


def _odd_out(h2, o, gate, w, fn, final_norm):
    M = h2.shape[0]
    tm = _pick(M, (512, 256, 128))
    row = lambda n: pl.BlockSpec((tm, n), lambda i: (i, 0))
    return pl.pallas_call(
        functools.partial(_odd_out_kernel, final_norm=final_norm),
        out_shape=jax.ShapeDtypeStruct((M, D_MODEL), F32),
        grid=(M // tm,),
        in_specs=[row(D_MODEL), row(o.shape[-1]), row(gate.shape[-1]), _const_spec(w.shape),
                  _const_spec(fn.shape)],
        out_specs=row(D_MODEL),
        compiler_params=_params(("parallel",)),
        name="odd_out",
    )(h2, o, gate, w, fn)


def _split_cols(w, sizes):
    idx = np.cumsum(sizes)[:-1].tolist()
    return jnp.split(w, idx, axis=-1)


def _pack_even(w_in, gate_a_w, gate_x_w, alpha_w):
    xa, ga, q, k, v, ad, gb = _split_cols(
        w_in, (RNN_WIDTH, RNN_WIDTH, GLA_HEADS * GLA_DK, GLA_HEADS * GLA_DK, GLA_HEADS * GLA_DV,
               GLA_GATE_RANK, GLA_HEADS * GLA_DV))
    ad = jnp.pad(ad, ((0, 0), (0, LANES - GLA_GATE_RANK)))
    w = jnp.concatenate([xa, ga, q, k, v, gb, ad], axis=-1).astype(BF16)
    wg = jnp.concatenate([gate_a_w, gate_x_w], axis=-1).astype(BF16)
    aw = jnp.pad(alpha_w, ((0, LANES - GLA_GATE_RANK), (0, 0))).astype(BF16)
    return w, wg, aw


def _pack_odd(w_in, w_q_up, w_kv_up):
    cq, ckv, kr, gate = _split_cols(w_in, (MLA_Q_RANK, MLA_KV_RANK, MLA_ROPE, MLA_HEADS * MLA_V))
    kr = jnp.pad(kr, ((0, 0), (MLA_NOPE, HEAD_PAD - MLA_NOPE - MLA_ROPE)))
    w = jnp.concatenate([cq, ckv, kr, gate], axis=-1).astype(BF16)
    qd = MLA_NOPE + MLA_ROPE
    wq = w_q_up.reshape(MLA_Q_RANK, MLA_HEADS, qd)
    wq = jnp.pad(wq, ((0, 0), (0, 0), (0, HEAD_PAD - qd))).reshape(MLA_Q_RANK, MLA_HEADS * HEAD_PAD)
    wkv = w_kv_up.reshape(MLA_KV_RANK, MLA_HEADS, MLA_NOPE + MLA_V)
    wk = jnp.pad(wkv[..., :MLA_NOPE], ((0, 0), (0, 0), (0, HEAD_PAD - MLA_NOPE)))
    wk = wk.reshape(MLA_KV_RANK, MLA_HEADS * HEAD_PAD)
    wv = wkv[..., MLA_NOPE:].reshape(MLA_KV_RANK, MLA_HEADS * MLA_V)
    return w, wq.astype(BF16), wk.astype(BF16), wv.astype(BF16)


def kernel(x, positions, meta_tokens, ab_norm, ab_w_in, ab_conv_w, ab_conv_b, ab_gate_a_w, ab_gate_a_b, ab_gate_x_w, ab_gate_x_b, ab_lru_lambda, ab_alpha_w, ab_alpha_b, ab_gla_norm, ab_w_out, c_norm, c_w_in, c_q_norm, c_w_q_up, c_kv_norm, c_w_kv_up, c_w_out, final_norm):
    B, S, D = x.shape
    T = N_META + S
    Tp = -(-T // LANES) * LANES
    M = B * Tp
    depth = ab_norm.shape[0] + c_norm.shape[0]
    row2 = lambda a: a.reshape(1, -1).astype(F32)

    meta = jnp.broadcast_to(meta_tokens.astype(x.dtype)[None], (B, N_META, D))
    h = jnp.concatenate([meta, x, jnp.zeros((B, Tp - T, D), x.dtype)], axis=1).reshape(M, D)

    meta_pos = jnp.broadcast_to(jnp.arange(N_META, dtype=positions.dtype)[None], (B, N_META))
    pos = jnp.concatenate([meta_pos, positions + N_META,
                           jnp.zeros((B, Tp - T), positions.dtype)], axis=1)
    inv_freq = ROPE_BASE ** (-jnp.arange(0, MLA_ROPE, 2, dtype=F32) / MLA_ROPE)
    invf = jnp.concatenate([jnp.zeros((MLA_NOPE,), F32), inv_freq, inv_freq,
                            jnp.zeros((HEAD_PAD - MLA_NOPE - MLA_ROPE,), F32)]).reshape(1, HEAD_PAD)
    cos_t, sin_t = _rope_tables(pos.astype(F32).reshape(M, 1), invf)

    for layer in range(depth):
        j = layer // 2
        if layer % 2 == 0:
            w, wg, aw = _pack_even(ab_w_in[j], ab_gate_a_w[j], ab_gate_x_w[j], ab_alpha_w[j])
            xa, ga, q, k, v, gb, ad = _even_in(h, row2(ab_norm[j]), w)
            r3 = lambda a: a.reshape(B, Tp, a.shape[-1])
            ya = _rglru(r3(xa), r3(ga), ab_conv_w[j].astype(F32), row2(ab_conv_b[j]), wg,
                        row2(ab_gate_a_b[j]), row2(ab_gate_x_b[j]), row2(ab_lru_lambda[j]))
            ob = _gla(r3(q), r3(k), r3(v), r3(gb), r3(ad), aw, row2(ab_alpha_b[j]),
                      row2(ab_gla_norm[j]))
            h = _even_out(h, ya.reshape(M, -1), ob.reshape(M, -1), ab_w_out[j].astype(BF16))
        else:
            w, wq, wk, wv = _pack_odd(c_w_in[j], c_w_q_up[j], c_w_kv_up[j])
            q, k, v, gate = _odd_in(h, row2(c_norm[j]), w, row2(c_q_norm[j]), wq,
                                    row2(c_kv_norm[j]), wk, wv, cos_t, sin_t)
            r3 = lambda a: a.reshape(B, Tp, a.shape[-1])
            o = _attention(r3(q), r3(k), r3(v))
            last = layer == depth - 1
            h = _odd_out(h, o.reshape(M, -1), gate, c_w_out[j].astype(BF16), row2(final_norm), last)
    if depth % 2 == 1:
        raise NotImplementedError("final RMSNorm is fused into the last (odd) layer")
    return h.reshape(B, Tp, D)[:, N_META:T]
```

```python
import functools

import jax
import jax.numpy as jnp
import numpy as np
from jax import lax
from jax.experimental import pallas as pl
from jax.experimental.pallas import tpu as pltpu

F32 = jnp.float32
BF16 = jnp.bfloat16

D_MODEL = 1024
N_META = 16
EPS = 1e-6
RNN_WIDTH = D_MODEL
RNN_BLOCKS = 8
RNN_BLOCK = RNN_WIDTH // RNN_BLOCKS
CONV_WIDTH = 4
RGLRU_C = 8.0
GLA_HEADS = 4
GLA_DK = 128
GLA_DV = 256
GLA_GATE_RANK = 16
GLA_TAU = 16.0
GLA_CHUNK = 64
MLA_HEADS = 16
MLA_NOPE = 64
MLA_ROPE = 32
MLA_V = 64
MLA_Q_RANK = 512
MLA_KV_RANK = 256
ROPE_BASE = 10000.0

LANES = 128
HEAD_PAD = 128
ATT_TILE = 256
SOFTMAX_SLAB = 16
LOG2_E = 1.4426950408889634
NEG_BIG = -1e30
VMEM_LIMIT = 56 * 1024 * 1024


def _pick(n, candidates):
    for c in candidates:
        if n % c == 0:
            return c
    raise ValueError(f"no tile in {candidates} divides {n}")


def _rms(x, g):
    var = jnp.mean(x * x, axis=-1, keepdims=True)
    return x * lax.rsqrt(var + EPS) * g


def _sigmoid(x):
    return 1.0 / (1.0 + jnp.exp(-x))


def _const_spec(shape):
    nd = len(shape)
    return pl.BlockSpec(shape, lambda *_: (0,) * nd)


def _params(sem):
    return pltpu.CompilerParams(dimension_semantics=sem, vmem_limit_bytes=VMEM_LIMIT)


EVEN_OUT_WIDTHS = (RNN_WIDTH, RNN_WIDTH, GLA_HEADS * GLA_DK, GLA_HEADS * GLA_DK,
                   GLA_HEADS * GLA_DV, GLA_HEADS * GLA_DV, LANES)


def _even_in_kernel(h_ref, g_ref, w_ref, *out_refs):
    xn = _rms(h_ref[...], g_ref[...]).astype(BF16)
    off = 0
    for ref in out_refs:
        n = ref.shape[-1]
        ref[...] = jnp.dot(xn, w_ref[:, off:off + n], preferred_element_type=F32).astype(ref.dtype)
        off += n


def _even_in(h2, g, w):
    M = h2.shape[0]
    tm = _pick(M, (512, 256, 128))
    outs = tuple(jax.ShapeDtypeStruct((M, n), BF16) for n in EVEN_OUT_WIDTHS)
    return pl.pallas_call(
        _even_in_kernel,
        out_shape=outs,
        grid=(M // tm,),
        in_specs=[pl.BlockSpec((tm, D_MODEL), lambda i: (i, 0)),
                  _const_spec((1, D_MODEL)),
                  _const_spec(w.shape)],
        out_specs=tuple(pl.BlockSpec((tm, n), lambda i: (i, 0)) for n in EVEN_OUT_WIDTHS),
        compiler_params=_params(("parallel",)),
        name="even_in",
    )(h2, g, w)


LRU_HIST = 4
LRU_LEAD = 8


def _rglru_kernel(xa_ref, ga_ref, cw_ref, cb_ref, wg_ref, ba_ref, bx_ref, lam_ref, y_ref,
                  xflat, a_s, u_s, h_s, *, tt, pitch, nblk):
    B = xa_ref.shape[0]
    R = B * pitch
    t_idx = pl.program_id(1)

    @pl.when(t_idx == 0)
    def _():
        xflat[...] = jnp.zeros_like(xflat)
        h_s[...] = jnp.zeros_like(h_s)

    @pl.when(t_idx > 0)
    def _():
        for b in range(B):
            base = LRU_LEAD + b * pitch
            xflat[base:base + LRU_HIST, :] = xflat[base + tt:base + tt + LRU_HIST, :]

    for b in range(B):
        base = LRU_LEAD + b * pitch + LRU_HIST
        xflat[base:base + tt, :] = xa_ref[b].astype(F32)

    for j in range(nblk):
        lanes = slice(j * LANES, (j + 1) * LANES)
        y = cb_ref[:, lanes] + xflat[LRU_LEAD - 3:LRU_LEAD - 3 + R, lanes] * cw_ref[0:1, lanes]
        for k in range(1, CONV_WIDTH):
            lo = LRU_LEAD - 3 + k
            y = y + xflat[lo:lo + R, lanes] * cw_ref[k:k + 1, lanes]
        g = jnp.dot(y.astype(BF16), wg_ref[j], preferred_element_type=F32)
        r = _sigmoid(g[:, :LANES] + ba_ref[:, lanes])
        i = _sigmoid(g[:, LANES:] + bx_ref[:, lanes])
        lam = lam_ref[:, lanes]
        softplus_neg_lam = jnp.maximum(-lam, 0.0) + jnp.log(1.0 + jnp.exp(-jnp.abs(lam)))
        a = jnp.exp((-RGLRU_C) * r * softplus_neg_lam)
        a_s[j] = a
        u_s[j] = jnp.sqrt(1.0 - a * a) * (i * y)

    def step(t, hs):
        out = []
        for j in range(nblk):
            idx = pl.ds(LRU_HIST + t, B, stride=pitch)
            h = a_s[j, idx, :] * hs[j] + u_s[j, idx, :]
            u_s[j, idx, :] = h
            out.append(h)
        return tuple(out)

    hs = lax.fori_loop(0, tt, step, tuple(h_s[j] for j in range(nblk)))
    for j in range(nblk):
        h_s[j] = hs[j]

    for j in range(nblk):
        lanes = slice(j * LANES, (j + 1) * LANES)
        for b in range(B):
            base = b * pitch + LRU_HIST
            ga = ga_ref[b, :, lanes].astype(F32)
            y_ref[b, :, lanes] = (u_s[j, base:base + tt, :] * (ga * _sigmoid(ga))).astype(y_ref.dtype)


def _rglru(xa, ga, cw, cb, wg, ba, bx, lam):
    B, Tp, C = xa.shape
    tt = _pick(Tp, (272, 256, 192, 128))
    pitch = tt + LRU_HIST
    cblk = 512
    nblk = cblk // LANES
    blk = pl.BlockSpec((B, tt, cblk), lambda c, t: (0, t, c))
    vec = lambda rows: pl.BlockSpec((rows, cblk), lambda c, t: (0, c))
    return pl.pallas_call(
        functools.partial(_rglru_kernel, tt=tt, pitch=pitch, nblk=nblk),
        out_shape=jax.ShapeDtypeStruct((B, Tp, C), BF16),
        grid=(C // cblk, Tp // tt),
        in_specs=[blk, blk, vec(CONV_WIDTH), vec(1),
                  pl.BlockSpec((nblk, RNN_BLOCK, 2 * RNN_BLOCK), lambda c, t: (c, 0, 0)),
                  vec(1), vec(1), vec(1)],
        out_specs=blk,
        scratch_shapes=[pltpu.VMEM((LRU_LEAD + B * pitch, cblk), F32),
                        pltpu.VMEM((nblk, B * pitch, LANES), F32),
                        pltpu.VMEM((nblk, B * pitch, LANES), F32),
                        pltpu.VMEM((nblk, B, LANES), F32)],
        compiler_params=_params(("parallel", "arbitrary")),
        name="rglru",
    )(xa, ga, cw, cb, wg, ba, bx, lam)


def _gla_kernel(q_ref, k_ref, v_ref, gb_ref, ad_ref, aw_ref, ab_ref, gn_ref, o_ref, st_ref, *, nchunk):
    C = GLA_CHUNK

    @pl.when(pl.program_id(1) == 0)
    def _():
        st_ref[...] = jnp.zeros_like(st_ref)

    row = lax.broadcasted_iota(jnp.int32, (C, C), 0)
    col = lax.broadcasted_iota(jnp.int32, (C, C), 1)
    causal = row >= col
    tril = causal.astype(BF16)
    scale = GLA_DK ** -0.5

    def chunk(c, carry):
        rows = pl.ds(pl.multiple_of(c * C, C), C)
        pre = jnp.dot(ad_ref[0, rows, :], aw_ref[...], preferred_element_type=F32) + ab_ref[...]
        la = (jnp.minimum(pre, 0.0) - jnp.log(1.0 + jnp.exp(-jnp.abs(pre)))) * (1.0 / GLA_TAU)
        la_hi = la.astype(BF16)
        la_lo = (la - la_hi.astype(F32)).astype(BF16)
        bc = (jnp.dot(tril, la_hi, preferred_element_type=F32)
              + jnp.dot(tril, la_lo, preferred_element_type=F32))
        b_last = bc[C - 1:C, :]
        q = q_ref[0, rows, :].astype(F32)
        k = k_ref[0, rows, :].astype(F32)
        q_dec = (q * (jnp.exp(bc) * scale)).astype(BF16)
        k_inv = (k * jnp.exp(-bc)).astype(BF16)
        k_end = (k * jnp.exp(b_last - bc)).astype(BF16)
        decay = jnp.exp(b_last)
        for h in range(GLA_HEADS):
            kl = slice(h * GLA_DK, (h + 1) * GLA_DK)
            vl = slice(h * GLA_DV, (h + 1) * GLA_DV)
            qd, ki, ke = q_dec[:, kl], k_inv[:, kl], k_end[:, kl]
            vh = v_ref[0, rows, vl]
            st = st_ref[h]
            s = lax.dot_general(qd, ki, (((1,), (1,)), ((), ())), preferred_element_type=F32)
            s = jnp.where(causal, s, 0.0).astype(BF16)
            o = (jnp.dot(s, vh, preferred_element_type=F32)
                 + lax.dot_general(qd, st.astype(BF16), (((1,), (1,)), ((), ())),
                                   preferred_element_type=F32))
            vk = lax.dot_general(vh, ke, (((0,), (0,)), ((), ())), preferred_element_type=F32)
            st_ref[h] = st * decay[:, kl] + vk
            gb = gb_ref[0, rows, vl].astype(F32)
            o_ref[0, rows, vl] = (_rms(o, gn_ref[...]) * (gb * _sigmoid(gb))).astype(o_ref.dtype)
        return carry

    lax.fori_loop(0, nchunk, chunk, 0)


def _gla(q, k, v, gb, ad, aw, ab, gn):
    B, Tp, _ = q.shape
    tc = Tp // 2 if (Tp // 2) % GLA_CHUNK == 0 else Tp
    kw, vw = GLA_HEADS * GLA_DK, GLA_HEADS * GLA_DV
    spec = lambda w: pl.BlockSpec((1, tc, w), lambda b, t: (b, t, 0))
    return pl.pallas_call(
        functools.partial(_gla_kernel, nchunk=tc // GLA_CHUNK),
        out_shape=jax.ShapeDtypeStruct((B, Tp, vw), BF16),
        grid=(B, Tp // tc),
        in_specs=[spec(kw), spec(kw), spec(vw), spec(vw), spec(LANES),
                  _const_spec(aw.shape), _const_spec(ab.shape), _const_spec(gn.shape)],
        out_specs=spec(vw),
        scratch_shapes=[pltpu.VMEM((GLA_HEADS, GLA_DV, GLA_DK), F32)],
        compiler_params=_params(("parallel", "arbitrary")),
        name="gla",
    )(q, k, v, gb, ad, aw, ab, gn)


def _even_out_kernel(h_ref, ya_ref, ob_ref, w_ref, o_ref):
    n = ya_ref.shape[-1]
    o_ref[...] = (h_ref[...]
                  + jnp.dot(ya_ref[...], w_ref[:n, :], preferred_element_type=F32)
                  + jnp.dot(ob_ref[...], w_ref[n:, :], preferred_element_type=F32))


def _even_out(h2, ya, ob, w):
    M = h2.shape[0]
    tm = _pick(M, (512, 256, 128))
    row = lambda n: pl.BlockSpec((tm, n), lambda i: (i, 0))
    return pl.pallas_call(
        _even_out_kernel,
        out_shape=jax.ShapeDtypeStruct((M, D_MODEL), F32),
        grid=(M // tm,),
        in_specs=[row(D_MODEL), row(ya.shape[-1]), row(ob.shape[-1]), _const_spec(w.shape)],
        out_specs=row(D_MODEL),
        compiler_params=_params(("parallel",)),
        name="even_out",
    )(h2, ya, ob, w)


def _rope_table_kernel(pos_ref, invf_ref, c_ref, s_ref):
    ang = pos_ref[...] * invf_ref[...]
    lane = lax.broadcasted_iota(jnp.int32, ang.shape, 1)
    half = MLA_ROPE // 2
    c_ref[...] = jnp.where(lane < MLA_NOPE, 1.0, jnp.where(lane < MLA_NOPE + MLA_ROPE, jnp.cos(ang), 0.0))
    sn = jnp.sin(ang)
    s_ref[...] = jnp.where(lane < MLA_NOPE, 0.0,
                           jnp.where(lane < MLA_NOPE + half, -sn,
                                     jnp.where(lane < MLA_NOPE + MLA_ROPE, sn, 0.0)))


def _rope_tables(posf, invf):
    M = posf.shape[0]
    tm = _pick(M, (512, 256, 128))
    out = jax.ShapeDtypeStruct((M, HEAD_PAD), F32)
    return pl.pallas_call(
        _rope_table_kernel,
        out_shape=(out, out),
        grid=(M // tm,),
        in_specs=[pl.BlockSpec((tm, 1), lambda i: (i, 0)), _const_spec((1, HEAD_PAD))],
        out_specs=(pl.BlockSpec((tm, HEAD_PAD), lambda i: (i, 0)),) * 2,
        compiler_params=_params(("parallel",)),
        name="rope_tables",
    )(posf, invf)


def _rope(x, cos_t, sin_t):
    lane = lax.broadcasted_iota(jnp.int32, x.shape, 1)
    half = MLA_ROPE // 2
    partner = jnp.where(lane < MLA_NOPE + half,
                        pltpu.roll(x, HEAD_PAD - half, axis=1),
                        pltpu.roll(x, half, axis=1))
    return x * cos_t + partner * sin_t


def _odd_in_kernel(h_ref, g_ref, w_ref, qn_ref, wq_ref, kvn_ref, wk_ref, wv_ref, cos_ref, sin_ref,
                   q_ref, k_ref, v_ref, gate_ref):
    xn = _rms(h_ref[...], g_ref[...]).astype(BF16)
    c0, c1, c2 = MLA_Q_RANK, MLA_Q_RANK + MLA_KV_RANK, MLA_Q_RANK + MLA_KV_RANK + HEAD_PAD
    gate_ref[...] = jnp.dot(xn, w_ref[:, c2:], preferred_element_type=F32).astype(gate_ref.dtype)
    cq = jnp.dot(xn, w_ref[:, :c0], preferred_element_type=F32)
    ckv = jnp.dot(xn, w_ref[:, c0:c1], preferred_element_type=F32)
    kr = jnp.dot(xn, w_ref[:, c1:c2], preferred_element_type=F32)
    cos_t, sin_t = cos_ref[...], sin_ref[...]
    kr = _rope(kr, cos_t, sin_t)
    cqn = _rms(cq, qn_ref[...]).astype(BF16)
    ckvn = _rms(ckv, kvn_ref[...]).astype(BF16)
    v_ref[...] = jnp.dot(ckvn, wv_ref[...], preferred_element_type=F32).astype(v_ref.dtype)
    scale = LOG2_E * (MLA_NOPE + MLA_ROPE) ** -0.5
    for hd in range(MLA_HEADS):
        lanes = slice(hd * HEAD_PAD, (hd + 1) * HEAD_PAD)
        qh = jnp.dot(cqn, wq_ref[:, lanes], preferred_element_type=F32)
        q_ref[:, lanes] = (_rope(qh, cos_t, sin_t) * scale).astype(q_ref.dtype)
        kh = jnp.dot(ckvn, wk_ref[:, lanes], preferred_element_type=F32)
        k_ref[:, lanes] = (kh + kr).astype(k_ref.dtype)


def _odd_in(h2, g, w, qn, wq, kvn, wk, wv, cos_t, sin_t):
    M = h2.shape[0]
    tm = _pick(M, (512, 256, 128))
    row = lambda n: pl.BlockSpec((tm, n), lambda i: (i, 0))
    hw = MLA_HEADS * HEAD_PAD
    vw = MLA_HEADS * MLA_V
    outs = (jax.ShapeDtypeStruct((M, hw), BF16), jax.ShapeDtypeStruct((M, hw), BF16),
            jax.ShapeDtypeStruct((M, vw), BF16), jax.ShapeDtypeStruct((M, vw), BF16))
    return pl.pallas_call(
        _odd_in_kernel,
        out_shape=outs,
        grid=(M // tm,),
        in_specs=[row(D_MODEL), _const_spec(g.shape), _const_spec(w.shape), _const_spec(qn.shape),
                  _const_spec(wq.shape), _const_spec(kvn.shape), _const_spec(wk.shape),
                  _const_spec(wv.shape), row(HEAD_PAD), row(HEAD_PAD)],
        out_specs=(row(hw), row(hw), row(vw), row(vw)),
        compiler_params=_params(("parallel",)),
        name="odd_in",
    )(h2, g, w, qn, wq, kvn, wk, wv, cos_t, sin_t)


def _attn_kernel(q_ref, k_ref, v_ref, o_ref, q_scr, k_scr, vt_scr, qt_scr, s_scr, p_scr,
                 m_scr, l_scr, acc_scr, o_scr, *, nh, n_q):
    T = ATT_TILE
    Tp = q_ref.shape[1]
    Tp2 = n_q * T
    SL = SOFTMAX_SLAB

    q_scr[0:Tp, :] = q_ref[0]
    k_scr[0:Tp, :] = k_ref[0]
    if Tp2 > Tp:
        q_scr[Tp:Tp2, :] = jnp.zeros((Tp2 - Tp, q_scr.shape[1]), q_scr.dtype)
        k_scr[Tp:Tp2, :] = jnp.zeros((Tp2 - Tp, k_scr.shape[1]), k_scr.dtype)
        vt_scr[:, Tp:Tp2] = jnp.zeros((vt_scr.shape[0], Tp2 - Tp), vt_scr.dtype)
    for c in range(0, Tp, LANES):
        vt_scr[:, c:c + LANES] = v_ref[0, c:c + LANES, :].astype(F32).T.astype(vt_scr.dtype)

    def qk(h, koff):
        s_scr[h] = jnp.dot(k_scr[pl.ds(koff, T), h * HEAD_PAD:(h + 1) * HEAD_PAD], qt_scr[h],
                           preferred_element_type=F32)

    def process(h, koff, masked):
        s_ref = s_scr.at[h]
        if masked:
            kk = lax.broadcasted_iota(jnp.int32, (SL, T), 0)
            qq = lax.broadcasted_iota(jnp.int32, (SL, T), 1)
        parts = [None] * 4
        for n, r in enumerate(range(0, T, SL)):
            blk = s_ref[r:r + SL, :]
            if masked:
                blk = jnp.where(kk + r <= qq, blk, NEG_BIG)
                s_ref[r:r + SL, :] = blk
            parts[n % 4] = blk if parts[n % 4] is None else jnp.maximum(parts[n % 4], blk)
        mt = jnp.maximum(jnp.maximum(parts[0], parts[1]), jnp.maximum(parts[2], parts[3]))
        m_old = m_scr[h]
        m_new = jnp.maximum(m_old, jnp.max(mt, axis=0, keepdims=True))
        alpha = jnp.exp2(m_old - m_new)
        m_scr[h] = m_new
        lsum = [None] * 4
        for n, r in enumerate(range(0, T, SL)):
            p = jnp.exp2(s_ref[r:r + SL, :] - m_new)
            lsum[n % 4] = p if lsum[n % 4] is None else lsum[n % 4] + p
            p_scr[h, r:r + SL, :] = p.astype(p_scr.dtype)
        l_scr[h] = alpha * l_scr[h] + ((lsum[0] + lsum[1]) + (lsum[2] + lsum[3]))
        rows = slice(h * MLA_V, (h + 1) * MLA_V)
        pv = jnp.dot(vt_scr[rows, pl.ds(koff, T)], p_scr[h], preferred_element_type=F32)
        acc_scr[rows, :] = alpha * acc_scr[rows, :] + pv

    def q_tile(j, carry):
        qoff = pl.multiple_of(j * T, T)
        for h in range(nh):
            qh = q_scr[pl.ds(qoff, T), h * HEAD_PAD:(h + 1) * HEAD_PAD]
            qt_scr[h] = qh.astype(F32).T.astype(qt_scr.dtype)
        m_scr[...] = jnp.full(m_scr.shape, NEG_BIG, F32)
        l_scr[...] = jnp.zeros_like(l_scr)
        acc_scr[...] = jnp.zeros_like(acc_scr)
        for h in range(nh):
            qk(h, 0)

        def kstep(i, c):
            koff = pl.multiple_of(i * T, T)
            for h in range(nh):
                process(h, koff, False)
                qk(h, koff + T)
            return c

        lax.fori_loop(0, j, kstep, 0)
        for h in range(nh):
            process(h, qoff, True)
        outs = []
        for h in range(nh):
            rows = slice(h * MLA_V, (h + 1) * MLA_V)
            l = jnp.sum(l_scr[h], axis=0, keepdims=True)
            outs.append(acc_scr[rows, :] / l)
        o_scr[pl.ds(qoff, T), :] = jnp.concatenate(outs, axis=0).T.astype(o_scr.dtype)
        return carry

    lax.fori_loop(0, n_q, q_tile, 0)
    o_ref[0] = o_scr[0:Tp, :]


ATT_HEADS_PER_STEP = 4


def _attention(q, k, v):
    B, Tp, _ = q.shape
    nh = ATT_HEADS_PER_STEP
    T = ATT_TILE
    n_q = -(-Tp // T)
    Tp2 = n_q * T
    qk_spec = pl.BlockSpec((1, Tp, nh * HEAD_PAD), lambda b, p: (b, 0, p))
    v_spec = pl.BlockSpec((1, Tp, nh * MLA_V), lambda b, p: (b, 0, p))
    return pl.pallas_call(
        functools.partial(_attn_kernel, nh=nh, n_q=n_q),
        out_shape=jax.ShapeDtypeStruct((B, Tp, MLA_HEADS * MLA_V), BF16),
        grid=(B, MLA_HEADS // nh),
        in_specs=[qk_spec, qk_spec, v_spec],
        out_specs=v_spec,
        scratch_shapes=[pltpu.VMEM((Tp2, nh * HEAD_PAD), BF16),
                        pltpu.VMEM((Tp2, nh * HEAD_PAD), BF16),
                        pltpu.VMEM((nh * MLA_V, Tp2), BF16),
                        pltpu.VMEM((nh, HEAD_PAD, T), BF16),
                        pltpu.VMEM((nh, T, T), F32),
                        pltpu.VMEM((nh, T, T), BF16),
                        pltpu.VMEM((nh, 1, T), F32),
                        pltpu.VMEM((nh, SOFTMAX_SLAB, T), F32),
                        pltpu.VMEM((nh * MLA_V, T), F32),
                        pltpu.VMEM((Tp2, nh * MLA_V), BF16)],
        compiler_params=_params(("parallel", "parallel")),
        name="mla_attention",
    )(q, k, v)


def _odd_out_kernel(h_ref, o_ref, gate_ref, w_ref, fn_ref, out_ref, *, final_norm):
    gate = gate_ref[...].astype(F32)
    x = (o_ref[...].astype(F32) * (gate * _sigmoid(gate))).astype(BF16)
    h = h_ref[...] + jnp.dot(x, w_ref[...], preferred_element_type=F32)
    out_ref[...] = _rms(h, fn_ref[...]) if final_norm else h


def _odd_out(h2, o, gate, w, fn, final_norm):
    M = h2.shape[0]
    tm = _pick(M, (512, 256, 128))
    row = lambda n: pl.BlockSpec((tm, n), lambda i: (i, 0))
    return pl.pallas_call(
        functools.partial(_odd_out_kernel, final_norm=final_norm),
        out_shape=jax.ShapeDtypeStruct((M, D_MODEL), F32),
        grid=(M // tm,),
        in_specs=[row(D_MODEL), row(o.shape[-1]), row(gate.shape[-1]), _const_spec(w.shape),
                  _const_spec(fn.shape)],
        out_specs=row(D_MODEL),
        compiler_params=_params(("parallel",)),
        name="odd_out",
    )(h2, o, gate, w, fn)


def _split_cols(w, sizes):
    idx = np.cumsum(sizes)[:-1].tolist()
    return jnp.split(w, idx, axis=-1)


def _pack_even(w_in, gate_a_w, gate_x_w, alpha_w):
    xa, ga, q, k, v, ad, gb = _split_cols(
        w_in, (RNN_WIDTH, RNN_WIDTH, GLA_HEADS * GLA_DK, GLA_HEADS * GLA_DK, GLA_HEADS * GLA_DV,
               GLA_GATE_RANK, GLA_HEADS * GLA_DV))
    ad = jnp.pad(ad, ((0, 0), (0, LANES - GLA_GATE_RANK)))
    w = jnp.concatenate([xa, ga, q, k, v, gb, ad], axis=-1).astype(BF16)
    wg = jnp.concatenate([gate_a_w, gate_x_w], axis=-1).astype(BF16)
    aw = jnp.pad(alpha_w, ((0, LANES - GLA_GATE_RANK), (0, 0))).astype(BF16)
    return w, wg, aw


def _pack_odd(w_in, w_q_up, w_kv_up):
    cq, ckv, kr, gate = _split_cols(w_in, (MLA_Q_RANK, MLA_KV_RANK, MLA_ROPE, MLA_HEADS * MLA_V))
    kr = jnp.pad(kr, ((0, 0), (MLA_NOPE, HEAD_PAD - MLA_NOPE - MLA_ROPE)))
    w = jnp.concatenate([cq, ckv, kr, gate], axis=-1).astype(BF16)
    qd = MLA_NOPE + MLA_ROPE
    wq = w_q_up.reshape(MLA_Q_RANK, MLA_HEADS, qd)
    wq = jnp.pad(wq, ((0, 0), (0, 0), (0, HEAD_PAD - qd))).reshape(MLA_Q_RANK, MLA_HEADS * HEAD_PAD)
    wkv = w_kv_up.reshape(MLA_KV_RANK, MLA_HEADS, MLA_NOPE + MLA_V)
    wk = jnp.pad(wkv[..., :MLA_NOPE], ((0, 0), (0, 0), (0, HEAD_PAD - MLA_NOPE)))
    wk = wk.reshape(MLA_KV_RANK, MLA_HEADS * HEAD_PAD)
    wv = wkv[..., MLA_NOPE:].reshape(MLA_KV_RANK, MLA_HEADS * MLA_V)
    return w, wq.astype(BF16), wk.astype(BF16), wv.astype(BF16)


def kernel(x, positions, meta_tokens, ab_norm, ab_w_in, ab_conv_w, ab_conv_b, ab_gate_a_w, ab_gate_a_b, ab_gate_x_w, ab_gate_x_b, ab_lru_lambda, ab_alpha_w, ab_alpha_b, ab_gla_norm, ab_w_out, c_norm, c_w_in, c_q_norm, c_w_q_up, c_kv_norm, c_w_kv_up, c_w_out, final_norm):
    B, S, D = x.shape
    T = N_META + S
    Tp = -(-T // LANES) * LANES
    M = B * Tp
    depth = ab_norm.shape[0] + c_norm.shape[0]
    row2 = lambda a: a.reshape(1, -1).astype(F32)

    meta = jnp.broadcast_to(meta_tokens.astype(x.dtype)[None], (B, N_META, D))
    h = jnp.concatenate([meta, x, jnp.zeros((B, Tp - T, D), x.dtype)], axis=1).reshape(M, D)

    meta_pos = jnp.broadcast_to(jnp.arange(N_META, dtype=positions.dtype)[None], (B, N_META))
    pos = jnp.concatenate([meta_pos, positions + N_META,
                           jnp.zeros((B, Tp - T), positions.dtype)], axis=1)
    inv_freq = ROPE_BASE ** (-jnp.arange(0, MLA_ROPE, 2, dtype=F32) / MLA_ROPE)
    invf = jnp.concatenate([jnp.zeros((MLA_NOPE,), F32), inv_freq, inv_freq,
                            jnp.zeros((HEAD_PAD - MLA_NOPE - MLA_ROPE,), F32)]).reshape(1, HEAD_PAD)
    cos_t, sin_t = _rope_tables(pos.astype(F32).reshape(M, 1), invf)

    for layer in range(depth):
        j = layer // 2
        if layer % 2 == 0:
            w, wg, aw = _pack_even(ab_w_in[j], ab_gate_a_w[j], ab_gate_x_w[j], ab_alpha_w[j])
            xa, ga, q, k, v, gb, ad = _even_in(h, row2(ab_norm[j]), w)
            r3 = lambda a: a.reshape(B, Tp, a.shape[-1])
            ya = _rglru(r3(xa), r3(ga), ab_conv_w[j].astype(F32), row2(ab_conv_b[j]), wg,
                        row2(ab_gate_a_b[j]), row2(ab_gate_x_b[j]), row2(ab_lru_lambda[j]))
            ob = _gla(r3(q), r3(k), r3(v), r3(gb), r3(ad), aw, row2(ab_alpha_b[j]),
                      row2(ab_gla_norm[j]))
            h = _even_out(h, ya.reshape(M, -1), ob.reshape(M, -1), ab_w_out[j].astype(BF16))
        else:
            w, wq, wk, wv = _pack_odd(c_w_in[j], c_w_q_up[j], c_w_kv_up[j])
            q, k, v, gate = _odd_in(h, row2(c_norm[j]), w, row2(c_q_norm[j]), wq,
                                    row2(c_kv_norm[j]), wk, wv, cos_t, sin_t)
            r3 = lambda a: a.reshape(B, Tp, a.shape[-1])
            o = _attention(r3(q), r3(k), r3(v))
            last = layer == depth - 1
            h = _odd_out(h, o.reshape(M, -1), gate, c_w_out[j].astype(BF16), row2(final_norm), last)
    if depth % 2 == 1:
        raise NotImplementedError("final RMSNorm is fused into the last (odd) layer")
    return h.reshape(B, Tp, D)[:, N_META:T]
```

```python
import functools

import jax
import jax.numpy as jnp
import numpy as np
from jax import lax
from jax.experimental import pallas as pl
from jax.experimental.pallas import tpu as pltpu

F32 = jnp.float32
BF16 = jnp.bfloat16

D_MODEL = 1024
N_META = 16
EPS = 1e-6
RNN_WIDTH = D_MODEL
RNN_BLOCKS = 8
RNN_BLOCK = RNN_WIDTH // RNN_BLOCKS
CONV_WIDTH = 4
RGLRU_C = 8.0
GLA_HEADS = 4
GLA_DK = 128
GLA_DV = 256
GLA_GATE_RANK = 16
GLA_TAU = 16.0
GLA_CHUNK = 64
MLA_HEADS = 16
MLA_NOPE = 64
MLA_ROPE = 32
MLA_V = 64
MLA_Q_RANK = 512
MLA_KV_RANK = 256
ROPE_BASE = 10000.0

LANES = 128
HEAD_PAD = 128
ATT_TILE = 256
SOFTMAX_SLAB = 16
LOG2_E = 1.4426950408889634
NEG_BIG = -1e30
VMEM_LIMIT = 56 * 1024 * 1024


def _pick(n, candidates):
    for c in candidates:
        if n % c == 0:
            return c
    raise ValueError(f"no tile in {candidates} divides {n}")


def _rms(x, g):
    var = jnp.mean(x * x, axis=-1, keepdims=True)
    return x * lax.rsqrt(var + EPS) * g


def _sigmoid(x):
    return 1.0 / (1.0 + jnp.exp(-x))


def _const_spec(shape):
    nd = len(shape)
    return pl.BlockSpec(shape, lambda *_: (0,) * nd)


def _params(sem):
    return pltpu.CompilerParams(dimension_semantics=sem, vmem_limit_bytes=VMEM_LIMIT)


EVEN_OUT_WIDTHS = (RNN_WIDTH, RNN_WIDTH, GLA_HEADS * GLA_DK, GLA_HEADS * GLA_DK,
                   GLA_HEADS * GLA_DV, GLA_HEADS * GLA_DV, LANES)


def _even_in_kernel(h_ref, g_ref, w_ref, *out_refs):
    xn = _rms(h_ref[...], g_ref[...]).astype(BF16)
    off = 0
    for ref in out_refs:
        n = ref.shape[-1]
        ref[...] = jnp.dot(xn, w_ref[:, off:off + n], preferred_element_type=F32).astype(ref.dtype)
        off += n


def _even_in(h2, g, w):
    M = h2.shape[0]
    tm = _pick(M, (512, 256, 128))
    outs = tuple(jax.ShapeDtypeStruct((M, n), BF16) for n in EVEN_OUT_WIDTHS)
    return pl.pallas_call(
        _even_in_kernel,
        out_shape=outs,
        grid=(M // tm,),
        in_specs=[pl.BlockSpec((tm, D_MODEL), lambda i: (i, 0)),
                  _const_spec((1, D_MODEL)),
                  _const_spec(w.shape)],
        out_specs=tuple(pl.BlockSpec((tm, n), lambda i: (i, 0)) for n in EVEN_OUT_WIDTHS),
        compiler_params=_params(("parallel",)),
        name="even_in",
    )(h2, g, w)


LRU_HIST = 4
LRU_LEAD = 8


def _rglru_kernel(xa_ref, ga_ref, cw_ref, cb_ref, wg_ref, ba_ref, bx_ref, lam_ref, y_ref,
                  xflat, a_s, u_s, h_s, *, tt, pitch, nblk):
    B = xa_ref.shape[0]
    R = B * pitch
    t_idx = pl.program_id(1)

    @pl.when(t_idx == 0)
    def _():
        xflat[...] = jnp.zeros_like(xflat)
        h_s[...] = jnp.zeros_like(h_s)

    @pl.when(t_idx > 0)
    def _():
        for b in range(B):
            base = LRU_LEAD + b * pitch
            xflat[base:base + LRU_HIST, :] = xflat[base + tt:base + tt + LRU_HIST, :]

    for b in range(B):
        base = LRU_LEAD + b * pitch + LRU_HIST
        xflat[base:base + tt, :] = xa_ref[b].astype(F32)

    for j in range(nblk):
        lanes = slice(j * LANES, (j + 1) * LANES)
        y = cb_ref[:, lanes] + xflat[LRU_LEAD - 3:LRU_LEAD - 3 + R, lanes] * cw_ref[0:1, lanes]
        for k in range(1, CONV_WIDTH):
            lo = LRU_LEAD - 3 + k
            y = y + xflat[lo:lo + R, lanes] * cw_ref[k:k + 1, lanes]
        g = jnp.dot(y.astype(BF16), wg_ref[j], preferred_element_type=F32)
        r = _sigmoid(g[:, :LANES] + ba_ref[:, lanes])
        i = _sigmoid(g[:, LANES:] + bx_ref[:, lanes])
        lam = lam_ref[:, lanes]
        softplus_neg_lam = jnp.maximum(-lam, 0.0) + jnp.log(1.0 + jnp.exp(-jnp.abs(lam)))
        a = jnp.exp((-RGLRU_C) * r * softplus_neg_lam)
        a_s[j] = a
        u_s[j] = jnp.sqrt(1.0 - a * a) * (i * y)

    def step(t, hs):
        out = []
        for j in range(nblk):
            idx = pl.ds(LRU_HIST + t, B, stride=pitch)
            h = a_s[j, idx, :] * hs[j] + u_s[j, idx, :]
            u_s[j, idx, :] = h
            out.append(h)
        return tuple(out)

    hs = lax.fori_loop(0, tt, step, tuple(h_s[j] for j in range(nblk)))
    for j in range(nblk):
        h_s[j] = hs[j]

    for j in range(nblk):
        lanes = slice(j * LANES, (j + 1) * LANES)
        for b in range(B):
            base = b * pitch + LRU_HIST
            ga = ga_ref[b, :, lanes].astype(F32)
            y_ref[b, :, lanes] = (u_s[j, base:base + tt, :] * (ga * _sigmoid(ga))).astype(y_ref.dtype)


def _rglru(xa, ga, cw, cb, wg, ba, bx, lam):
    B, Tp, C = xa.shape
    tt = _pick(Tp, (272, 256, 192, 128))
    pitch = tt + LRU_HIST
    cblk = 512
    nblk = cblk // LANES
    blk = pl.BlockSpec((B, tt, cblk), lambda c, t: (0, t, c))
    vec = lambda rows: pl.BlockSpec((rows, cblk), lambda c, t: (0, c))
    return pl.pallas_call(
        functools.partial(_rglru_kernel, tt=tt, pitch=pitch, nblk=nblk),
        out_shape=jax.ShapeDtypeStruct((B, Tp, C), BF16),
        grid=(C // cblk, Tp // tt),
        in_specs=[blk, blk, vec(CONV_WIDTH), vec(1),
                  pl.BlockSpec((nblk, RNN_BLOCK, 2 * RNN_BLOCK), lambda c, t: (c, 0, 0)),
                  vec(1), vec(1), vec(1)],
        out_specs=blk,
        scratch_shapes=[pltpu.VMEM((LRU_LEAD + B * pitch, cblk), F32),
                        pltpu.VMEM((nblk, B * pitch, LANES), F32),
                        pltpu.VMEM((nblk, B * pitch, LANES), F32),
                        pltpu.VMEM((nblk, B, LANES), F32)],
        compiler_params=_params(("parallel", "arbitrary")),
        name="rglru",
    )(xa, ga, cw, cb, wg, ba, bx, lam)


def _gla_kernel(q_ref, k_ref, v_ref, gb_ref, ad_ref, aw_ref, ab_ref, gn_ref, o_ref, st_ref, *, nchunk):
    C = GLA_CHUNK

    @pl.when(pl.program_id(1) == 0)
    def _():
        st_ref[...] = jnp.zeros_like(st_ref)

    row = lax.broadcasted_iota(jnp.int32, (C, C), 0)
    col = lax.broadcasted_iota(jnp.int32, (C, C), 1)
    causal = row >= col
    tril = causal.astype(BF16)
    scale = GLA_DK ** -0.5

    def chunk(c, carry):
        rows = pl.ds(pl.multiple_of(c * C, C), C)
        pre = jnp.dot(ad_ref[0, rows, :], aw_ref[...], preferred_element_type=F32) + ab_ref[...]
        la = (jnp.minimum(pre, 0.0) - jnp.log(1.0 + jnp.exp(-jnp.abs(pre)))) * (1.0 / GLA_TAU)
        la_hi = la.astype(BF16)
        la_lo = (la - la_hi.astype(F32)).astype(BF16)
        bc = (jnp.dot(tril, la_hi, preferred_element_type=F32)
              + jnp.dot(tril, la_lo, preferred_element_type=F32))
        b_last = bc[C - 1:C, :]
        q = q_ref[0, rows, :].astype(F32)
        k = k_ref[0, rows, :].astype(F32)
        q_dec = (q * (jnp.exp(bc) * scale)).astype(BF16)
        k_inv = (k * jnp.exp(-bc)).astype(BF16)
        k_end = (k * jnp.exp(b_last - bc)).astype(BF16)
        decay = jnp.exp(b_last)
        for h in range(GLA_HEADS):
            kl = slice(h * GLA_DK, (h + 1) * GLA_DK)
            vl = slice(h * GLA_DV, (h + 1) * GLA_DV)
            qd, ki, ke = q_dec[:, kl], k_inv[:, kl], k_end[:, kl]
            vh = v_ref[0, rows, vl]
            st = st_ref[h]
            s = lax.dot_general(qd, ki, (((1,), (1,)), ((), ())), preferred_element_type=F32)
            s = jnp.where(causal, s, 0.0).astype(BF16)
            o = (jnp.dot(s, vh, preferred_element_type=F32)
                 + lax.dot_general(qd, st.astype(BF16), (((1,), (1,)), ((), ())),
                                   preferred_element_type=F32))
            vk = lax.dot_general(vh, ke, (((0,), (0,)), ((), ())), preferred_element_type=F32)
            st_ref[h] = st * decay[:, kl] + vk
            gb = gb_ref[0, rows, vl].astype(F32)
            o_ref[0, rows, vl] = (_rms(o, gn_ref[...]) * (gb * _sigmoid(gb))).astype(o_ref.dtype)
        return carry

    lax.fori_loop(0, nchunk, chunk, 0)


def _gla(q, k, v, gb, ad, aw, ab, gn):
    B, Tp, _ = q.shape
    tc = Tp // 2 if (Tp // 2) % GLA_CHUNK == 0 else Tp
    kw, vw = GLA_HEADS * GLA_DK, GLA_HEADS * GLA_DV
    spec = lambda w: pl.BlockSpec((1, tc, w), lambda b, t: (b, t, 0))
    return pl.pallas_call(
        functools.partial(_gla_kernel, nchunk=tc // GLA_CHUNK),
        out_shape=jax.ShapeDtypeStruct((B, Tp, vw), BF16),
        grid=(B, Tp // tc),
        in_specs=[spec(kw), spec(kw), spec(vw), spec(vw), spec(LANES),
                  _const_spec(aw.shape), _const_spec(ab.shape), _const_spec(gn.shape)],
        out_specs=spec(vw),
        scratch_shapes=[pltpu.VMEM((GLA_HEADS, GLA_DV, GLA_DK), F32)],
        compiler_params=_params(("parallel", "arbitrary")),
        name="gla",
    )(q, k, v, gb, ad, aw, ab, gn)


def _even_out_kernel(h_ref, ya_ref, ob_ref, w_ref, o_ref):
    n = ya_ref.shape[-1]
    o_ref[...] = (h_ref[...]
                  + jnp.dot(ya_ref[...], w_ref[:n, :], preferred_element_type=F32)
                  + jnp.dot(ob_ref[...], w_ref[n:, :], preferred_element_type=F32))


def _even_out(h2, ya, ob, w):
    M = h2.shape[0]
    tm = _pick(M, (512, 256, 128))
    row = lambda n: pl.BlockSpec((tm, n), lambda i: (i, 0))
    return pl.pallas_call(
        _even_out_kernel,
        out_shape=jax.ShapeDtypeStruct((M, D_MODEL), F32),
        grid=(M // tm,),
        in_specs=[row(D_MODEL), row(ya.shape[-1]), row(ob.shape[-1]), _const_spec(w.shape)],
        out_specs=row(D_MODEL),
        compiler_params=_params(("parallel",)),
        name="even_out",
    )(h2, ya, ob, w)


def _rope_table_kernel(pos_ref, invf_ref, c_ref, s_ref):
    ang = pos_ref[...] * invf_ref[...]
    lane = lax.broadcasted_iota(jnp.int32, ang.shape, 1)
    half = MLA_ROPE // 2
    c_ref[...] = jnp.where(lane < MLA_NOPE, 1.0, jnp.where(lane < MLA_NOPE + MLA_ROPE, jnp.cos(ang), 0.0))
    sn = jnp.sin(ang)
    s_ref[...] = jnp.where(lane < MLA_NOPE, 0.0,
                           jnp.where(lane < MLA_NOPE + half, -sn,
                                     jnp.where(lane < MLA_NOPE + MLA_ROPE, sn, 0.0)))


def _rope_tables(posf, invf):
    M = posf.shape[0]
    tm = _pick(M, (512, 256, 128))
    out = jax.ShapeDtypeStruct((M, HEAD_PAD), F32)
    return pl.pallas_call(
        _rope_table_kernel,
        out_shape=(out, out),
        grid=(M // tm,),
        in_specs=[pl.BlockSpec((tm, 1), lambda i: (i, 0)), _const_spec((1, HEAD_PAD))],
        out_specs=(pl.BlockSpec((tm, HEAD_PAD), lambda i: (i, 0)),) * 2,
        compiler_params=_params(("parallel",)),
        name="rope_tables",
    )(posf, invf)


def _rope(x, cos_t, sin_t):
    lane = lax.broadcasted_iota(jnp.int32, x.shape, 1)
    half = MLA_ROPE // 2
    partner = jnp.where(lane < MLA_NOPE + half,
                        pltpu.roll(x, HEAD_PAD - half, axis=1),
                        pltpu.roll(x, half, axis=1))
    return x * cos_t + partner * sin_t


def _odd_in_kernel(h_ref, g_ref, w_ref, qn_ref, wq_ref, kvn_ref, wk_ref, wv_ref, cos_ref, sin_ref,
                   q_ref, k_ref, v_ref, gate_ref):
    xn = _rms(h_ref[...], g_ref[...]).astype(BF16)
    c0, c1, c2 = MLA_Q_RANK, MLA_Q_RANK + MLA_KV_RANK, MLA_Q_RANK + MLA_KV_RANK + HEAD_PAD
    gate_ref[...] = jnp.dot(xn, w_ref[:, c2:], preferred_element_type=F32).astype(gate_ref.dtype)
    cq = jnp.dot(xn, w_ref[:, :c0], preferred_element_type=F32)
    ckv = jnp.dot(xn, w_ref[:, c0:c1], preferred_element_type=F32)
    kr = jnp.dot(xn, w_ref[:, c1:c2], preferred_element_type=F32)
    cos_t, sin_t = cos_ref[...], sin_ref[...]
    kr = _rope(kr, cos_t, sin_t)
    cqn = _rms(cq, qn_ref[...]).astype(BF16)
    ckvn = _rms(ckv, kvn_ref[...]).astype(BF16)
    v_ref[...] = jnp.dot(ckvn, wv_ref[...], preferred_element_type=F32).astype(v_ref.dtype)
    scale = LOG2_E * (MLA_NOPE + MLA_ROPE) ** -0.5
    for hd in range(MLA_HEADS):
        lanes = slice(hd * HEAD_PAD, (hd + 1) * HEAD_PAD)
        qh = jnp.dot(cqn, wq_ref[:, lanes], preferred_element_type=F32)
        q_ref[:, lanes] = (_rope(qh, cos_t, sin_t) * scale).astype(q_ref.dtype)
        kh = jnp.dot(ckvn, wk_ref[:, lanes], preferred_element_type=F32)
        k_ref[:, lanes] = (kh + kr).astype(k_ref.dtype)


def _odd_in(h2, g, w, qn, wq, kvn, wk, wv, cos_t, sin_t):
    M = h2.shape[0]
    tm = _pick(M, (512, 256, 128))
    row = lambda n: pl.BlockSpec((tm, n), lambda i: (i, 0))
    hw = MLA_HEADS * HEAD_PAD
    vw = MLA_HEADS * MLA_V
    outs = (jax.ShapeDtypeStruct((M, hw), BF16), jax.ShapeDtypeStruct((M, hw), BF16),
            jax.ShapeDtypeStruct((M, vw), BF16), jax.ShapeDtypeStruct((M, vw), BF16))
    return pl.pallas_call(
        _odd_in_kernel,
        out_shape=outs,
        grid=(M // tm,),
        in_specs=[row(D_MODEL), _const_spec(g.shape), _const_spec(w.shape), _const_spec(qn.shape),
                  _const_spec(wq.shape), _const_spec(kvn.shape), _const_spec(wk.shape),
                  _const_spec(wv.shape), row(HEAD_PAD), row(HEAD_PAD)],
        out_specs=(row(hw), row(hw), row(vw), row(vw)),
        compiler_params=_params(("parallel",)),
        name="odd_in",
    )(h2, g, w, qn, wq, kvn, wk, wv, cos_t, sin_t)


def _attn_kernel(q_ref, k_ref, v_ref, o_ref, k_scr, vt_scr, qt_scr, bias_scr, s_a, s_b, p_scr,
                 m_scr, acc_scr, o_scr, *, nh, n_q):
    T = ATT_TILE
    Tp = q_ref.shape[1]
    Tp2 = n_q * T
    SL = SOFTMAX_SLAB
    VE = ATT_V_EXT

    k_scr[0:Tp, :] = k_ref[0]
    if Tp2 > Tp:
        k_scr[Tp:Tp2, :] = jnp.zeros((Tp2 - Tp, k_scr.shape[1]), k_scr.dtype)
    ones_row = (lax.broadcasted_iota(jnp.int32, (VE - MLA_V, Tp2), 0) == 0).astype(vt_scr.dtype)
    for h in range(nh):
        vt_scr[h * VE + MLA_V:(h + 1) * VE, :] = ones_row
        if Tp2 > Tp:
            vt_scr[h * VE:h * VE + MLA_V, Tp:Tp2] = jnp.zeros((MLA_V, Tp2 - Tp), vt_scr.dtype)
    for c in range(0, Tp, LANES):
        vt = v_ref[0, c:c + LANES, :].astype(F32).T.astype(vt_scr.dtype)
        for h in range(nh):
            vt_scr[h * VE:h * VE + MLA_V, c:c + LANES] = vt[h * MLA_V:(h + 1) * MLA_V, :]
    for j in range(n_q):
        rows = min(T, Tp - j * T)
        for h in range(nh):
            qh = q_ref[0, j * T:j * T + rows, h * HEAD_PAD:(h + 1) * HEAD_PAD]
            qt_scr[j, h, :, 0:rows] = qh.astype(F32).T.astype(qt_scr.dtype)
            if rows < T:
                qt_scr[j, h, :, rows:T] = jnp.zeros((HEAD_PAD, T - rows), qt_scr.dtype)
    kk = lax.broadcasted_iota(jnp.int32, (T, T), 0)
    qq = lax.broadcasted_iota(jnp.int32, (T, T), 1)
    bias_scr[...] = jnp.where(kk <= qq, 0.0, NEG_BIG)

    def qk(s_ref, j, i):
        koff = pl.multiple_of(i * T, T)
        for h in range(nh):
            s_ref[h] = jnp.dot(k_scr[pl.ds(koff, T), h * HEAD_PAD:(h + 1) * HEAD_PAD],
                               qt_scr[j, h], preferred_element_type=F32)

    def softmax_pv(s_ref, j, i, diag):
        koff = pl.multiple_of(i * T, T)
        for h in range(nh):
            def slab(r):
                blk = s_ref[h, r:r + SL, :]
                return blk + bias_scr[r:r + SL, :] if diag else blk
            parts = [None] * 4
            for n, r in enumerate(range(0, T, SL)):
                blk = slab(r)
                parts[n % 4] = blk if parts[n % 4] is None else jnp.maximum(parts[n % 4], blk)
            mt = jnp.maximum(jnp.maximum(parts[0], parts[1]), jnp.maximum(parts[2], parts[3]))
            mt = jnp.max(mt, axis=0, keepdims=True)
            if diag:
                m_new = mt
            else:
                m_old = m_scr[j, h]
                m_new = jnp.maximum(m_old, mt)
                alpha = jnp.exp2(m_old - m_new)
            m_scr[j, h] = m_new
            for r in range(0, T, SL):
                p_scr[h, r:r + SL, :] = jnp.exp2(slab(r) - m_new).astype(p_scr.dtype)
            rows = slice(h * VE, (h + 1) * VE)
            pv = jnp.dot(vt_scr[rows, pl.ds(koff, T)], p_scr[h], preferred_element_type=F32)
            acc_scr[j, rows, :] = pv if diag else alpha * acc_scr[j, rows, :] + pv

    def diag_pair(n, c):
        j = 2 * n
        qk(s_b, j + 1, j + 1)
        softmax_pv(s_a, j, j, True)
        jn = jnp.minimum(j + 2, n_q - 1)
        qk(s_a, jn, jn)
        softmax_pv(s_b, j + 1, j + 1, True)
        return c

    qk(s_a, 0, 0)
    lax.fori_loop(0, n_q // 2, diag_pair, 0)
    if n_q % 2:
        softmax_pv(s_a, n_q - 1, n_q - 1, True)

    def advance(j, i):
        last = i + 1 >= j
        return jnp.where(last, j + 1, j), jnp.where(last, 0, i + 1)

    def lower_pair(n, c):
        j0, i0 = c
        j1, i1 = advance(j0, i0)
        qk(s_b, jnp.minimum(j1, n_q - 1), i1)
        softmax_pv(s_a, j0, i0, False)
        j2, i2 = advance(j1, i1)
        qk(s_a, jnp.minimum(j2, n_q - 1), i2)
        softmax_pv(s_b, j1, i1, False)
        return j2, i2

    n_lower = n_q * (n_q - 1) // 2
    if n_lower:
        qk(s_a, 1, 0)
        j_last, i_last = lax.fori_loop(0, n_lower // 2, lower_pair, (jnp.int32(1), jnp.int32(0)))
        if n_lower % 2:
            softmax_pv(s_a, j_last, i_last, False)

    def finish(j, c):
        outs = []
        for h in range(nh):
            acc = acc_scr[j, h * VE:h * VE + MLA_V, :]
            den = acc_scr[j, h * VE + MLA_V:h * VE + MLA_V + 1, :]
            outs.append(acc / den)
        o_scr[pl.ds(pl.multiple_of(j * T, T), T), :] = jnp.concatenate(outs, axis=0).T.astype(o_scr.dtype)
        return c

    lax.fori_loop(0, n_q, finish, 0)
    o_ref[0] = o_scr[0:Tp, :]


ATT_HEADS_PER_STEP = 4
ATT_V_EXT = MLA_V + 16


def _attention(q, k, v):
    B, Tp, _ = q.shape
    nh = ATT_HEADS_PER_STEP
    T = ATT_TILE
    n_q = -(-Tp // T)
    Tp2 = n_q * T
    qk_spec = pl.BlockSpec((1, Tp, nh * HEAD_PAD), lambda b, p: (b, 0, p))
    v_spec = pl.BlockSpec((1, Tp, nh * MLA_V), lambda b, p: (b, 0, p))
    return pl.pallas_call(
        functools.partial(_attn_kernel, nh=nh, n_q=n_q),
        out_shape=jax.ShapeDtypeStruct((B, Tp, MLA_HEADS * MLA_V), BF16),
        grid=(B, MLA_HEADS // nh),
        in_specs=[qk_spec, qk_spec, v_spec],
        out_specs=v_spec,
        scratch_shapes=[pltpu.VMEM((Tp2, nh * HEAD_PAD), BF16),
                        pltpu.VMEM((nh * ATT_V_EXT, Tp2), BF16),
                        pltpu.VMEM((n_q, nh, HEAD_PAD, T), BF16),
                        pltpu.VMEM((T, T), F32),
                        pltpu.VMEM((nh, T, T), F32),
                        pltpu.VMEM((nh, T, T), F32),
                        pltpu.VMEM((nh, T, T), BF16),
                        pltpu.VMEM((n_q, nh, 1, T), F32),
                        pltpu.VMEM((n_q, nh * ATT_V_EXT, T), F32),
                        pltpu.VMEM((Tp2, nh * MLA_V), BF16)],
        compiler_params=_params(("parallel", "parallel")),
        name="mla_attention",
    )(q, k, v)


def _odd_out_kernel(h_ref, o_ref, gate_ref, w_ref, fn_ref, out_ref, *, final_norm):
    gate = gate_ref[...].astype(F32)
    x = (o_ref[...].astype(F32) * (gate * _sigmoid(gate))).astype(BF16)
    h = h_ref[...] + jnp.dot(x, w_ref[...], preferred_element_type=F32)
    out_ref[...] = _rms(h, fn_ref[...]) if final_norm else h


def _odd_out(h2, o, gate, w, fn, final_norm):
    M = h2.shape[0]
    tm = _pick(M, (512, 256, 128))
    row = lambda n: pl.BlockSpec((tm, n), lambda i: (i, 0))
    return pl.pallas_call(
        functools.partial(_odd_out_kernel, final_norm=final_norm),
        out_shape=jax.ShapeDtypeStruct((M, D_MODEL), F32),
        grid=(M // tm,),
        in_specs=[row(D_MODEL), row(o.shape[-1]), row(gate.shape[-1]), _const_spec(w.shape),
                  _const_spec(fn.shape)],
        out_specs=row(D_MODEL),
        compiler_params=_params(("parallel",)),
        name="odd_out",
    )(h2, o, gate, w, fn)


def _split_cols(w, sizes):
    idx = np.cumsum(sizes)[:-1].tolist()
    return jnp.split(w, idx, axis=-1)


def _pack_even(w_in, gate_a_w, gate_x_w, alpha_w):
    xa, ga, q, k, v, ad, gb = _split_cols(
        w_in, (RNN_WIDTH, RNN_WIDTH, GLA_HEADS * GLA_DK, GLA_HEADS * GLA_DK, GLA_HEADS * GLA_DV,
               GLA_GATE_RANK, GLA_HEADS * GLA_DV))
    ad = jnp.pad(ad, ((0, 0), (0, LANES - GLA_GATE_RANK)))
    w = jnp.concatenate([xa, ga, q, k, v, gb, ad], axis=-1).astype(BF16)
    wg = jnp.concatenate([gate_a_w, gate_x_w], axis=-1).astype(BF16)
    aw = jnp.pad(alpha_w, ((0, LANES - GLA_GATE_RANK), (0, 0))).astype(BF16)
    return w, wg, aw


def _pack_odd(w_in, w_q_up, w_kv_up):
    cq, ckv, kr, gate = _split_cols(w_in, (MLA_Q_RANK, MLA_KV_RANK, MLA_ROPE, MLA_HEADS * MLA_V))
    kr = jnp.pad(kr, ((0, 0), (MLA_NOPE, HEAD_PAD - MLA_NOPE - MLA_ROPE)))
    w = jnp.concatenate([cq, ckv, kr, gate], axis=-1).astype(BF16)
    qd = MLA_NOPE + MLA_ROPE
    wq = w_q_up.reshape(MLA_Q_RANK, MLA_HEADS, qd)
    wq = jnp.pad(wq, ((0, 0), (0, 0), (0, HEAD_PAD - qd))).reshape(MLA_Q_RANK, MLA_HEADS * HEAD_PAD)
    wkv = w_kv_up.reshape(MLA_KV_RANK, MLA_HEADS, MLA_NOPE + MLA_V)
    wk = jnp.pad(wkv[..., :MLA_NOPE], ((0, 0), (0, 0), (0, HEAD_PAD - MLA_NOPE)))
    wk = wk.reshape(MLA_KV_RANK, MLA_HEADS * HEAD_PAD)
    wv = wkv[..., MLA_NOPE:].reshape(MLA_KV_RANK, MLA_HEADS * MLA_V)
    return w, wq.astype(BF16), wk.astype(BF16), wv.astype(BF16)


def kernel(x, positions, meta_tokens, ab_norm, ab_w_in, ab_conv_w, ab_conv_b, ab_gate_a_w, ab_gate_a_b, ab_gate_x_w, ab_gate_x_b, ab_lru_lambda, ab_alpha_w, ab_alpha_b, ab_gla_norm, ab_w_out, c_norm, c_w_in, c_q_norm, c_w_q_up, c_kv_norm, c_w_kv_up, c_w_out, final_norm):
    B, S, D = x.shape
    T = N_META + S
    Tp = -(-T // LANES) * LANES
    M = B * Tp
    depth = ab_norm.shape[0] + c_norm.shape[0]
    row2 = lambda a: a.reshape(1, -1).astype(F32)

    meta = jnp.broadcast_to(meta_tokens.astype(x.dtype)[None], (B, N_META, D))
    h = jnp.concatenate([meta, x, jnp.zeros((B, Tp - T, D), x.dtype)], axis=1).reshape(M, D)

    meta_pos = jnp.broadcast_to(jnp.arange(N_META, dtype=positions.dtype)[None], (B, N_META))
    pos = jnp.concatenate([meta_pos, positions + N_META,
                           jnp.zeros((B, Tp - T), positions.dtype)], axis=1)
    inv_freq = ROPE_BASE ** (-jnp.arange(0, MLA_ROPE, 2, dtype=F32) / MLA_ROPE)
    invf = jnp.concatenate([jnp.zeros((MLA_NOPE,), F32), inv_freq, inv_freq,
                            jnp.zeros((HEAD_PAD - MLA_NOPE - MLA_ROPE,), F32)]).reshape(1, HEAD_PAD)
    cos_t, sin_t = _rope_tables(pos.astype(F32).reshape(M, 1), invf)

    for layer in range(depth):
        j = layer // 2
        if layer % 2 == 0:
            w, wg, aw = _pack_even(ab_w_in[j], ab_gate_a_w[j], ab_gate_x_w[j], ab_alpha_w[j])
            xa, ga, q, k, v, gb, ad = _even_in(h, row2(ab_norm[j]), w)
            r3 = lambda a: a.reshape(B, Tp, a.shape[-1])
            ya = _rglru(r3(xa), r3(ga), ab_conv_w[j].astype(F32), row2(ab_conv_b[j]), wg,
                        row2(ab_gate_a_b[j]), row2(ab_gate_x_b[j]), row2(ab_lru_lambda[j]))
            ob = _gla(r3(q), r3(k), r3(v), r3(gb), r3(ad), aw, row2(ab_alpha_b[j]),
                      row2(ab_gla_norm[j]))
            h = _even_out(h, ya.reshape(M, -1), ob.reshape(M, -1), ab_w_out[j].astype(BF16))
        else:
            w, wq, wk, wv = _pack_odd(c_w_in[j], c_w_q_up[j], c_w_kv_up[j])
            q, k, v, gate = _odd_in(h, row2(c_norm[j]), w, row2(c_q_norm[j]), wq,
                                    row2(c_kv_norm[j]), wk, wv, cos_t, sin_t)
            r3 = lambda a: a.reshape(B, Tp, a.shape[-1])
            o = _attention(r3(q), r3(k), r3(v))
            last = layer == depth - 1
            h = _odd_out(h, o.reshape(M, -1), gate, c_w_out[j].astype(BF16), row2(final_norm), last)
    if depth % 2 == 1:
        raise NotImplementedError("final RMSNorm is fused into the last (odd) layer")
    return h.reshape(B, Tp, D)[:, N_META:T]
```

```python
import functools

import jax
import jax.numpy as jnp
import numpy as np
from jax import lax
from jax.experimental import pallas as pl
from jax.experimental.pallas import tpu as pltpu

F32 = jnp.float32
BF16 = jnp.bfloat16

D_MODEL = 1024
N_META = 16
EPS = 1e-6
RNN_WIDTH = D_MODEL
RNN_BLOCKS = 8
RNN_BLOCK = RNN_WIDTH // RNN_BLOCKS
CONV_WIDTH = 4
RGLRU_C = 8.0
GLA_HEADS = 4
GLA_DK = 128
GLA_DV = 256
GLA_GATE_RANK = 16
GLA_TAU = 16.0
GLA_CHUNK = 64
MLA_HEADS = 16
MLA_NOPE = 64
MLA_ROPE = 32
MLA_V = 64
MLA_Q_RANK = 512
MLA_KV_RANK = 256
ROPE_BASE = 10000.0

LANES = 128
HEAD_PAD = 128
ATT_TILE = 256
SOFTMAX_SLAB = 16
LOG2_E = 1.4426950408889634
NEG_BIG = -1e30
VMEM_LIMIT = 56 * 1024 * 1024


def _pick(n, candidates):
    for c in candidates:
        if n % c == 0:
            return c
    raise ValueError(f"no tile in {candidates} divides {n}")


def _rms(x, g):
    var = jnp.mean(x * x, axis=-1, keepdims=True)
    return x * lax.rsqrt(var + EPS) * g


def _sigmoid(x):
    return 1.0 / (1.0 + jnp.exp(-x))


def _const_spec(shape):
    nd = len(shape)
    return pl.BlockSpec(shape, lambda *_: (0,) * nd)


def _params(sem):
    return pltpu.CompilerParams(dimension_semantics=sem, vmem_limit_bytes=VMEM_LIMIT)


EVEN_OUT_WIDTHS = (RNN_WIDTH, RNN_WIDTH, GLA_HEADS * GLA_DK, GLA_HEADS * GLA_DK,
                   GLA_HEADS * GLA_DV, GLA_HEADS * GLA_DV, LANES)


def _even_in_kernel(h_ref, g_ref, w_ref, *out_refs):
    xn = _rms(h_ref[...], g_ref[...]).astype(BF16)
    off = 0
    for ref in out_refs:
        n = ref.shape[-1]
        ref[...] = jnp.dot(xn, w_ref[:, off:off + n], preferred_element_type=F32).astype(ref.dtype)
        off += n


def _even_in(h2, g, w):
    M = h2.shape[0]
    tm = _pick(M, (512, 256, 128))
    outs = tuple(jax.ShapeDtypeStruct((M, n), BF16) for n in EVEN_OUT_WIDTHS)
    return pl.pallas_call(
        _even_in_kernel,
        out_shape=outs,
        grid=(M // tm,),
        in_specs=[pl.BlockSpec((tm, D_MODEL), lambda i: (i, 0)),
                  _const_spec((1, D_MODEL)),
                  _const_spec(w.shape)],
        out_specs=tuple(pl.BlockSpec((tm, n), lambda i: (i, 0)) for n in EVEN_OUT_WIDTHS),
        compiler_params=_params(("parallel",)),
        name="even_in",
    )(h2, g, w)


def _lru_pitch(tt):
    return tt if (tt // 8) % 2 else tt + 8


def _rglru_kernel(xa_ref, ga_ref, cw_ref, cb_ref, wg_ref, ba_ref, bx_ref, lam_ref, y_ref,
                  flat, u_s, a_s, hist, h_s, *, tt, pitch, nblk):
    B = xa_ref.shape[0]
    t_idx = pl.program_id(1)

    @pl.when(t_idx == 0)
    def _():
        hist[...] = jnp.zeros_like(hist)
        h_s[...] = jnp.zeros_like(h_s)

    for j in range(nblk):
        lanes = slice(j * LANES, (j + 1) * LANES)
        for b in range(B):
            flat[j, b * pitch:b * pitch + tt, :] = xa_ref[b, :, lanes].astype(F32)

    taps = [[cw_ref[k:k + 1, j * LANES:(j + 1) * LANES] for k in range(CONV_WIDTH)] for j in range(nblk)]
    bias = [cb_ref[:, j * LANES:(j + 1) * LANES] for j in range(nblk)]

    def conv_step(t, carry):
        out = []
        for j in range(nblk):
            x1, x2, x3 = carry[j]
            x0 = flat[j, pl.ds(t, B, stride=pitch), :]
            w = taps[j]
            u_s[j, pl.ds(pl.multiple_of(t * B, B), B), :] = (
                bias[j] + w[3] * x0 + w[2] * x1 + w[1] * x2 + w[0] * x3)
            out.append((x0, x1, x2))
        return tuple(out)

    last = lax.fori_loop(0, tt, conv_step,
                         tuple((hist[j, 0], hist[j, 1], hist[j, 2]) for j in range(nblk)), unroll=8)
    for j in range(nblk):
        for k in range(CONV_WIDTH - 1):
            hist[j, k] = last[j][k]

    for j in range(nblk):
        lanes = slice(j * LANES, (j + 1) * LANES)
        y = u_s[j]
        g = jnp.dot(y.astype(BF16), wg_ref[j], preferred_element_type=F32)
        r = _sigmoid(g[:, :LANES] + ba_ref[:, lanes])
        i = _sigmoid(g[:, LANES:] + bx_ref[:, lanes])
        lam = lam_ref[:, lanes]
        softplus_neg_lam = jnp.maximum(-lam, 0.0) + jnp.log(1.0 + jnp.exp(-jnp.abs(lam)))
        a = jnp.exp(r * ((-RGLRU_C) * softplus_neg_lam))
        x = 1.0 - a * a
        a_s[j] = a
        u_s[j] = (x * lax.rsqrt(jnp.maximum(x, 1e-30))) * (i * y)

    def scan_step(t, hs):
        out = []
        for j in range(nblk):
            rows = pl.ds(pl.multiple_of(t * B, B), B)
            h = a_s[j, rows, :] * hs[j] + u_s[j, rows, :]
            flat[j, pl.ds(t, B, stride=pitch), :] = h
            out.append(h)
        return tuple(out)

    hs = lax.fori_loop(0, tt, scan_step, tuple(h_s[j] for j in range(nblk)), unroll=8)
    for j in range(nblk):
        h_s[j] = hs[j]

    for j in range(nblk):
        lanes = slice(j * LANES, (j + 1) * LANES)
        for b in range(B):
            ga = ga_ref[b, :, lanes].astype(F32)
            y_ref[b, :, lanes] = (flat[j, b * pitch:b * pitch + tt, :]
                                  * (ga * _sigmoid(ga))).astype(y_ref.dtype)


def _rglru(xa, ga, cw, cb, wg, ba, bx, lam):
    B, Tp, C = xa.shape
    tt = _pick(Tp, (272, 256, 192, 128))
    pitch = _lru_pitch(tt)
    cblk = 512
    nblk = cblk // LANES
    blk = pl.BlockSpec((B, tt, cblk), lambda c, t: (0, t, c))
    vec = lambda rows: pl.BlockSpec((rows, cblk), lambda c, t: (0, c))
    return pl.pallas_call(
        functools.partial(_rglru_kernel, tt=tt, pitch=pitch, nblk=nblk),
        out_shape=jax.ShapeDtypeStruct((B, Tp, C), BF16),
        grid=(C // cblk, Tp // tt),
        in_specs=[blk, blk, vec(CONV_WIDTH), vec(1),
                  pl.BlockSpec((nblk, RNN_BLOCK, 2 * RNN_BLOCK), lambda c, t: (c, 0, 0)),
                  vec(1), vec(1), vec(1)],
        out_specs=blk,
        scratch_shapes=[pltpu.VMEM((nblk, B * pitch, LANES), F32),
                        pltpu.VMEM((nblk, tt * B, LANES), F32),
                        pltpu.VMEM((nblk, tt * B, LANES), F32),
                        pltpu.VMEM((nblk, CONV_WIDTH - 1, B, LANES), F32),
                        pltpu.VMEM((nblk, B, LANES), F32)],
        compiler_params=_params(("parallel", "arbitrary")),
        name="rglru",
    )(xa, ga, cw, cb, wg, ba, bx, lam)


def _gla_kernel(q_ref, k_ref, v_ref, gb_ref, ad_ref, aw_ref, ab_ref, gn_ref, o_ref, st_ref, *, nchunk, nb):
    C = GLA_CHUNK

    @pl.when(pl.program_id(1) == 0)
    def _():
        st_ref[...] = jnp.zeros_like(st_ref)

    row = lax.broadcasted_iota(jnp.int32, (C, C), 0)
    col = lax.broadcasted_iota(jnp.int32, (C, C), 1)
    causal = row >= col
    tril = causal.astype(BF16)
    scale = GLA_DK ** -0.5

    def chunk(c, carry):
        rows = pl.ds(pl.multiple_of(c * C, C), C)
        for b in range(nb):
            pre = jnp.dot(ad_ref[b, rows, :], aw_ref[...], preferred_element_type=F32) + ab_ref[...]
            la = (jnp.minimum(pre, 0.0) - jnp.log(1.0 + jnp.exp(-jnp.abs(pre)))) * (1.0 / GLA_TAU)
            la_hi = la.astype(BF16)
            la_lo = (la - la_hi.astype(F32)).astype(BF16)
            bc = (jnp.dot(tril, la_hi, preferred_element_type=F32)
                  + jnp.dot(tril, la_lo, preferred_element_type=F32))
            b_last = bc[C - 1:C, :]
            q = q_ref[b, rows, :].astype(F32)
            k = k_ref[b, rows, :].astype(F32)
            q_dec = (q * (jnp.exp(bc) * scale)).astype(BF16)
            k_inv = (k * jnp.exp(-bc)).astype(BF16)
            k_end = (k * jnp.exp(b_last - bc)).astype(BF16)
            decay = jnp.exp(b_last)
            for h in range(GLA_HEADS):
                kl = slice(h * GLA_DK, (h + 1) * GLA_DK)
                vl = slice(h * GLA_DV, (h + 1) * GLA_DV)
                qd, ki, ke = q_dec[:, kl], k_inv[:, kl], k_end[:, kl]
                vh = v_ref[b, rows, vl]
                st = st_ref[b, h]
                s = lax.dot_general(qd, ki, (((1,), (1,)), ((), ())), preferred_element_type=F32)
                s = jnp.where(causal, s, 0.0).astype(BF16)
                o = (jnp.dot(s, vh, preferred_element_type=F32)
                     + lax.dot_general(qd, st.astype(BF16), (((1,), (1,)), ((), ())),
                                       preferred_element_type=F32))
                vk = lax.dot_general(vh, ke, (((0,), (0,)), ((), ())), preferred_element_type=F32)
                st_ref[b, h] = st * decay[:, kl] + vk
                gb = gb_ref[b, rows, vl].astype(F32)
                o_ref[b, rows, vl] = (_rms(o, gn_ref[...]) * (gb * _sigmoid(gb))).astype(o_ref.dtype)
        return carry

    lax.fori_loop(0, nchunk, chunk, 0)


GLA_BATCH_PER_STEP = 2


def _gla(q, k, v, gb, ad, aw, ab, gn):
    B, Tp, _ = q.shape
    nb = GLA_BATCH_PER_STEP if B % GLA_BATCH_PER_STEP == 0 else 1
    tc = Tp // 2 if (Tp // 2) % GLA_CHUNK == 0 else Tp
    kw, vw = GLA_HEADS * GLA_DK, GLA_HEADS * GLA_DV
    spec = lambda w: pl.BlockSpec((nb, tc, w), lambda b, t: (b, t, 0))
    return pl.pallas_call(
        functools.partial(_gla_kernel, nchunk=tc // GLA_CHUNK, nb=nb),
        out_shape=jax.ShapeDtypeStruct((B, Tp, vw), BF16),
        grid=(B // nb, Tp // tc),
        in_specs=[spec(kw), spec(kw), spec(vw), spec(vw), spec(LANES),
                  _const_spec(aw.shape), _const_spec(ab.shape), _const_spec(gn.shape)],
        out_specs=spec(vw),
        scratch_shapes=[pltpu.VMEM((nb, GLA_HEADS, GLA_DV, GLA_DK), F32)],
        compiler_params=_params(("parallel", "arbitrary")),
        name="gla",
    )(q, k, v, gb, ad, aw, ab, gn)


def _even_out_kernel(h_ref, ya_ref, ob_ref, w_ref, o_ref):
    n = ya_ref.shape[-1]
    o_ref[...] = (h_ref[...]
                  + jnp.dot(ya_ref[...], w_ref[:n, :], preferred_element_type=F32)
                  + jnp.dot(ob_ref[...], w_ref[n:, :], preferred_element_type=F32))


def _even_out(h2, ya, ob, w):
    M = h2.shape[0]
    tm = _pick(M, (512, 256, 128))
    row = lambda n: pl.BlockSpec((tm, n), lambda i: (i, 0))
    return pl.pallas_call(
        _even_out_kernel,
        out_shape=jax.ShapeDtypeStruct((M, D_MODEL), F32),
        grid=(M // tm,),
        in_specs=[row(D_MODEL), row(ya.shape[-1]), row(ob.shape[-1]), _const_spec(w.shape)],
        out_specs=row(D_MODEL),
        compiler_params=_params(("parallel",)),
        name="even_out",
    )(h2, ya, ob, w)


ROPE_HALF = MLA_ROPE // 2
ROPE_X2_LANE = HEAD_PAD // 2


def _head_lanes(nope, rope_part):
    zeros = lambda n: jnp.zeros(nope.shape[:-1] + (n,), nope.dtype)
    n1 = ROPE_X2_LANE - ROPE_HALF
    return jnp.concatenate([rope_part[..., :ROPE_HALF], nope[..., :n1], rope_part[..., ROPE_HALF:],
                            nope[..., n1:], zeros(HEAD_PAD - MLA_NOPE - MLA_ROPE)], axis=-1)


def _rope_table_kernel(pos_ref, invf_ref, c_ref, s_ref):
    ang = pos_ref[...] * invf_ref[...]
    lane = lax.broadcasted_iota(jnp.int32, ang.shape, 1)
    c_ref[...] = jnp.cos(ang)
    sn = jnp.sin(ang)
    s_ref[...] = jnp.where(lane < ROPE_X2_LANE, -sn, sn)


def _rope_tables(posf, invf):
    M = posf.shape[0]
    tm = _pick(M, (512, 256, 128))
    out = jax.ShapeDtypeStruct((M, HEAD_PAD), F32)
    return pl.pallas_call(
        _rope_table_kernel,
        out_shape=(out, out),
        grid=(M // tm,),
        in_specs=[pl.BlockSpec((tm, 1), lambda i: (i, 0)), _const_spec((1, HEAD_PAD))],
        out_specs=(pl.BlockSpec((tm, HEAD_PAD), lambda i: (i, 0)),) * 2,
        compiler_params=_params(("parallel",)),
        name="rope_tables",
    )(posf, invf)


def _rope(x, cos_t, sin_t):
    heads = [pltpu.roll(x[:, c:c + HEAD_PAD], ROPE_X2_LANE, axis=1) for c in range(0, x.shape[1], HEAD_PAD)]
    partner = heads[0] if len(heads) == 1 else jnp.concatenate(heads, axis=1)
    return x * cos_t + partner * sin_t


def _odd_in_kernel(h_ref, g_ref, w_ref, qn_ref, wq_ref, kvn_ref, wk_ref, wv_ref, cos_ref, sin_ref,
                   q_ref, k_ref, v_ref, gate_ref):
    xn = _rms(h_ref[...], g_ref[...]).astype(BF16)
    c0, c1, c2 = MLA_Q_RANK, MLA_Q_RANK + MLA_KV_RANK, MLA_Q_RANK + MLA_KV_RANK + HEAD_PAD
    gate_ref[...] = jnp.dot(xn, w_ref[:, c2:], preferred_element_type=F32).astype(gate_ref.dtype)
    cq = jnp.dot(xn, w_ref[:, :c0], preferred_element_type=F32)
    ckv = jnp.dot(xn, w_ref[:, c0:c1], preferred_element_type=F32)
    kr = jnp.dot(xn, w_ref[:, c1:c2], preferred_element_type=F32)
    cos_t, sin_t = cos_ref[...], sin_ref[...]
    kr = _rope(kr, cos_t, sin_t)
    cqn = _rms(cq, qn_ref[...]).astype(BF16)
    ckvn = _rms(ckv, kvn_ref[...]).astype(BF16)
    v_ref[...] = jnp.dot(ckvn, wv_ref[...], preferred_element_type=F32).astype(v_ref.dtype)
    scale = LOG2_E * (MLA_NOPE + MLA_ROPE) ** -0.5
    cos2 = jnp.concatenate([cos_t, cos_t], axis=1) * scale
    sin2 = jnp.concatenate([sin_t, sin_t], axis=1) * scale
    kr2 = jnp.concatenate([kr, kr], axis=1)
    for hp in range(MLA_HEADS // 2):
        lanes = slice(hp * 2 * HEAD_PAD, (hp + 1) * 2 * HEAD_PAD)
        qh = jnp.dot(cqn, wq_ref[:, lanes], preferred_element_type=F32)
        q_ref[:, lanes] = _rope(qh, cos2, sin2).astype(q_ref.dtype)
        kh = jnp.dot(ckvn, wk_ref[:, lanes], preferred_element_type=F32)
        k_ref[:, lanes] = (kh + kr2).astype(k_ref.dtype)


def _odd_in(h2, g, w, qn, wq, kvn, wk, wv, cos_t, sin_t):
    M = h2.shape[0]
    tm = _pick(M, (512, 256, 128))
    row = lambda n: pl.BlockSpec((tm, n), lambda i: (i, 0))
    hw = MLA_HEADS * HEAD_PAD
    vw = MLA_HEADS * MLA_V
    outs = (jax.ShapeDtypeStruct((M, hw), BF16), jax.ShapeDtypeStruct((M, hw), BF16),
            jax.ShapeDtypeStruct((M, vw), BF16), jax.ShapeDtypeStruct((M, vw), BF16))
    return pl.pallas_call(
        _odd_in_kernel,
        out_shape=outs,
        grid=(M // tm,),
        in_specs=[row(D_MODEL), _const_spec(g.shape), _const_spec(w.shape), _const_spec(qn.shape),
                  _const_spec(wq.shape), _const_spec(kvn.shape), _const_spec(wk.shape),
                  _const_spec(wv.shape), row(HEAD_PAD), row(HEAD_PAD)],
        out_specs=(row(hw), row(hw), row(vw), row(vw)),
        compiler_params=_params(("parallel",)),
        name="odd_in",
    )(h2, g, w, qn, wq, kvn, wk, wv, cos_t, sin_t)


def _attn_kernel(q_ref, k_ref, v_ref, o_ref, k_scr, vt_scr, qt_scr, bias_scr, s_a, s_b, p_scr,
                 m_scr, acc_scr, o_scr, *, nh, n_q):
    T = ATT_TILE
    Tp = q_ref.shape[1]
    Tp2 = n_q * T
    SL = SOFTMAX_SLAB
    VE = ATT_V_EXT

    k_scr[0:Tp, :] = k_ref[0]
    if Tp2 > Tp:
        k_scr[Tp:Tp2, :] = jnp.zeros((Tp2 - Tp, k_scr.shape[1]), k_scr.dtype)
    ones_row = (lax.broadcasted_iota(jnp.int32, (VE - MLA_V, Tp2), 0) == 0).astype(vt_scr.dtype)
    for h in range(nh):
        vt_scr[h * VE + MLA_V:(h + 1) * VE, :] = ones_row
        if Tp2 > Tp:
            vt_scr[h * VE:h * VE + MLA_V, Tp:Tp2] = jnp.zeros((MLA_V, Tp2 - Tp), vt_scr.dtype)
    for c in range(0, Tp, LANES):
        vt = v_ref[0, c:c + LANES, :].astype(F32).T.astype(vt_scr.dtype)
        for h in range(nh):
            vt_scr[h * VE:h * VE + MLA_V, c:c + LANES] = vt[h * MLA_V:(h + 1) * MLA_V, :]
    for j in range(n_q):
        rows = min(T, Tp - j * T)
        for h in range(nh):
            qh = q_ref[0, j * T:j * T + rows, h * HEAD_PAD:(h + 1) * HEAD_PAD]
            qt_scr[j, h, :, 0:rows] = qh.astype(F32).T.astype(qt_scr.dtype)
            if rows < T:
                qt_scr[j, h, :, rows:T] = jnp.zeros((HEAD_PAD, T - rows), qt_scr.dtype)
    kk = lax.broadcasted_iota(jnp.int32, (T, T), 0)
    qq = lax.broadcasted_iota(jnp.int32, (T, T), 1)
    bias_scr[...] = jnp.where(kk <= qq, 0.0, NEG_BIG)

    def qk(s_ref, j, i):
        koff = pl.multiple_of(i * T, T)
        for h in range(nh):
            s_ref[h] = jnp.dot(k_scr[pl.ds(koff, T), h * HEAD_PAD:(h + 1) * HEAD_PAD],
                               qt_scr[j, h], preferred_element_type=F32)

    def softmax_pv(s_ref, j, i, diag):
        koff = pl.multiple_of(i * T, T)
        for h in range(nh):
            def slab(r):
                blk = s_ref[h, r:r + SL, :]
                return blk + bias_scr[r:r + SL, :] if diag else blk
            parts = [None] * 4
            for n, r in enumerate(range(0, T, SL)):
                blk = slab(r)
                parts[n % 4] = blk if parts[n % 4] is None else jnp.maximum(parts[n % 4], blk)
            mt = jnp.maximum(jnp.maximum(parts[0], parts[1]), jnp.maximum(parts[2], parts[3]))
            mt = jnp.max(mt, axis=0, keepdims=True)
            if diag:
                m_new = mt
            else:
                m_old = m_scr[j, h]
                m_new = jnp.maximum(m_old, mt)
                alpha = jnp.exp2(m_old - m_new)
            m_scr[j, h] = m_new
            for r in range(0, T, SL):
                p_scr[h, r:r + SL, :] = jnp.exp2(slab(r) - m_new).astype(p_scr.dtype)
            rows = slice(h * VE, (h + 1) * VE)
            pv = jnp.dot(vt_scr[rows, pl.ds(koff, T)], p_scr[h], preferred_element_type=F32)
            acc_scr[j, rows, :] = pv if diag else alpha * acc_scr[j, rows, :] + pv

    def diag_pair(n, c):
        j = 2 * n
        qk(s_b, j + 1, j + 1)
        softmax_pv(s_a, j, j, True)
        jn = jnp.minimum(j + 2, n_q - 1)
        qk(s_a, jn, jn)
        softmax_pv(s_b, j + 1, j + 1, True)
        return c

    qk(s_a, 0, 0)
    lax.fori_loop(0, n_q // 2, diag_pair, 0)
    if n_q % 2:
        softmax_pv(s_a, n_q - 1, n_q - 1, True)

    def advance(j, i):
        last = i + 1 >= j
        return jnp.where(last, j + 1, j), jnp.where(last, 0, i + 1)

    def lower_pair(n, c):
        j0, i0 = c
        j1, i1 = advance(j0, i0)
        qk(s_b, jnp.minimum(j1, n_q - 1), i1)
        softmax_pv(s_a, j0, i0, False)
        j2, i2 = advance(j1, i1)
        qk(s_a, jnp.minimum(j2, n_q - 1), i2)
        softmax_pv(s_b, j1, i1, False)
        return j2, i2

    n_lower = n_q * (n_q - 1) // 2
    if n_lower:
        qk(s_a, 1, 0)
        j_last, i_last = lax.fori_loop(0, n_lower // 2, lower_pair, (jnp.int32(1), jnp.int32(0)))
        if n_lower % 2:
            softmax_pv(s_a, j_last, i_last, False)

    def finish(j, c):
        outs = []
        for h in range(nh):
            acc = acc_scr[j, h * VE:h * VE + MLA_V, :]
            den = acc_scr[j, h * VE + MLA_V:h * VE + MLA_V + 1, :]
            outs.append(acc / den)
        o_scr[pl.ds(pl.multiple_of(j * T, T), T), :] = jnp.concatenate(outs, axis=0).T.astype(o_scr.dtype)
        return c

    lax.fori_loop(0, n_q, finish, 0)
    o_ref[0] = o_scr[0:Tp, :]


ATT_HEADS_PER_STEP = 4
ATT_V_EXT = MLA_V + 16


def _attention(q, k, v):
    B, Tp, _ = q.shape
    nh = ATT_HEADS_PER_STEP
    T = ATT_TILE
    n_q = -(-Tp // T)
    Tp2 = n_q * T
    qk_spec = pl.BlockSpec((1, Tp, nh * HEAD_PAD), lambda b, p: (b, 0, p))
    v_spec = pl.BlockSpec((1, Tp, nh * MLA_V), lambda b, p: (b, 0, p))
    return pl.pallas_call(
        functools.partial(_attn_kernel, nh=nh, n_q=n_q),
        out_shape=jax.ShapeDtypeStruct((B, Tp, MLA_HEADS * MLA_V), BF16),
        grid=(B, MLA_HEADS // nh),
        in_specs=[qk_spec, qk_spec, v_spec],
        out_specs=v_spec,
        scratch_shapes=[pltpu.VMEM((Tp2, nh * HEAD_PAD), BF16),
                        pltpu.VMEM((nh * ATT_V_EXT, Tp2), BF16),
                        pltpu.VMEM((n_q, nh, HEAD_PAD, T), BF16),
                        pltpu.VMEM((T, T), F32),
                        pltpu.VMEM((nh, T, T), F32),
                        pltpu.VMEM((nh, T, T), F32),
                        pltpu.VMEM((nh, T, T), BF16),
                        pltpu.VMEM((n_q, nh, 1, T), F32),
                        pltpu.VMEM((n_q, nh * ATT_V_EXT, T), F32),
                        pltpu.VMEM((Tp2, nh * MLA_V), BF16)],
        compiler_params=_params(("parallel", "parallel")),
        name="mla_attention",
    )(q, k, v)


def _odd_out_kernel(h_ref, o_ref, gate_ref, w_ref, fn_ref, out_ref, *, final_norm):
    gate = gate_ref[...].astype(F32)
    x = (o_ref[...].astype(F32) * (gate * _sigmoid(gate))).astype(BF16)
    h = h_ref[...] + jnp.dot(x, w_ref[...], preferred_element_type=F32)
    out_ref[...] = _rms(h, fn_ref[...]) if final_norm else h


def _odd_out(h2, o, gate, w, fn, final_norm):
    M = h2.shape[0]
    tm = _pick(M, (512, 256, 128))
    row = lambda n: pl.BlockSpec((tm, n), lambda i: (i, 0))
    return pl.pallas_call(
        functools.partial(_odd_out_kernel, final_norm=final_norm),
        out_shape=jax.ShapeDtypeStruct((M, D_MODEL), F32),
        grid=(M // tm,),
        in_specs=[row(D_MODEL), row(o.shape[-1]), row(gate.shape[-1]), _const_spec(w.shape),
                  _const_spec(fn.shape)],
        out_specs=row(D_MODEL),
        compiler_params=_params(("parallel",)),
        name="odd_out",
    )(h2, o, gate, w, fn)


def _split_cols(w, sizes):
    idx = np.cumsum(sizes)[:-1].tolist()
    return jnp.split(w, idx, axis=-1)


def _pack_even(w_in, gate_a_w, gate_x_w, alpha_w):
    xa, ga, q, k, v, ad, gb = _split_cols(
        w_in, (RNN_WIDTH, RNN_WIDTH, GLA_HEADS * GLA_DK, GLA_HEADS * GLA_DK, GLA_HEADS * GLA_DV,
               GLA_GATE_RANK, GLA_HEADS * GLA_DV))
    ad = jnp.pad(ad, ((0, 0), (0, LANES - GLA_GATE_RANK)))
    w = jnp.concatenate([xa, ga, q, k, v, gb, ad], axis=-1).astype(BF16)
    wg = jnp.concatenate([gate_a_w, gate_x_w], axis=-1).astype(BF16)
    aw = jnp.pad(alpha_w, ((0, LANES - GLA_GATE_RANK), (0, 0))).astype(BF16)
    return w, wg, aw


def _pack_odd(w_in, w_q_up, w_kv_up):
    cq, ckv, kr, gate = _split_cols(w_in, (MLA_Q_RANK, MLA_KV_RANK, MLA_ROPE, MLA_HEADS * MLA_V))
    kr = _head_lanes(jnp.zeros((kr.shape[0], MLA_NOPE), kr.dtype), kr)
    w = jnp.concatenate([cq, ckv, kr, gate], axis=-1).astype(BF16)
    wq = w_q_up.reshape(MLA_Q_RANK, MLA_HEADS, MLA_NOPE + MLA_ROPE)
    wq = _head_lanes(wq[..., :MLA_NOPE], wq[..., MLA_NOPE:]).reshape(MLA_Q_RANK, MLA_HEADS * HEAD_PAD)
    wkv = w_kv_up.reshape(MLA_KV_RANK, MLA_HEADS, MLA_NOPE + MLA_V)
    wk = _head_lanes(wkv[..., :MLA_NOPE], jnp.zeros(wkv.shape[:-1] + (MLA_ROPE,), wkv.dtype))
    wk = wk.reshape(MLA_KV_RANK, MLA_HEADS * HEAD_PAD)
    wv = wkv[..., MLA_NOPE:].reshape(MLA_KV_RANK, MLA_HEADS * MLA_V)
    return w, wq.astype(BF16), wk.astype(BF16), wv.astype(BF16)


def kernel(x, positions, meta_tokens, ab_norm, ab_w_in, ab_conv_w, ab_conv_b, ab_gate_a_w, ab_gate_a_b, ab_gate_x_w, ab_gate_x_b, ab_lru_lambda, ab_alpha_w, ab_alpha_b, ab_gla_norm, ab_w_out, c_norm, c_w_in, c_q_norm, c_w_q_up, c_kv_norm, c_w_kv_up, c_w_out, final_norm):
    B, S, D = x.shape
    T = N_META + S
    Tp = -(-T // LANES) * LANES
    M = B * Tp
    depth = ab_norm.shape[0] + c_norm.shape[0]
    row2 = lambda a: a.reshape(1, -1).astype(F32)

    meta = jnp.broadcast_to(meta_tokens.astype(x.dtype)[None], (B, N_META, D))
    h = jnp.concatenate([meta, x, jnp.zeros((B, Tp - T, D), x.dtype)], axis=1).reshape(M, D)

    meta_pos = jnp.broadcast_to(jnp.arange(N_META, dtype=positions.dtype)[None], (B, N_META))
    pos = jnp.concatenate([meta_pos, positions + N_META,
                           jnp.zeros((B, Tp - T), positions.dtype)], axis=1)
    inv_freq = ROPE_BASE ** (-jnp.arange(0, MLA_ROPE, 2, dtype=F32) / MLA_ROPE)
    invf = _head_lanes(jnp.zeros((1, MLA_NOPE), F32), jnp.concatenate([inv_freq, inv_freq])[None])
    cos_t, sin_t = _rope_tables(pos.astype(F32).reshape(M, 1), invf)

    for layer in range(depth):
        j = layer // 2
        if layer % 2 == 0:
            w, wg, aw = _pack_even(ab_w_in[j], ab_gate_a_w[j], ab_gate_x_w[j], ab_alpha_w[j])
            xa, ga, q, k, v, gb, ad = _even_in(h, row2(ab_norm[j]), w)
            r3 = lambda a: a.reshape(B, Tp, a.shape[-1])
            ya = _rglru(r3(xa), r3(ga), ab_conv_w[j].astype(F32), row2(ab_conv_b[j]), wg,
                        row2(ab_gate_a_b[j]), row2(ab_gate_x_b[j]), row2(ab_lru_lambda[j]))
            ob = _gla(r3(q), r3(k), r3(v), r3(gb), r3(ad), aw, row2(ab_alpha_b[j]),
                      row2(ab_gla_norm[j]))
            h = _even_out(h, ya.reshape(M, -1), ob.reshape(M, -1), ab_w_out[j].astype(BF16))
        else:
            w, wq, wk, wv = _pack_odd(c_w_in[j], c_w_q_up[j], c_w_kv_up[j])
            q, k, v, gate = _odd_in(h, row2(c_norm[j]), w, row2(c_q_norm[j]), wq,
                                    row2(c_kv_norm[j]), wk, wv, cos_t, sin_t)
            r3 = lambda a: a.reshape(B, Tp, a.shape[-1])
            o = _attention(r3(q), r3(k), r3(v))
            last = layer == depth - 1
            h = _odd_out(h, o.reshape(M, -1), gate, c_w_out[j].astype(BF16), row2(final_norm), last)
    if depth % 2 == 1:
        raise NotImplementedError("final RMSNorm is fused into the last (odd) layer")
    return h.reshape(B, Tp, D)[:, N_META:T]
```

```python
import functools

import jax
import jax.numpy as jnp
import numpy as np
from jax import lax
from jax.experimental import pallas as pl
from jax.experimental.pallas import tpu as pltpu

F32 = jnp.float32
BF16 = jnp.bfloat16

D_MODEL = 1024
N_META = 16
EPS = 1e-6
RNN_WIDTH = D_MODEL
RNN_BLOCKS = 8
RNN_BLOCK = RNN_WIDTH // RNN_BLOCKS
CONV_WIDTH = 4
RGLRU_C = 8.0
GLA_HEADS = 4
GLA_DK = 128
GLA_DV = 256
GLA_GATE_RANK = 16
GLA_TAU = 16.0
GLA_CHUNK = 64
MLA_HEADS = 16
MLA_NOPE = 64
MLA_ROPE = 32
MLA_V = 64
MLA_Q_RANK = 512
MLA_KV_RANK = 256
ROPE_BASE = 10000.0

LANES = 128
HEAD_PAD = 128
ATT_TILE = 256
SOFTMAX_SLAB = 16
LOG2_E = 1.4426950408889634
NEG_BIG = -1e30
VMEM_LIMIT = 56 * 1024 * 1024


def _pick(n, candidates):
    for c in candidates:
        if n % c == 0:
            return c
    raise ValueError(f"no tile in {candidates} divides {n}")


def _rms(x, g):
    var = jnp.mean(x * x, axis=-1, keepdims=True)
    return x * lax.rsqrt(var + EPS) * g


def _sigmoid(x):
    return 1.0 / (1.0 + jnp.exp(-x))


def _const_spec(shape):
    nd = len(shape)
    return pl.BlockSpec(shape, lambda *_: (0,) * nd)


def _params(sem):
    return pltpu.CompilerParams(dimension_semantics=sem, vmem_limit_bytes=VMEM_LIMIT)


EVEN_OUT_WIDTHS = (RNN_WIDTH, RNN_WIDTH, GLA_HEADS * GLA_DK, GLA_HEADS * GLA_DK,
                   GLA_HEADS * GLA_DV, GLA_HEADS * GLA_DV, LANES)


def _even_in_kernel(h_ref, g_ref, w_ref, *out_refs):
    xn = _rms(h_ref[...], g_ref[...]).astype(BF16)
    off = 0
    for ref in out_refs:
        n = ref.shape[-1]
        ref[...] = jnp.dot(xn, w_ref[:, off:off + n], preferred_element_type=F32).astype(ref.dtype)
        off += n


def _even_in(h2, g, w):
    M = h2.shape[0]
    tm = _pick(M, (512, 256, 128))
    outs = tuple(jax.ShapeDtypeStruct((M, n), BF16) for n in EVEN_OUT_WIDTHS)
    return pl.pallas_call(
        _even_in_kernel,
        out_shape=outs,
        grid=(M // tm,),
        in_specs=[pl.BlockSpec((tm, D_MODEL), lambda i: (i, 0)),
                  _const_spec((1, D_MODEL)),
                  _const_spec(w.shape)],
        out_specs=tuple(pl.BlockSpec((tm, n), lambda i: (i, 0)) for n in EVEN_OUT_WIDTHS),
        compiler_params=_params(("parallel",)),
        name="even_in",
    )(h2, g, w)


def _lru_pitch(tt):
    return tt if (tt // 8) % 2 else tt + 8


def _rglru_kernel(xa_ref, ga_ref, cw_ref, cb_ref, wg_ref, ba_ref, bx_ref, lam_ref, y_ref,
                  flat, u_s, a_s, hist, h_s, *, tt, pitch, nblk):
    B = xa_ref.shape[0]
    t_idx = pl.program_id(1)

    @pl.when(t_idx == 0)
    def _():
        hist[...] = jnp.zeros_like(hist)
        h_s[...] = jnp.zeros_like(h_s)

    for j in range(nblk):
        lanes = slice(j * LANES, (j + 1) * LANES)
        for b in range(B):
            flat[j, b * pitch:b * pitch + tt, :] = xa_ref[b, :, lanes].astype(F32)

    taps = [[cw_ref[k:k + 1, j * LANES:(j + 1) * LANES] for k in range(CONV_WIDTH)] for j in range(nblk)]
    bias = [cb_ref[:, j * LANES:(j + 1) * LANES] for j in range(nblk)]

    def conv_step(t, carry):
        out = []
        for j in range(nblk):
            x1, x2, x3 = carry[j]
            x0 = flat[j, pl.ds(t, B, stride=pitch), :]
            w = taps[j]
            u_s[j, pl.ds(pl.multiple_of(t * B, B), B), :] = (
                bias[j] + w[3] * x0 + w[2] * x1 + w[1] * x2 + w[0] * x3)
            out.append((x0, x1, x2))
        return tuple(out)

    last = lax.fori_loop(0, tt, conv_step,
                         tuple((hist[j, 0], hist[j, 1], hist[j, 2]) for j in range(nblk)), unroll=8)
    for j in range(nblk):
        for k in range(CONV_WIDTH - 1):
            hist[j, k] = last[j][k]

    for j in range(nblk):
        lanes = slice(j * LANES, (j + 1) * LANES)
        y = u_s[j]
        g = jnp.dot(y.astype(BF16), wg_ref[j], preferred_element_type=F32)
        r = _sigmoid(g[:, :LANES] + ba_ref[:, lanes])
        i = _sigmoid(g[:, LANES:] + bx_ref[:, lanes])
        lam = lam_ref[:, lanes]
        softplus_neg_lam = jnp.maximum(-lam, 0.0) + jnp.log(1.0 + jnp.exp(-jnp.abs(lam)))
        a = jnp.exp(r * ((-RGLRU_C) * softplus_neg_lam))
        x = 1.0 - a * a
        a_s[j] = a
        u_s[j] = (x * lax.rsqrt(jnp.maximum(x, 1e-30))) * (i * y)

    def scan_step(t, hs):
        out = []
        for j in range(nblk):
            rows = pl.ds(pl.multiple_of(t * B, B), B)
            h = a_s[j, rows, :] * hs[j] + u_s[j, rows, :]
            flat[j, pl.ds(t, B, stride=pitch), :] = h
            out.append(h)
        return tuple(out)

    hs = lax.fori_loop(0, tt, scan_step, tuple(h_s[j] for j in range(nblk)), unroll=8)
    for j in range(nblk):
        h_s[j] = hs[j]

    for j in range(nblk):
        lanes = slice(j * LANES, (j + 1) * LANES)
        for b in range(B):
            ga = ga_ref[b, :, lanes].astype(F32)
            y_ref[b, :, lanes] = (flat[j, b * pitch:b * pitch + tt, :]
                                  * (ga * _sigmoid(ga))).astype(y_ref.dtype)


def _rglru(xa, ga, cw, cb, wg, ba, bx, lam):
    B, Tp, C = xa.shape
    tt = _pick(Tp, (272, 256, 192, 128))
    pitch = _lru_pitch(tt)
    cblk = 512
    nblk = cblk // LANES
    blk = pl.BlockSpec((B, tt, cblk), lambda c, t: (0, t, c))
    vec = lambda rows: pl.BlockSpec((rows, cblk), lambda c, t: (0, c))
    return pl.pallas_call(
        functools.partial(_rglru_kernel, tt=tt, pitch=pitch, nblk=nblk),
        out_shape=jax.ShapeDtypeStruct((B, Tp, C), BF16),
        grid=(C // cblk, Tp // tt),
        in_specs=[blk, blk, vec(CONV_WIDTH), vec(1),
                  pl.BlockSpec((nblk, RNN_BLOCK, 2 * RNN_BLOCK), lambda c, t: (c, 0, 0)),
                  vec(1), vec(1), vec(1)],
        out_specs=blk,
        scratch_shapes=[pltpu.VMEM((nblk, B * pitch, LANES), F32),
                        pltpu.VMEM((nblk, tt * B, LANES), F32),
                        pltpu.VMEM((nblk, tt * B, LANES), F32),
                        pltpu.VMEM((nblk, CONV_WIDTH - 1, B, LANES), F32),
                        pltpu.VMEM((nblk, B, LANES), F32)],
        compiler_params=_params(("parallel", "arbitrary")),
        name="rglru",
    )(xa, ga, cw, cb, wg, ba, bx, lam)


def _gla_kernel(q_ref, k_ref, v_ref, gb_ref, ad_ref, aw_ref, ab_ref, gn_ref, o_ref, st_ref, *, nchunk, nb):
    C = GLA_CHUNK

    @pl.when(pl.program_id(1) == 0)
    def _():
        st_ref[...] = jnp.zeros_like(st_ref)

    row = lax.broadcasted_iota(jnp.int32, (C, C), 0)
    col = lax.broadcasted_iota(jnp.int32, (C, C), 1)
    causal = row >= col
    tril = causal.astype(BF16)
    scale = GLA_DK ** -0.5

    def chunk(c, carry):
        rows = pl.ds(pl.multiple_of(c * C, C), C)
        for b in range(nb):
            pre = jnp.dot(ad_ref[b, rows, :], aw_ref[...], preferred_element_type=F32) + ab_ref[...]
            la = (jnp.minimum(pre, 0.0) - jnp.log(1.0 + jnp.exp(-jnp.abs(pre)))) * (1.0 / GLA_TAU)
            la_hi = la.astype(BF16)
            la_lo = (la - la_hi.astype(F32)).astype(BF16)
            bc = (jnp.dot(tril, la_hi, preferred_element_type=F32)
                  + jnp.dot(tril, la_lo, preferred_element_type=F32))
            b_last = bc[C - 1:C, :]
            q = q_ref[b, rows, :].astype(F32)
            k = k_ref[b, rows, :].astype(F32)
            q_dec = (q * (jnp.exp(bc) * scale)).astype(BF16)
            k_inv = (k * jnp.exp(-bc)).astype(BF16)
            k_end = (k * jnp.exp(b_last - bc)).astype(BF16)
            decay = jnp.exp(b_last)
            for h in range(GLA_HEADS):
                kl = slice(h * GLA_DK, (h + 1) * GLA_DK)
                vl = slice(h * GLA_DV, (h + 1) * GLA_DV)
                qd, ki, ke = q_dec[:, kl], k_inv[:, kl], k_end[:, kl]
                vh = v_ref[b, rows, vl]
                st = st_ref[b, h]
                s = lax.dot_general(qd, ki, (((1,), (1,)), ((), ())), preferred_element_type=F32)
                s = jnp.where(causal, s, 0.0).astype(BF16)
                o = (jnp.dot(s, vh, preferred_element_type=F32)
                     + lax.dot_general(qd, st.astype(BF16), (((1,), (1,)), ((), ())),
                                       preferred_element_type=F32))
                vk = lax.dot_general(vh, ke, (((0,), (0,)), ((), ())), preferred_element_type=F32)
                st_ref[b, h] = st * decay[:, kl] + vk
                gb = gb_ref[b, rows, vl].astype(F32)
                o_ref[b, rows, vl] = (_rms(o, gn_ref[...]) * (gb * _sigmoid(gb))).astype(o_ref.dtype)
        return carry

    lax.fori_loop(0, nchunk, chunk, 0)


GLA_BATCH_PER_STEP = 8
GLA_TIME_TILE = 128


def _gla(q, k, v, gb, ad, aw, ab, gn):
    B, Tp, _ = q.shape
    nb = GLA_BATCH_PER_STEP if B % GLA_BATCH_PER_STEP == 0 else 1
    tc = GLA_TIME_TILE
    kw, vw = GLA_HEADS * GLA_DK, GLA_HEADS * GLA_DV
    spec = lambda w: pl.BlockSpec((nb, tc, w), lambda b, t: (b, t, 0))
    return pl.pallas_call(
        functools.partial(_gla_kernel, nchunk=tc // GLA_CHUNK, nb=nb),
        out_shape=jax.ShapeDtypeStruct((B, Tp, vw), BF16),
        grid=(B // nb, Tp // tc),
        in_specs=[spec(kw), spec(kw), spec(vw), spec(vw), spec(LANES),
                  _const_spec(aw.shape), _const_spec(ab.shape), _const_spec(gn.shape)],
        out_specs=spec(vw),
        scratch_shapes=[pltpu.VMEM((nb, GLA_HEADS, GLA_DV, GLA_DK), F32)],
        compiler_params=_params(("parallel", "arbitrary")),
        name="gla",
    )(q, k, v, gb, ad, aw, ab, gn)


def _even_out_kernel(h_ref, ya_ref, ob_ref, w_ref, o_ref):
    n = ya_ref.shape[-1]
    o_ref[...] = (h_ref[...]
                  + jnp.dot(ya_ref[...], w_ref[:n, :], preferred_element_type=F32)
                  + jnp.dot(ob_ref[...], w_ref[n:, :], preferred_element_type=F32))


def _even_out(h2, ya, ob, w):
    M = h2.shape[0]
    tm = _pick(M, (512, 256, 128))
    row = lambda n: pl.BlockSpec((tm, n), lambda i: (i, 0))
    return pl.pallas_call(
        _even_out_kernel,
        out_shape=jax.ShapeDtypeStruct((M, D_MODEL), F32),
        grid=(M // tm,),
        in_specs=[row(D_MODEL), row(ya.shape[-1]), row(ob.shape[-1]), _const_spec(w.shape)],
        out_specs=row(D_MODEL),
        compiler_params=_params(("parallel",)),
        name="even_out",
    )(h2, ya, ob, w)


ROPE_HALF = MLA_ROPE // 2
ROPE_X2_LANE = HEAD_PAD // 2


def _head_lanes(nope, rope_part):
    zeros = lambda n: jnp.zeros(nope.shape[:-1] + (n,), nope.dtype)
    n1 = ROPE_X2_LANE - ROPE_HALF
    return jnp.concatenate([rope_part[..., :ROPE_HALF], nope[..., :n1], rope_part[..., ROPE_HALF:],
                            nope[..., n1:], zeros(HEAD_PAD - MLA_NOPE - MLA_ROPE)], axis=-1)


def _rope_table_kernel(pos_ref, invf_ref, c_ref, s_ref):
    ang = pos_ref[...] * invf_ref[...]
    lane = lax.broadcasted_iota(jnp.int32, ang.shape, 1)
    c_ref[...] = jnp.cos(ang)
    sn = jnp.sin(ang)
    s_ref[...] = jnp.where(lane < ROPE_X2_LANE, -sn, sn)


def _rope_tables(posf, invf):
    M = posf.shape[0]
    tm = _pick(M, (512, 256, 128))
    out = jax.ShapeDtypeStruct((M, HEAD_PAD), F32)
    return pl.pallas_call(
        _rope_table_kernel,
        out_shape=(out, out),
        grid=(M // tm,),
        in_specs=[pl.BlockSpec((tm, 1), lambda i: (i, 0)), _const_spec((1, HEAD_PAD))],
        out_specs=(pl.BlockSpec((tm, HEAD_PAD), lambda i: (i, 0)),) * 2,
        compiler_params=_params(("parallel",)),
        name="rope_tables",
    )(posf, invf)


def _rope(x, cos_t, sin_t):
    heads = [pltpu.roll(x[:, c:c + HEAD_PAD], ROPE_X2_LANE, axis=1) for c in range(0, x.shape[1], HEAD_PAD)]
    partner = heads[0] if len(heads) == 1 else jnp.concatenate(heads, axis=1)
    return x * cos_t + partner * sin_t


def _odd_in_kernel(h_ref, g_ref, w_ref, qn_ref, wq_ref, kvn_ref, wk_ref, wv_ref, cos_ref, sin_ref,
                   q_ref, k_ref, v_ref, gate_ref):
    xn = _rms(h_ref[...], g_ref[...]).astype(BF16)
    c0, c1, c2 = MLA_Q_RANK, MLA_Q_RANK + MLA_KV_RANK, MLA_Q_RANK + MLA_KV_RANK + HEAD_PAD
    gate_ref[...] = jnp.dot(xn, w_ref[:, c2:], preferred_element_type=F32).astype(gate_ref.dtype)
    cq = jnp.dot(xn, w_ref[:, :c0], preferred_element_type=F32)
    ckv = jnp.dot(xn, w_ref[:, c0:c1], preferred_element_type=F32)
    kr = jnp.dot(xn, w_ref[:, c1:c2], preferred_element_type=F32)
    cos_t, sin_t = cos_ref[...], sin_ref[...]
    kr = _rope(kr, cos_t, sin_t)
    cqn = _rms(cq, qn_ref[...]).astype(BF16)
    ckvn = _rms(ckv, kvn_ref[...]).astype(BF16)
    v_ref[...] = jnp.dot(ckvn, wv_ref[...], preferred_element_type=F32).astype(v_ref.dtype)
    scale = LOG2_E * (MLA_NOPE + MLA_ROPE) ** -0.5
    cos2 = jnp.concatenate([cos_t, cos_t], axis=1) * scale
    sin2 = jnp.concatenate([sin_t, sin_t], axis=1) * scale
    kr2 = jnp.concatenate([kr, kr], axis=1)
    for hp in range(MLA_HEADS // 2):
        lanes = slice(hp * 2 * HEAD_PAD, (hp + 1) * 2 * HEAD_PAD)
        qh = jnp.dot(cqn, wq_ref[:, lanes], preferred_element_type=F32)
        q_ref[:, lanes] = _rope(qh, cos2, sin2).astype(q_ref.dtype)
        kh = jnp.dot(ckvn, wk_ref[:, lanes], preferred_element_type=F32)
        k_ref[:, lanes] = (kh + kr2).astype(k_ref.dtype)


def _odd_in(h2, g, w, qn, wq, kvn, wk, wv, cos_t, sin_t):
    M = h2.shape[0]
    tm = _pick(M, (512, 256, 128))
    row = lambda n: pl.BlockSpec((tm, n), lambda i: (i, 0))
    hw = MLA_HEADS * HEAD_PAD
    vw = MLA_HEADS * MLA_V
    outs = (jax.ShapeDtypeStruct((M, hw), BF16), jax.ShapeDtypeStruct((M, hw), BF16),
            jax.ShapeDtypeStruct((M, vw), BF16), jax.ShapeDtypeStruct((M, vw), BF16))
    return pl.pallas_call(
        _odd_in_kernel,
        out_shape=outs,
        grid=(M // tm,),
        in_specs=[row(D_MODEL), _const_spec(g.shape), _const_spec(w.shape), _const_spec(qn.shape),
                  _const_spec(wq.shape), _const_spec(kvn.shape), _const_spec(wk.shape),
                  _const_spec(wv.shape), row(HEAD_PAD), row(HEAD_PAD)],
        out_specs=(row(hw), row(hw), row(vw), row(vw)),
        compiler_params=_params(("parallel",)),
        name="odd_in",
    )(h2, g, w, qn, wq, kvn, wk, wv, cos_t, sin_t)


def _attn_kernel(q_ref, k_ref, v_ref, o_ref, k_scr, vt_scr, qt_scr, bias_scr, s_a, s_b, mx_a, mx_b,
                 p_scr, m_scr, acc_scr, o_scr, *, nh, n_q):
    T = ATT_TILE
    Tp = q_ref.shape[1]
    Tp2 = n_q * T
    SL = SOFTMAX_SLAB
    VE = ATT_V_EXT

    k_scr[0:Tp, :] = k_ref[0]
    if Tp2 > Tp:
        k_scr[Tp:Tp2, :] = jnp.zeros((Tp2 - Tp, k_scr.shape[1]), k_scr.dtype)
    ones_row = (lax.broadcasted_iota(jnp.int32, (VE - MLA_V, Tp2), 0) == 0).astype(vt_scr.dtype)
    for h in range(nh):
        vt_scr[h * VE + MLA_V:(h + 1) * VE, :] = ones_row
        if Tp2 > Tp:
            vt_scr[h * VE:h * VE + MLA_V, Tp:Tp2] = jnp.zeros((MLA_V, Tp2 - Tp), vt_scr.dtype)
    for c in range(0, Tp, LANES):
        vt = v_ref[0, c:c + LANES, :].T
        for h in range(nh):
            vt_scr[h * VE:h * VE + MLA_V, c:c + LANES] = vt[h * MLA_V:(h + 1) * MLA_V, :]
    for j in range(n_q):
        rows = min(T, Tp - j * T)
        for h in range(nh):
            qh = q_ref[0, j * T:j * T + rows, h * HEAD_PAD:(h + 1) * HEAD_PAD]
            qt_scr[j, h, :, 0:rows] = qh.T
            if rows < T:
                qt_scr[j, h, :, rows:T] = jnp.zeros((HEAD_PAD, T - rows), qt_scr.dtype)
    kk = lax.broadcasted_iota(jnp.int32, (T, T), 0)
    qq = lax.broadcasted_iota(jnp.int32, (T, T), 1)
    bias_scr[...] = jnp.where(kk <= qq, 0.0, NEG_BIG)

    def qk(buf, j, i, diag=False):
        s_ref, mx_ref = buf
        koff = pl.multiple_of(i * T, T)
        for h in range(nh):
            sv = jnp.dot(k_scr[pl.ds(koff, T), h * HEAD_PAD:(h + 1) * HEAD_PAD],
                         qt_scr[j, h], preferred_element_type=F32)
            if diag:
                sv = sv + bias_scr[...]
            s_ref[h] = sv
            parts = [None] * 4
            for n, r in enumerate(range(0, T, SL)):
                blk = sv[r:r + SL, :]
                parts[n % 4] = blk if parts[n % 4] is None else jnp.maximum(parts[n % 4], blk)
            mt = jnp.maximum(jnp.maximum(parts[0], parts[1]), jnp.maximum(parts[2], parts[3]))
            mx_ref[h] = jnp.max(mt, axis=0, keepdims=True)

    def softmax_pv(buf, j, i, diag):
        s_ref, mx_ref = buf
        koff = pl.multiple_of(i * T, T)
        for h in range(nh):
            if diag:
                m_new = mx_ref[h]
            else:
                m_old = m_scr[j, h]
                m_new = jnp.maximum(m_old, mx_ref[h])
                alpha = jnp.exp2(m_old - m_new)
            m_scr[j, h] = m_new
            for r in range(0, T, SL):
                p_scr[h, r:r + SL, :] = jnp.exp2((s_ref[h, r:r + SL, :] - m_new).astype(p_scr.dtype))
            rows = slice(h * VE, (h + 1) * VE)
            pv = jnp.dot(vt_scr[rows, pl.ds(koff, T)], p_scr[h], preferred_element_type=F32)
            acc_scr[j, rows, :] = pv if diag else alpha * acc_scr[j, rows, :] + pv

    bufs = ((s_a, mx_a), (s_b, mx_b))
    G = ATT_STEPS_PER_ITER

    def run(n_steps, first, advance, diag):
        def steps(count, jt):
            for t in range(count):
                nxt = advance(*jt)
                qk(bufs[(t + 1) % 2], jnp.minimum(nxt[0], n_q - 1), nxt[1], diag)
                softmax_pv(bufs[t % 2], jt[0], jt[1], diag)
                jt = nxt
            return jt

        qk(bufs[0], first[0], first[1], diag)
        jt = lax.fori_loop(0, n_steps // G, lambda n, c: steps(G, c),
                           (jnp.int32(first[0]), jnp.int32(first[1])))
        steps(n_steps % G, jt)

    run(n_q, (0, 0), lambda j, i: (j + 1, jnp.minimum(i + 1, n_q - 1)), True)

    def advance(j, i):
        last = i + 1 >= j
        return jnp.where(last, j + 1, j), jnp.where(last, 0, i + 1)

    if n_q > 1:
        run(n_q * (n_q - 1) // 2, (1, 0), advance, False)

    def finish(j, c):
        outs = []
        for h in range(nh):
            acc = acc_scr[j, h * VE:h * VE + MLA_V, :]
            den = acc_scr[j, h * VE + MLA_V:h * VE + MLA_V + 1, :]
            outs.append(acc / den)
        o_scr[pl.ds(pl.multiple_of(j * T, T), T), :] = jnp.concatenate(outs, axis=0).T.astype(o_scr.dtype)
        return c

    lax.fori_loop(0, n_q, finish, 0)
    o_ref[0] = o_scr[0:Tp, :]


ATT_HEADS_PER_STEP = 4
ATT_STEPS_PER_ITER = 4
ATT_V_EXT = MLA_V + 16


def _attention(q, k, v):
    B, Tp, _ = q.shape
    nh = ATT_HEADS_PER_STEP
    T = ATT_TILE
    n_q = -(-Tp // T)
    Tp2 = n_q * T
    qk_spec = pl.BlockSpec((1, Tp, nh * HEAD_PAD), lambda b, p: (b, 0, p))
    v_spec = pl.BlockSpec((1, Tp, nh * MLA_V), lambda b, p: (b, 0, p))
    return pl.pallas_call(
        functools.partial(_attn_kernel, nh=nh, n_q=n_q),
        out_shape=jax.ShapeDtypeStruct((B, Tp, MLA_HEADS * MLA_V), BF16),
        grid=(B, MLA_HEADS // nh),
        in_specs=[qk_spec, qk_spec, v_spec],
        out_specs=v_spec,
        scratch_shapes=[pltpu.VMEM((Tp2, nh * HEAD_PAD), BF16),
                        pltpu.VMEM((nh * ATT_V_EXT, Tp2), BF16),
                        pltpu.VMEM((n_q, nh, HEAD_PAD, T), BF16),
                        pltpu.VMEM((T, T), F32),
                        pltpu.VMEM((nh, T, T), F32),
                        pltpu.VMEM((nh, T, T), F32),
                        pltpu.VMEM((nh, 1, T), F32),
                        pltpu.VMEM((nh, 1, T), F32),
                        pltpu.VMEM((nh, T, T), BF16),
                        pltpu.VMEM((n_q, nh, 1, T), F32),
                        pltpu.VMEM((n_q, nh * ATT_V_EXT, T), F32),
                        pltpu.VMEM((Tp2, nh * MLA_V), BF16)],
        compiler_params=_params(("parallel", "parallel")),
        name="mla_attention",
    )(q, k, v)


def _odd_out_kernel(h_ref, o_ref, gate_ref, w_ref, fn_ref, out_ref, *, final_norm):
    gate = gate_ref[...].astype(F32)
    x = (o_ref[...].astype(F32) * (gate * _sigmoid(gate))).astype(BF16)
    h = h_ref[...] + jnp.dot(x, w_ref[...], preferred_element_type=F32)
    out_ref[...] = _rms(h, fn_ref[...]) if final_norm else h


def _odd_out(h2, o, gate, w, fn, final_norm):
    M = h2.shape[0]
    tm = _pick(M, (512, 256, 128))
    row = lambda n: pl.BlockSpec((tm, n), lambda i: (i, 0))
    return pl.pallas_call(
        functools.partial(_odd_out_kernel, final_norm=final_norm),
        out_shape=jax.ShapeDtypeStruct((M, D_MODEL), F32),
        grid=(M // tm,),
        in_specs=[row(D_MODEL), row(o.shape[-1]), row(gate.shape[-1]), _const_spec(w.shape),
                  _const_spec(fn.shape)],
        out_specs=row(D_MODEL),
        compiler_params=_params(("parallel",)),
        name="odd_out",
    )(h2, o, gate, w, fn)


def _odd_out_last(h2, o, gate, w, fn, B, Tp, S):
    tm = _pick(S, (512, 256, 128, 64))
    win = lambda n: pl.BlockSpec((pl.Element(tm), pl.Element(n)),
                                 lambda b, i: (pl.multiple_of(b * Tp + N_META + i * tm, N_META), 0))
    return pl.pallas_call(
        functools.partial(_odd_out_kernel, final_norm=True),
        out_shape=jax.ShapeDtypeStruct((B * S, D_MODEL), F32),
        grid=(B, S // tm),
        in_specs=[win(D_MODEL), win(o.shape[-1]), win(gate.shape[-1]), _const_spec(w.shape),
                  _const_spec(fn.shape)],
        out_specs=pl.BlockSpec((tm, D_MODEL), lambda b, i: (b * (S // tm) + i, 0)),
        compiler_params=_params(("parallel", "parallel")),
        name="odd_out_last",
    )(h2, o, gate, w, fn)


def _split_cols(w, sizes):
    idx = np.cumsum(sizes)[:-1].tolist()
    return jnp.split(w, idx, axis=-1)


def _pack_even(w_in, gate_a_w, gate_x_w, alpha_w):
    xa, ga, q, k, v, ad, gb = _split_cols(
        w_in, (RNN_WIDTH, RNN_WIDTH, GLA_HEADS * GLA_DK, GLA_HEADS * GLA_DK, GLA_HEADS * GLA_DV,
               GLA_GATE_RANK, GLA_HEADS * GLA_DV))
    ad = jnp.pad(ad, ((0, 0), (0, LANES - GLA_GATE_RANK)))
    w = jnp.concatenate([xa, ga, q, k, v, gb, ad], axis=-1).astype(BF16)
    wg = jnp.concatenate([gate_a_w, gate_x_w], axis=-1).astype(BF16)
    aw = jnp.pad(alpha_w, ((0, LANES - GLA_GATE_RANK), (0, 0))).astype(BF16)
    return w, wg, aw


def _pack_odd(w_in, w_q_up, w_kv_up):
    cq, ckv, kr, gate = _split_cols(w_in, (MLA_Q_RANK, MLA_KV_RANK, MLA_ROPE, MLA_HEADS * MLA_V))
    kr = _head_lanes(jnp.zeros((kr.shape[0], MLA_NOPE), kr.dtype), kr)
    w = jnp.concatenate([cq, ckv, kr, gate], axis=-1).astype(BF16)
    wq = w_q_up.reshape(MLA_Q_RANK, MLA_HEADS, MLA_NOPE + MLA_ROPE)
    wq = _head_lanes(wq[..., :MLA_NOPE], wq[..., MLA_NOPE:]).reshape(MLA_Q_RANK, MLA_HEADS * HEAD_PAD)
    wkv = w_kv_up.reshape(MLA_KV_RANK, MLA_HEADS, MLA_NOPE + MLA_V)
    wk = _head_lanes(wkv[..., :MLA_NOPE], jnp.zeros(wkv.shape[:-1] + (MLA_ROPE,), wkv.dtype))
    wk = wk.reshape(MLA_KV_RANK, MLA_HEADS * HEAD_PAD)
    wv = wkv[..., MLA_NOPE:].reshape(MLA_KV_RANK, MLA_HEADS * MLA_V)
    return w, wq.astype(BF16), wk.astype(BF16), wv.astype(BF16)


def kernel(x, positions, meta_tokens, ab_norm, ab_w_in, ab_conv_w, ab_conv_b, ab_gate_a_w, ab_gate_a_b, ab_gate_x_w, ab_gate_x_b, ab_lru_lambda, ab_alpha_w, ab_alpha_b, ab_gla_norm, ab_w_out, c_norm, c_w_in, c_q_norm, c_w_q_up, c_kv_norm, c_w_kv_up, c_w_out, final_norm):
    B, S, D = x.shape
    T = N_META + S
    Tp = -(-T // LANES) * LANES
    M = B * Tp
    depth = ab_norm.shape[0] + c_norm.shape[0]
    row2 = lambda a: a.reshape(1, -1).astype(F32)

    meta = jnp.broadcast_to(meta_tokens.astype(x.dtype)[None], (B, N_META, D))
    h = jnp.concatenate([meta, x, jnp.zeros((B, Tp - T, D), x.dtype)], axis=1).reshape(M, D)

    meta_pos = jnp.broadcast_to(jnp.arange(N_META, dtype=positions.dtype)[None], (B, N_META))
    pos = jnp.concatenate([meta_pos, positions + N_META,
                           jnp.zeros((B, Tp - T), positions.dtype)], axis=1)
    inv_freq = ROPE_BASE ** (-jnp.arange(0, MLA_ROPE, 2, dtype=F32) / MLA_ROPE)
    invf = _head_lanes(jnp.zeros((1, MLA_NOPE), F32), jnp.concatenate([inv_freq, inv_freq])[None])
    cos_t, sin_t = _rope_tables(pos.astype(F32).reshape(M, 1), invf)

    for layer in range(depth):
        j = layer // 2
        if layer % 2 == 0:
            w, wg, aw = _pack_even(ab_w_in[j], ab_gate_a_w[j], ab_gate_x_w[j], ab_alpha_w[j])
            xa, ga, q, k, v, gb, ad = _even_in(h, row2(ab_norm[j]), w)
            r3 = lambda a: a.reshape(B, Tp, a.shape[-1])
            ya = _rglru(r3(xa), r3(ga), ab_conv_w[j].astype(F32), row2(ab_conv_b[j]), wg,
                        row2(ab_gate_a_b[j]), row2(ab_gate_x_b[j]), row2(ab_lru_lambda[j]))
            ob = _gla(r3(q), r3(k), r3(v), r3(gb), r3(ad), aw, row2(ab_alpha_b[j]),
                      row2(ab_gla_norm[j]))
            h = _even_out(h, ya.reshape(M, -1), ob.reshape(M, -1), ab_w_out[j].astype(BF16))
        else:
            w, wq, wk, wv = _pack_odd(c_w_in[j], c_w_q_up[j], c_w_kv_up[j])
            q, k, v, gate = _odd_in(h, row2(c_norm[j]), w, row2(c_q_norm[j]), wq,
                                    row2(c_kv_norm[j]), wk, wv, cos_t, sin_t)
            r3 = lambda a: a.reshape(B, Tp, a.shape[-1])
            o = _attention(r3(q), r3(k), r3(v))
            if layer == depth - 1:
                out = _odd_out_last(h, o.reshape(M, -1), gate, c_w_out[j].astype(BF16),
                                    row2(final_norm), B, Tp, S)
                return out.reshape(B, S, D)
            h = _odd_out(h, o.reshape(M, -1), gate, c_w_out[j].astype(BF16), row2(final_norm), False)
    raise NotImplementedError("the final RMSNorm is fused into an odd (MLA) last layer")
```

```python
import functools

import jax
import jax.numpy as jnp
import numpy as np
from jax import lax
from jax.experimental import pallas as pl
from jax.experimental.pallas import tpu as pltpu

F32 = jnp.float32
BF16 = jnp.bfloat16

D_MODEL = 1024
N_META = 16
EPS = 1e-6
RNN_WIDTH = D_MODEL
RNN_BLOCKS = 8
RNN_BLOCK = RNN_WIDTH // RNN_BLOCKS
CONV_WIDTH = 4
RGLRU_C = 8.0
GLA_HEADS = 4
GLA_DK = 128
GLA_DV = 256
GLA_GATE_RANK = 16
GLA_TAU = 16.0
GLA_CHUNK = 64
MLA_HEADS = 16
MLA_NOPE = 64
MLA_ROPE = 32
MLA_V = 64
MLA_Q_RANK = 512
MLA_KV_RANK = 256
ROPE_BASE = 10000.0

LANES = 128
HEAD_PAD = 128
ATT_TILE = 256
SOFTMAX_SLAB = 32
LOG2_E = 1.4426950408889634
NEG_BIG = -1e30
VMEM_LIMIT = 56 * 1024 * 1024


def _pick(n, candidates):
    for c in candidates:
        if n % c == 0:
            return c
    raise ValueError(f"no tile in {candidates} divides {n}")


def _rms(x, g):
    var = jnp.mean(x * x, axis=-1, keepdims=True)
    return x * lax.rsqrt(var + EPS) * g


def _sigmoid(x):
    return 1.0 / (1.0 + jnp.exp(-x))


def _const_spec(shape):
    nd = len(shape)
    return pl.BlockSpec(shape, lambda *_: (0,) * nd, pipeline_mode=pl.Buffered(1))


def _params(sem):
    return pltpu.CompilerParams(dimension_semantics=sem, vmem_limit_bytes=VMEM_LIMIT)


EVEN_OUT_WIDTHS = (RNN_WIDTH, RNN_WIDTH, GLA_HEADS * GLA_DK, GLA_HEADS * GLA_DK,
                   GLA_HEADS * GLA_DV, GLA_HEADS * GLA_DV, LANES)


def _even_in_body(h, g_ref, w_ref, out_refs):
    xn = _rms(h, g_ref[...]).astype(BF16)
    off = 0
    for ref in out_refs:
        n = ref.shape[-1]
        ref[...] = jnp.dot(xn, w_ref[:, off:off + n], preferred_element_type=F32).astype(ref.dtype)
        off += n


def _even_in_kernel(h_ref, g_ref, w_ref, *out_refs):
    _even_in_body(h_ref[...], g_ref, w_ref, out_refs)


def _row_tile(M):
    return _pick(M, (512, 256, 128))


def _row_spec(tm, n):
    return pl.BlockSpec((tm, n), lambda i: (i, 0))


def _even_in_outs(M):
    return tuple(jax.ShapeDtypeStruct((M, n), BF16) for n in EVEN_OUT_WIDTHS)


def _even_in(h2, g, w):
    M = h2.shape[0]
    tm = _row_tile(M)
    return pl.pallas_call(
        _even_in_kernel,
        out_shape=_even_in_outs(M),
        grid=(M // tm,),
        in_specs=[_row_spec(tm, D_MODEL), _const_spec(g.shape), _const_spec(w.shape)],
        out_specs=tuple(_row_spec(tm, n) for n in EVEN_OUT_WIDTHS),
        compiler_params=_params(("parallel",)),
        name="even_in",
    )(h2, g, w)


def _lru_pitch(tt):
    return tt if (tt // 8) % 2 else tt + 8


def _rglru_kernel(xa_ref, ga_ref, cw_ref, cb_ref, wg_ref, ba_ref, bx_ref, lam_ref, y_ref,
                  flat, u_s, a_s, hist, h_s, *, tt, pitch, nblk):
    B = xa_ref.shape[0]
    t_idx = pl.program_id(1)

    @pl.when(t_idx == 0)
    def _():
        hist[...] = jnp.zeros_like(hist)
        h_s[...] = jnp.zeros_like(h_s)

    for j in range(nblk):
        lanes = slice(j * LANES, (j + 1) * LANES)
        for b in range(B):
            flat[j, b * pitch:b * pitch + tt, :] = xa_ref[b, :, lanes].astype(F32)

    taps = [[cw_ref[k:k + 1, j * LANES:(j + 1) * LANES] for k in range(CONV_WIDTH)] for j in range(nblk)]
    bias = [cb_ref[:, j * LANES:(j + 1) * LANES] for j in range(nblk)]

    def conv_step(t, carry):
        out = []
        for j in range(nblk):
            x1, x2, x3 = carry[j]
            x0 = flat[j, pl.ds(t, B, stride=pitch), :]
            w = taps[j]
            u_s[j, pl.ds(pl.multiple_of(t * B, B), B), :] = (
                bias[j] + w[3] * x0 + w[2] * x1 + w[1] * x2 + w[0] * x3)
            out.append((x0, x1, x2))
        return tuple(out)

    last = lax.fori_loop(0, tt, conv_step,
                         tuple((hist[j, 0], hist[j, 1], hist[j, 2]) for j in range(nblk)), unroll=8)
    for j in range(nblk):
        for k in range(CONV_WIDTH - 1):
            hist[j, k] = last[j][k]

    for j in range(nblk):
        lanes = slice(j * LANES, (j + 1) * LANES)
        y = u_s[j]
        g = jnp.dot(y.astype(BF16), wg_ref[j], preferred_element_type=F32)
        r = _sigmoid(g[:, :LANES] + ba_ref[:, lanes])
        i = _sigmoid(g[:, LANES:] + bx_ref[:, lanes])
        lam = lam_ref[:, lanes]
        softplus_neg_lam = jnp.maximum(-lam, 0.0) + jnp.log(1.0 + jnp.exp(-jnp.abs(lam)))
        a = jnp.exp(r * ((-RGLRU_C) * softplus_neg_lam))
        x = 1.0 - a * a
        a_s[j] = a
        u_s[j] = (x * lax.rsqrt(jnp.maximum(x, 1e-30))) * (i * y)

    def scan_step(t, hs):
        out = []
        for j in range(nblk):
            rows = pl.ds(pl.multiple_of(t * B, B), B)
            h = a_s[j, rows, :] * hs[j] + u_s[j, rows, :]
            flat[j, pl.ds(t, B, stride=pitch), :] = h
            out.append(h)
        return tuple(out)

    hs = lax.fori_loop(0, tt, scan_step, tuple(h_s[j] for j in range(nblk)), unroll=8)
    for j in range(nblk):
        h_s[j] = hs[j]

    for j in range(nblk):
        lanes = slice(j * LANES, (j + 1) * LANES)
        for b in range(B):
            ga = ga_ref[b, :, lanes].astype(F32)
            y_ref[b, :, lanes] = (flat[j, b * pitch:b * pitch + tt, :]
                                  * (ga * _sigmoid(ga))).astype(y_ref.dtype)


def _rglru(xa, ga, cw, cb, wg, ba, bx, lam):
    B, Tp, C = xa.shape
    tt = _pick(Tp, (272, 256, 192, 128))
    pitch = _lru_pitch(tt)
    cblk = 512
    nblk = cblk // LANES
    blk = pl.BlockSpec((B, tt, cblk), lambda c, t: (0, t, c))
    vec = lambda rows: pl.BlockSpec((rows, cblk), lambda c, t: (0, c))
    return pl.pallas_call(
        functools.partial(_rglru_kernel, tt=tt, pitch=pitch, nblk=nblk),
        out_shape=jax.ShapeDtypeStruct((B, Tp, C), BF16),
        grid=(C // cblk, Tp // tt),
        in_specs=[blk, blk, vec(CONV_WIDTH), vec(1),
                  pl.BlockSpec((nblk, RNN_BLOCK, 2 * RNN_BLOCK), lambda c, t: (c, 0, 0)),
                  vec(1), vec(1), vec(1)],
        out_specs=blk,
        scratch_shapes=[pltpu.VMEM((nblk, B * pitch, LANES), F32),
                        pltpu.VMEM((nblk, tt * B, LANES), F32),
                        pltpu.VMEM((nblk, tt * B, LANES), F32),
                        pltpu.VMEM((nblk, CONV_WIDTH - 1, B, LANES), F32),
                        pltpu.VMEM((nblk, B, LANES), F32)],
        compiler_params=_params(("parallel", "arbitrary")),
        name="rglru",
    )(xa, ga, cw, cb, wg, ba, bx, lam)


def _gla_kernel(q_ref, k_ref, v_ref, gb_ref, ad_ref, aw_ref, ab_ref, gn_ref, o_ref, st_ref, *, nchunk, nb):
    C = GLA_CHUNK

    @pl.when(pl.program_id(1) == 0)
    def _():
        st_ref[...] = jnp.zeros_like(st_ref)

    row = lax.broadcasted_iota(jnp.int32, (C, C), 0)
    col = lax.broadcasted_iota(jnp.int32, (C, C), 1)
    causal = row >= col
    tril = causal.astype(BF16)
    scale = GLA_DK ** -0.5

    def chunk(c, carry):
        rows = pl.ds(pl.multiple_of(c * C, C), C)
        for b in range(nb):
            pre = jnp.dot(ad_ref[b, rows, :], aw_ref[...], preferred_element_type=F32) + ab_ref[...]
            la = (jnp.minimum(pre, 0.0) - jnp.log(1.0 + jnp.exp(-jnp.abs(pre)))) * (1.0 / GLA_TAU)
            la_hi = la.astype(BF16)
            la_lo = (la - la_hi.astype(F32)).astype(BF16)
            bc = (jnp.dot(tril, la_hi, preferred_element_type=F32)
                  + jnp.dot(tril, la_lo, preferred_element_type=F32))
            b_last = bc[C - 1:C, :]
            q = q_ref[b, rows, :].astype(F32)
            k = k_ref[b, rows, :].astype(F32)
            q_dec = (q * (jnp.exp(bc) * scale)).astype(BF16)
            k_inv = (k * jnp.exp(-bc)).astype(BF16)
            k_end = (k * jnp.exp(b_last - bc)).astype(BF16)
            decay = jnp.exp(b_last)
            for h in range(GLA_HEADS):
                kl = slice(h * GLA_DK, (h + 1) * GLA_DK)
                vl = slice(h * GLA_DV, (h + 1) * GLA_DV)
                qd, ki, ke = q_dec[:, kl], k_inv[:, kl], k_end[:, kl]
                vh = v_ref[b, rows, vl]
                st = st_ref[b, h]
                s = lax.dot_general(qd, ki, (((1,), (1,)), ((), ())), preferred_element_type=F32)
                s = jnp.where(causal, s, 0.0).astype(BF16)
                o = (jnp.dot(s, vh, preferred_element_type=F32)
                     + lax.dot_general(qd, st.astype(BF16), (((1,), (1,)), ((), ())),
                                       preferred_element_type=F32))
                vk = lax.dot_general(vh, ke, (((0,), (0,)), ((), ())), preferred_element_type=F32)
                st_ref[b, h] = st * decay[:, kl] + vk
                gb = gb_ref[b, rows, vl].astype(F32)
                o_ref[b, rows, vl] = (_rms(o, gn_ref[...]) * (gb * _sigmoid(gb))).astype(o_ref.dtype)
        return carry

    lax.fori_loop(0, nchunk, chunk, 0, unroll=True)


GLA_BATCH_PER_STEP = 8
GLA_TIME_TILE = 128


def _gla(q, k, v, gb, ad, aw, ab, gn):
    B, Tp, _ = q.shape
    nb = GLA_BATCH_PER_STEP if B % GLA_BATCH_PER_STEP == 0 else 1
    tc = GLA_TIME_TILE
    kw, vw = GLA_HEADS * GLA_DK, GLA_HEADS * GLA_DV
    spec = lambda w: pl.BlockSpec((nb, tc, w), lambda b, t: (b, t, 0))
    return pl.pallas_call(
        functools.partial(_gla_kernel, nchunk=tc // GLA_CHUNK, nb=nb),
        out_shape=jax.ShapeDtypeStruct((B, Tp, vw), BF16),
        grid=(B // nb, Tp // tc),
        in_specs=[spec(kw), spec(kw), spec(vw), spec(vw), spec(LANES),
                  _const_spec(aw.shape), _const_spec(ab.shape), _const_spec(gn.shape)],
        out_specs=spec(vw),
        scratch_shapes=[pltpu.VMEM((nb, GLA_HEADS, GLA_DV, GLA_DK), F32)],
        compiler_params=_params(("parallel", "arbitrary")),
        name="gla",
    )(q, k, v, gb, ad, aw, ab, gn)


def _even_out_value(h_ref, ya_ref, ob_ref, w_ref):
    n = ya_ref.shape[-1]
    return (h_ref[...]
            + jnp.dot(ya_ref[...], w_ref[:n, :], preferred_element_type=F32)
            + jnp.dot(ob_ref[...], w_ref[n:, :], preferred_element_type=F32))


def _even_out_odd_in_kernel(h_ref, ya_ref, ob_ref, wo_ref, *refs):
    hout_ref = refs[-5]
    h = _even_out_value(h_ref, ya_ref, ob_ref, wo_ref)
    hout_ref[...] = h
    _odd_in_body(h, *refs[:-5], *refs[-4:])


def _even_out_odd_in(h2, ya, ob, wo, params, cos_t, sin_t):
    M = h2.shape[0]
    tm = _row_tile(M)
    return pl.pallas_call(
        _even_out_odd_in_kernel,
        out_shape=(jax.ShapeDtypeStruct((M, D_MODEL), F32),) + _odd_in_outs(M),
        grid=(M // tm,),
        in_specs=[_row_spec(tm, D_MODEL), _row_spec(tm, ya.shape[-1]), _row_spec(tm, ob.shape[-1]),
                  _const_spec(wo.shape)] + _odd_in_specs(tm, params),
        out_specs=(_row_spec(tm, D_MODEL),) + tuple(_row_spec(tm, n) for n in ODD_IN_WIDTHS),
        compiler_params=_params(("parallel",)),
        name="even_out_odd_in",
    )(h2, ya, ob, wo, *params, cos_t, sin_t)


ROPE_HALF = MLA_ROPE // 2
ROPE_X2_LANE = HEAD_PAD // 2


def _head_lanes(nope, rope_part):
    zeros = lambda n: jnp.zeros(nope.shape[:-1] + (n,), nope.dtype)
    n1 = ROPE_X2_LANE - ROPE_HALF
    return jnp.concatenate([rope_part[..., :ROPE_HALF], nope[..., :n1], rope_part[..., ROPE_HALF:],
                            nope[..., n1:], zeros(HEAD_PAD - MLA_NOPE - MLA_ROPE)], axis=-1)


def _rope_table_kernel(pos_ref, invf_ref, c_ref, s_ref):
    ang = pos_ref[...] * invf_ref[...]
    lane = lax.broadcasted_iota(jnp.int32, ang.shape, 1)
    c_ref[...] = jnp.cos(ang)
    sn = jnp.sin(ang)
    s_ref[...] = jnp.where(lane < ROPE_X2_LANE, -sn, sn)


def _rope_tables(posf, invf):
    M = posf.shape[0]
    tm = _pick(M, (512, 256, 128))
    out = jax.ShapeDtypeStruct((M, HEAD_PAD), F32)
    return pl.pallas_call(
        _rope_table_kernel,
        out_shape=(out, out),
        grid=(M // tm,),
        in_specs=[pl.BlockSpec((tm, 1), lambda i: (i, 0)), _const_spec((1, HEAD_PAD))],
        out_specs=(pl.BlockSpec((tm, HEAD_PAD), lambda i: (i, 0)),) * 2,
        compiler_params=_params(("parallel",)),
        name="rope_tables",
    )(posf, invf)


def _rope(x, cos_t, sin_t):
    heads = [pltpu.roll(x[:, c:c + HEAD_PAD], ROPE_X2_LANE, axis=1) for c in range(0, x.shape[1], HEAD_PAD)]
    partner = heads[0] if len(heads) == 1 else jnp.concatenate(heads, axis=1)
    return x * cos_t + partner * sin_t


def _odd_in_body(h, g_ref, w_ref, qn_ref, wq_ref, kvn_ref, wk_ref, wv_ref, cos_ref, sin_ref,
                 q_ref, k_ref, v_ref, gate_ref):
    xn = _rms(h, g_ref[...]).astype(BF16)
    c0, c1, c2 = MLA_Q_RANK, MLA_Q_RANK + MLA_KV_RANK, MLA_Q_RANK + MLA_KV_RANK + HEAD_PAD
    gate_ref[...] = jnp.dot(xn, w_ref[:, c2:], preferred_element_type=F32).astype(gate_ref.dtype)
    cq = jnp.dot(xn, w_ref[:, :c0], preferred_element_type=F32)
    ckv = jnp.dot(xn, w_ref[:, c0:c1], preferred_element_type=F32)
    kr = jnp.dot(xn, w_ref[:, c1:c2], preferred_element_type=F32)
    cos_t, sin_t = cos_ref[...], sin_ref[...]
    kr = _rope(kr, cos_t, sin_t)
    cqn = _rms(cq, qn_ref[...]).astype(BF16)
    ckvn = _rms(ckv, kvn_ref[...]).astype(BF16)
    v_ref[...] = jnp.dot(ckvn, wv_ref[...], preferred_element_type=F32).astype(v_ref.dtype)
    scale = LOG2_E * (MLA_NOPE + MLA_ROPE) ** -0.5
    cos2 = jnp.concatenate([cos_t, cos_t], axis=1) * scale
    sin2 = jnp.concatenate([sin_t, sin_t], axis=1) * scale
    kr2 = jnp.concatenate([kr, kr], axis=1)
    for hp in range(MLA_HEADS // 2):
        lanes = slice(hp * 2 * HEAD_PAD, (hp + 1) * 2 * HEAD_PAD)
        qh = jnp.dot(cqn, wq_ref[:, lanes], preferred_element_type=F32)
        q_ref[:, lanes] = _rope(qh, cos2, sin2).astype(q_ref.dtype)
        kh = jnp.dot(ckvn, wk_ref[:, lanes], preferred_element_type=F32)
        k_ref[:, lanes] = (kh + kr2).astype(k_ref.dtype)


ODD_IN_WIDTHS = (MLA_HEADS * HEAD_PAD, MLA_HEADS * HEAD_PAD, MLA_HEADS * MLA_V, MLA_HEADS * MLA_V)


def _odd_in_outs(M):
    return tuple(jax.ShapeDtypeStruct((M, n), BF16) for n in ODD_IN_WIDTHS)


def _odd_in_specs(tm, params):
    return [_const_spec(p.shape) for p in params] + [_row_spec(tm, HEAD_PAD), _row_spec(tm, HEAD_PAD)]


def _attn_kernel(q_ref, k_ref, v_ref, o_ref, k_scr, vt_scr, qt_scr, bias_scr, s_a, s_b, mx_a, mx_b,
                 m_scr, acc_scr, o_scr, *, nh, n_q):
    T = ATT_TILE
    Tp = q_ref.shape[1]
    Tp2 = n_q * T
    SL = SOFTMAX_SLAB
    VE = ATT_V_EXT

    k_scr[0:Tp, :] = k_ref[0]
    if Tp2 > Tp:
        k_scr[Tp:Tp2, :] = jnp.zeros((Tp2 - Tp, k_scr.shape[1]), k_scr.dtype)
    ones_row = (lax.broadcasted_iota(jnp.int32, (VE - MLA_V, Tp2), 0) == 0).astype(vt_scr.dtype)
    for h in range(nh):
        vt_scr[h * VE + MLA_V:(h + 1) * VE, :] = ones_row
        if Tp2 > Tp:
            vt_scr[h * VE:h * VE + MLA_V, Tp:Tp2] = jnp.zeros((MLA_V, Tp2 - Tp), vt_scr.dtype)
    for c in range(0, Tp, LANES):
        vt = v_ref[0, c:c + LANES, :].T
        for h in range(nh):
            vt_scr[h * VE:h * VE + MLA_V, c:c + LANES] = vt[h * MLA_V:(h + 1) * MLA_V, :]
    for j in range(n_q):
        rows = min(T, Tp - j * T)
        for h in range(nh):
            qh = q_ref[0, j * T:j * T + rows, h * HEAD_PAD:(h + 1) * HEAD_PAD]
            qt_scr[j, h, :, 0:rows] = qh.T
            if rows < T:
                qt_scr[j, h, :, rows:T] = jnp.zeros((HEAD_PAD, T - rows), qt_scr.dtype)
    kk = lax.broadcasted_iota(jnp.int32, (T, T), 0)
    qq = lax.broadcasted_iota(jnp.int32, (T, T), 1)
    bias_scr[...] = jnp.where(kk <= qq, 0.0, NEG_BIG)

    def qk(buf, j, i, diag=False):
        s_ref, mx_ref = buf
        koff = pl.multiple_of(i * T, T)
        for h in range(nh):
            sv = jnp.dot(k_scr[pl.ds(koff, T), h * HEAD_PAD:(h + 1) * HEAD_PAD],
                         qt_scr[j, h], preferred_element_type=F32)
            if diag:
                sv = sv + bias_scr[...]
            s_ref[h] = sv
            parts = [None] * 4
            for n, r in enumerate(range(0, T, SL)):
                blk = sv[r:r + SL, :]
                parts[n % 4] = blk if parts[n % 4] is None else jnp.maximum(parts[n % 4], blk)
            mt = jnp.maximum(jnp.maximum(parts[0], parts[1]), jnp.maximum(parts[2], parts[3]))
            mx_ref[h] = jnp.max(mt, axis=0, keepdims=True)

    def softmax_pv(buf, j, i, diag):
        s_ref, mx_ref = buf
        koff = pl.multiple_of(i * T, T)
        for h in range(nh):
            if diag:
                m_new = mx_ref[h]
            else:
                m_old = m_scr[j, h]
                m_new = jnp.maximum(m_old, mx_ref[h])
                alpha = jnp.exp2(m_old - m_new)
            m_scr[j, h] = m_new
            p = jnp.concatenate([jnp.exp2((s_ref[h, r:r + SL, :] - m_new).astype(BF16))
                                 for r in range(0, T, SL)], axis=0)
            rows = slice(h * VE, (h + 1) * VE)
            pv = jnp.dot(vt_scr[rows, pl.ds(koff, T)], p, preferred_element_type=F32)
            acc_scr[j, rows, :] = pv if diag else alpha * acc_scr[j, rows, :] + pv

    bufs = ((s_a, mx_a), (s_b, mx_b))
    G = ATT_STEPS_PER_ITER

    def run(n_steps, first, advance, diag):
        def steps(count, jt):
            for t in range(count):
                nxt = advance(*jt)
                qk(bufs[(t + 1) % 2], jnp.minimum(nxt[0], n_q - 1), nxt[1], diag)
                softmax_pv(bufs[t % 2], jt[0], jt[1], diag)
                jt = nxt
            return jt

        qk(bufs[0], first[0], first[1], diag)
        jt = lax.fori_loop(0, n_steps // G, lambda n, c: steps(G, c),
                           (jnp.int32(first[0]), jnp.int32(first[1])))
        steps(n_steps % G, jt)

    run(n_q, (0, 0), lambda j, i: (j + 1, jnp.minimum(i + 1, n_q - 1)), True)

    def advance(j, i):
        last = i + 1 >= j
        return jnp.where(last, j + 1, j), jnp.where(last, 0, i + 1)

    if n_q > 1:
        run(n_q * (n_q - 1) // 2, (1, 0), advance, False)

    def finish(j, c):
        outs = []
        for h in range(nh):
            acc = acc_scr[j, h * VE:h * VE + MLA_V, :]
            den = acc_scr[j, h * VE + MLA_V:h * VE + MLA_V + 1, :]
            outs.append(acc / den)
        o_scr[pl.ds(pl.multiple_of(j * T, T), T), :] = jnp.concatenate(outs, axis=0).T.astype(o_scr.dtype)
        return c

    lax.fori_loop(0, n_q, finish, 0)
    o_ref[0] = o_scr[0:Tp, :]


ATT_HEADS_PER_STEP = 4
ATT_STEPS_PER_ITER = 12
ATT_V_EXT = MLA_V + 16


def _attention(q, k, v):
    B, Tp, _ = q.shape
    nh = ATT_HEADS_PER_STEP
    T = ATT_TILE
    n_q = -(-Tp // T)
    Tp2 = n_q * T
    qk_spec = pl.BlockSpec((1, Tp, nh * HEAD_PAD), lambda b, p: (b, 0, p))
    v_spec = pl.BlockSpec((1, Tp, nh * MLA_V), lambda b, p: (b, 0, p))
    return pl.pallas_call(
        functools.partial(_attn_kernel, nh=nh, n_q=n_q),
        out_shape=jax.ShapeDtypeStruct((B, Tp, MLA_HEADS * MLA_V), BF16),
        grid=(B, MLA_HEADS // nh),
        in_specs=[qk_spec, qk_spec, v_spec],
        out_specs=v_spec,
        scratch_shapes=[pltpu.VMEM((Tp2, nh * HEAD_PAD), BF16),
                        pltpu.VMEM((nh * ATT_V_EXT, Tp2), BF16),
                        pltpu.VMEM((n_q, nh, HEAD_PAD, T), BF16),
                        pltpu.VMEM((T, T), F32),
                        pltpu.VMEM((nh, T, T), F32),
                        pltpu.VMEM((nh, T, T), F32),
                        pltpu.VMEM((nh, 1, T), F32),
                        pltpu.VMEM((nh, 1, T), F32),
                        pltpu.VMEM((n_q, nh, 1, T), F32),
                        pltpu.VMEM((n_q, nh * ATT_V_EXT, T), F32),
                        pltpu.VMEM((Tp2, nh * MLA_V), BF16)],
        compiler_params=_params(("parallel", "parallel")),
        name="mla_attention",
    )(q, k, v)


def _odd_out_value(h_ref, o_ref, gate_ref, w_ref):
    gate = gate_ref[...].astype(F32)
    x = (o_ref[...].astype(F32) * (gate * _sigmoid(gate))).astype(BF16)
    return h_ref[...] + jnp.dot(x, w_ref[...], preferred_element_type=F32)


def _odd_out_even_in_kernel(h_ref, o_ref, gate_ref, wo_ref, g_ref, w_ref, hout_ref, *out_refs):
    h = _odd_out_value(h_ref, o_ref, gate_ref, wo_ref)
    hout_ref[...] = h
    _even_in_body(h, g_ref, w_ref, out_refs)


def _odd_out_even_in(h2, o, gate, wo, g, w):
    M = h2.shape[0]
    tm = _row_tile(M)
    return pl.pallas_call(
        _odd_out_even_in_kernel,
        out_shape=(jax.ShapeDtypeStruct((M, D_MODEL), F32),) + _even_in_outs(M),
        grid=(M // tm,),
        in_specs=[_row_spec(tm, D_MODEL), _row_spec(tm, o.shape[-1]), _row_spec(tm, gate.shape[-1]),
                  _const_spec(wo.shape), _const_spec(g.shape), _const_spec(w.shape)],
        out_specs=(_row_spec(tm, D_MODEL),) + tuple(_row_spec(tm, n) for n in EVEN_OUT_WIDTHS),
        compiler_params=_params(("parallel",)),
        name="odd_out_even_in",
    )(h2, o, gate, wo, g, w)


def _odd_out_last_kernel(h_ref, o_ref, gate_ref, w_ref, fn_ref, out_ref):
    out_ref[...] = _rms(_odd_out_value(h_ref, o_ref, gate_ref, w_ref), fn_ref[...])


def _odd_out_last(h2, o, gate, w, fn, B, Tp, S):
    tm = _pick(S, (512, 256, 128, 64))
    win = lambda n: pl.BlockSpec((pl.Element(tm), pl.Element(n)),
                                 lambda b, i: (pl.multiple_of(b * Tp + N_META + i * tm, N_META), 0))
    return pl.pallas_call(
        _odd_out_last_kernel,
        out_shape=jax.ShapeDtypeStruct((B * S, D_MODEL), F32),
        grid=(B, S // tm),
        in_specs=[win(D_MODEL), win(o.shape[-1]), win(gate.shape[-1]), _const_spec(w.shape),
                  _const_spec(fn.shape)],
        out_specs=pl.BlockSpec((tm, D_MODEL), lambda b, i: (b * (S // tm) + i, 0)),
        compiler_params=_params(("parallel", "parallel")),
        name="odd_out_last",
    )(h2, o, gate, w, fn)


def _split_cols(w, sizes):
    idx = np.cumsum(sizes)[:-1].tolist()
    return jnp.split(w, idx, axis=-1)


def _pack_even(w_in, gate_a_w, gate_x_w, alpha_w):
    xa, ga, q, k, v, ad, gb = _split_cols(
        w_in, (RNN_WIDTH, RNN_WIDTH, GLA_HEADS * GLA_DK, GLA_HEADS * GLA_DK, GLA_HEADS * GLA_DV,
               GLA_GATE_RANK, GLA_HEADS * GLA_DV))
    ad = jnp.pad(ad, ((0, 0), (0, LANES - GLA_GATE_RANK)))
    w = jnp.concatenate([xa, ga, q, k, v, gb, ad], axis=-1).astype(BF16)
    wg = jnp.concatenate([gate_a_w, gate_x_w], axis=-1).astype(BF16)
    aw = jnp.pad(alpha_w, ((0, LANES - GLA_GATE_RANK), (0, 0))).astype(BF16)
    return w, wg, aw


def _pack_odd(w_in, w_q_up, w_kv_up):
    cq, ckv, kr, gate = _split_cols(w_in, (MLA_Q_RANK, MLA_KV_RANK, MLA_ROPE, MLA_HEADS * MLA_V))
    kr = _head_lanes(jnp.zeros((kr.shape[0], MLA_NOPE), kr.dtype), kr)
    w = jnp.concatenate([cq, ckv, kr, gate], axis=-1).astype(BF16)
    wq = w_q_up.reshape(MLA_Q_RANK, MLA_HEADS, MLA_NOPE + MLA_ROPE)
    wq = _head_lanes(wq[..., :MLA_NOPE], wq[..., MLA_NOPE:]).reshape(MLA_Q_RANK, MLA_HEADS * HEAD_PAD)
    wkv = w_kv_up.reshape(MLA_KV_RANK, MLA_HEADS, MLA_NOPE + MLA_V)
    wk = _head_lanes(wkv[..., :MLA_NOPE], jnp.zeros(wkv.shape[:-1] + (MLA_ROPE,), wkv.dtype))
    wk = wk.reshape(MLA_KV_RANK, MLA_HEADS * HEAD_PAD)
    wv = wkv[..., MLA_NOPE:].reshape(MLA_KV_RANK, MLA_HEADS * MLA_V)
    return w, wq.astype(BF16), wk.astype(BF16), wv.astype(BF16)


def kernel(x, positions, meta_tokens, ab_norm, ab_w_in, ab_conv_w, ab_conv_b, ab_gate_a_w, ab_gate_a_b, ab_gate_x_w, ab_gate_x_b, ab_lru_lambda, ab_alpha_w, ab_alpha_b, ab_gla_norm, ab_w_out, c_norm, c_w_in, c_q_norm, c_w_q_up, c_kv_norm, c_w_kv_up, c_w_out, final_norm):
    B, S, D = x.shape
    T = N_META + S
    Tp = -(-T // LANES) * LANES
    M = B * Tp
    depth = ab_norm.shape[0] + c_norm.shape[0]
    row2 = lambda a: a.reshape(1, -1).astype(F32)

    meta = jnp.broadcast_to(meta_tokens.astype(x.dtype)[None], (B, N_META, D))
    h = jnp.concatenate([meta, x, jnp.zeros((B, Tp - T, D), x.dtype)], axis=1).reshape(M, D)

    meta_pos = jnp.broadcast_to(jnp.arange(N_META, dtype=positions.dtype)[None], (B, N_META))
    pos = jnp.concatenate([meta_pos, positions + N_META,
                           jnp.zeros((B, Tp - T), positions.dtype)], axis=1)
    inv_freq = ROPE_BASE ** (-jnp.arange(0, MLA_ROPE, 2, dtype=F32) / MLA_ROPE)
    invf = _head_lanes(jnp.zeros((1, MLA_NOPE), F32), jnp.concatenate([inv_freq, inv_freq])[None])
    cos_t, sin_t = _rope_tables(pos.astype(F32).reshape(M, 1), invf)

    if depth % 2 or ab_norm.shape[0] != c_norm.shape[0]:
        raise NotImplementedError("layers must alternate (RG-LRU || GLA), MLA and end on an MLA layer")
    r3 = lambda a: a.reshape(B, Tp, a.shape[-1])
    even_in = None
    even_packs = [_pack_even(ab_w_in[j], ab_gate_a_w[j], ab_gate_x_w[j], ab_alpha_w[j])
                  for j in range(depth // 2)]
    for j in range(depth // 2):
        w, wg, aw = even_packs[j]
        if even_in is None:
            even_in = _even_in(h, row2(ab_norm[j]), w)
        xa, ga, q, k, v, gb, ad = even_in
        ya = _rglru(r3(xa), r3(ga), ab_conv_w[j].astype(F32), row2(ab_conv_b[j]), wg,
                    row2(ab_gate_a_b[j]), row2(ab_gate_x_b[j]), row2(ab_lru_lambda[j]))
        ob = _gla(r3(q), r3(k), r3(v), r3(gb), r3(ad), aw, row2(ab_alpha_b[j]), row2(ab_gla_norm[j]))
        w, wq, wk, wv = _pack_odd(c_w_in[j], c_w_q_up[j], c_w_kv_up[j])
        odd_params = (row2(c_norm[j]), w, row2(c_q_norm[j]), wq, row2(c_kv_norm[j]), wk, wv)
        h, q, k, v, gate = _even_out_odd_in(h, ya.reshape(M, -1), ob.reshape(M, -1),
                                            ab_w_out[j].astype(BF16), odd_params, cos_t, sin_t)
        o = _attention(r3(q), r3(k), r3(v)).reshape(M, -1)
        wo = c_w_out[j].astype(BF16)
        if j == depth // 2 - 1:
            return _odd_out_last(h, o, gate, wo, row2(final_norm), B, Tp, S).reshape(B, S, D)
        h, *even_in = _odd_out_even_in(h, o, gate, wo, row2(ab_norm[j + 1]), even_packs[j + 1][0])
```

```python
import functools

import jax
import jax.numpy as jnp
import numpy as np
from jax import lax
from jax.experimental import pallas as pl
from jax.experimental.pallas import tpu as pltpu

F32 = jnp.float32
BF16 = jnp.bfloat16

D_MODEL = 1024
N_META = 16
EPS = 1e-6
RNN_WIDTH = D_MODEL
RNN_BLOCKS = 8
RNN_BLOCK = RNN_WIDTH // RNN_BLOCKS
CONV_WIDTH = 4
RGLRU_C = 8.0
GLA_HEADS = 4
GLA_DK = 128
GLA_DV = 256
GLA_GATE_RANK = 16
GLA_TAU = 16.0
GLA_CHUNK = 64
MLA_HEADS = 16
MLA_NOPE = 64
MLA_ROPE = 32
MLA_V = 64
MLA_Q_RANK = 512
MLA_KV_RANK = 256
ROPE_BASE = 10000.0

LANES = 128
HEAD_PAD = 128
ATT_TILE = 256
SOFTMAX_SLAB = 32
LOG2_E = 1.4426950408889634
NEG_BIG = -1e30
VMEM_LIMIT = 56 * 1024 * 1024


def _pick(n, candidates):
    for c in candidates:
        if n % c == 0:
            return c
    raise ValueError(f"no tile in {candidates} divides {n}")


def _rms(x, g):
    var = jnp.mean(x * x, axis=-1, keepdims=True)
    return x * lax.rsqrt(var + EPS) * g


def _sigmoid(x):
    return 1.0 / (1.0 + jnp.exp(-x))


def _const_spec(shape):
    nd = len(shape)
    return pl.BlockSpec(shape, lambda *_: (0,) * nd, pipeline_mode=pl.Buffered(1))


def _params(sem):
    return pltpu.CompilerParams(dimension_semantics=sem, vmem_limit_bytes=VMEM_LIMIT)


EVEN_OUT_WIDTHS = (RNN_WIDTH, RNN_WIDTH, GLA_HEADS * GLA_DK, GLA_HEADS * GLA_DK,
                   GLA_HEADS * GLA_DV, GLA_HEADS * GLA_DV, LANES)


def _even_in_body(h, g_ref, w_ref, out_refs):
    xn = _rms(h, g_ref[...]).astype(BF16)
    off = 0
    for ref in out_refs:
        n = ref.shape[-1]
        ref[...] = jnp.dot(xn, w_ref[:, off:off + n], preferred_element_type=F32).astype(ref.dtype)
        off += n


def _even_in_kernel(h_ref, g_ref, w_ref, *out_refs):
    _even_in_body(h_ref[...], g_ref, w_ref, out_refs)


def _row_tile(M):
    return _pick(M, (512, 256, 128))


def _row_spec(tm, n):
    return pl.BlockSpec((tm, n), lambda i: (i, 0))


def _even_in_outs(M):
    return tuple(jax.ShapeDtypeStruct((M, n), BF16) for n in EVEN_OUT_WIDTHS)


def _even_in(h2, g, w):
    M = h2.shape[0]
    tm = _row_tile(M)
    return pl.pallas_call(
        _even_in_kernel,
        out_shape=_even_in_outs(M),
        grid=(M // tm,),
        in_specs=[_row_spec(tm, D_MODEL), _const_spec(g.shape), _const_spec(w.shape)],
        out_specs=tuple(_row_spec(tm, n) for n in EVEN_OUT_WIDTHS),
        compiler_params=_params(("parallel",)),
        name="even_in",
    )(h2, g, w)


def _lru_pitch(tt):
    return tt if (tt // 8) % 2 else tt + 8


def _rglru_kernel(xa_ref, ga_ref, cw_ref, cb_ref, wg_ref, ba_ref, bx_ref, lam_ref, y_ref,
                  flat, u_s, a_s, hist, h_s, *, tt, pitch, nblk):
    B = xa_ref.shape[0]
    t_idx = pl.program_id(1)

    @pl.when(t_idx == 0)
    def _():
        hist[...] = jnp.zeros_like(hist)
        h_s[...] = jnp.zeros_like(h_s)

    for j in range(nblk):
        lanes = slice(j * LANES, (j + 1) * LANES)
        for b in range(B):
            flat[j, b * pitch:b * pitch + tt, :] = xa_ref[b, :, lanes].astype(F32)

    taps = [[cw_ref[k:k + 1, j * LANES:(j + 1) * LANES] for k in range(CONV_WIDTH)] for j in range(nblk)]
    bias = [cb_ref[:, j * LANES:(j + 1) * LANES] for j in range(nblk)]

    def conv_step(t, carry):
        out = []
        for j in range(nblk):
            x1, x2, x3 = carry[j]
            x0 = flat[j, pl.ds(t, B, stride=pitch), :]
            w = taps[j]
            u_s[j, pl.ds(pl.multiple_of(t * B, B), B), :] = (
                bias[j] + w[3] * x0 + w[2] * x1 + w[1] * x2 + w[0] * x3)
            out.append((x0, x1, x2))
        return tuple(out)

    last = lax.fori_loop(0, tt, conv_step,
                         tuple((hist[j, 0], hist[j, 1], hist[j, 2]) for j in range(nblk)), unroll=8)
    for j in range(nblk):
        for k in range(CONV_WIDTH - 1):
            hist[j, k] = last[j][k]

    for j in range(nblk):
        lanes = slice(j * LANES, (j + 1) * LANES)
        y = u_s[j]
        g = jnp.dot(y.astype(BF16), wg_ref[j], preferred_element_type=F32)
        r = _sigmoid(g[:, :LANES] + ba_ref[:, lanes])
        i = _sigmoid(g[:, LANES:] + bx_ref[:, lanes])
        lam = lam_ref[:, lanes]
        softplus_neg_lam = jnp.maximum(-lam, 0.0) + jnp.log(1.0 + jnp.exp(-jnp.abs(lam)))
        a = jnp.exp(r * ((-RGLRU_C) * softplus_neg_lam))
        x = 1.0 - a * a
        a_s[j] = a
        u_s[j] = (x * lax.rsqrt(jnp.maximum(x, 1e-30))) * (i * y)

    def scan_step(t, hs):
        out = []
        for j in range(nblk):
            rows = pl.ds(pl.multiple_of(t * B, B), B)
            h = a_s[j, rows, :] * hs[j] + u_s[j, rows, :]
            flat[j, pl.ds(t, B, stride=pitch), :] = h
            out.append(h)
        return tuple(out)

    hs = lax.fori_loop(0, tt, scan_step, tuple(h_s[j] for j in range(nblk)), unroll=8)
    for j in range(nblk):
        h_s[j] = hs[j]

    for j in range(nblk):
        lanes = slice(j * LANES, (j + 1) * LANES)
        for b in range(B):
            ga = ga_ref[b, :, lanes].astype(F32)
            y_ref[b, :, lanes] = (flat[j, b * pitch:b * pitch + tt, :]
                                  * (ga * _sigmoid(ga))).astype(y_ref.dtype)


def _rglru(xa, ga, cw, cb, wg, ba, bx, lam):
    B, Tp, C = xa.shape
    tt = _pick(Tp, (272, 256, 192, 128))
    pitch = _lru_pitch(tt)
    cblk = 512
    nblk = cblk // LANES
    blk = pl.BlockSpec((B, tt, cblk), lambda c, t: (0, t, c))
    vec = lambda rows: pl.BlockSpec((rows, cblk), lambda c, t: (0, c))
    return pl.pallas_call(
        functools.partial(_rglru_kernel, tt=tt, pitch=pitch, nblk=nblk),
        out_shape=jax.ShapeDtypeStruct((B, Tp, C), BF16),
        grid=(C // cblk, Tp // tt),
        in_specs=[blk, blk, vec(CONV_WIDTH), vec(1),
                  pl.BlockSpec((nblk, RNN_BLOCK, 2 * RNN_BLOCK), lambda c, t: (c, 0, 0)),
                  vec(1), vec(1), vec(1)],
        out_specs=blk,
        scratch_shapes=[pltpu.VMEM((nblk, B * pitch, LANES), F32),
                        pltpu.VMEM((nblk, tt * B, LANES), F32),
                        pltpu.VMEM((nblk, tt * B, LANES), F32),
                        pltpu.VMEM((nblk, CONV_WIDTH - 1, B, LANES), F32),
                        pltpu.VMEM((nblk, B, LANES), F32)],
        compiler_params=_params(("parallel", "arbitrary")),
        name="rglru",
    )(xa, ga, cw, cb, wg, ba, bx, lam)


def _gla_kernel(q_ref, k_ref, v_ref, gb_ref, ad_ref, aw_ref, ab_ref, gn_ref, o_ref, st_ref, *, nchunk, nb):
    C = GLA_CHUNK

    @pl.when(pl.program_id(1) == 0)
    def _():
        st_ref[...] = jnp.zeros_like(st_ref)

    row = lax.broadcasted_iota(jnp.int32, (C, C), 0)
    col = lax.broadcasted_iota(jnp.int32, (C, C), 1)
    causal = row >= col
    tril = causal.astype(BF16)
    scale = GLA_DK ** -0.5

    def chunk(c, carry):
        rows = pl.ds(pl.multiple_of(c * C, C), C)
        bs = range(nb)
        hs = range(GLA_HEADS)
        kl = [slice(h * GLA_DK, (h + 1) * GLA_DK) for h in hs]
        vl = [slice(h * GLA_DV, (h + 1) * GLA_DV) for h in hs]
        pre = [jnp.dot(ad_ref[b, rows, :], aw_ref[...], preferred_element_type=F32) + ab_ref[...]
               for b in bs]
        la = [(jnp.minimum(p, 0.0) - jnp.log(1.0 + jnp.exp(-jnp.abs(p)))) * (1.0 / GLA_TAU) for p in pre]
        la_hi = [x.astype(BF16) for x in la]
        la_lo = [(x - hi.astype(F32)).astype(BF16) for x, hi in zip(la, la_hi)]
        bc = [jnp.dot(tril, hi, preferred_element_type=F32) + jnp.dot(tril, lo, preferred_element_type=F32)
              for hi, lo in zip(la_hi, la_lo)]
        b_last = [x[C - 1:C, :] for x in bc]
        q = [q_ref[b, rows, :].astype(F32) for b in bs]
        k = [k_ref[b, rows, :].astype(F32) for b in bs]
        q_dec = [(q[b] * (jnp.exp(bc[b]) * scale)).astype(BF16) for b in bs]
        k_inv = [(k[b] * jnp.exp(-bc[b])).astype(BF16) for b in bs]
        k_end = [(k[b] * jnp.exp(b_last[b] - bc[b])).astype(BF16) for b in bs]
        decay = [jnp.exp(x) for x in b_last]
        s = [[jnp.where(causal,
                        lax.dot_general(q_dec[b][:, kl[h]], k_inv[b][:, kl[h]], (((1,), (1,)), ((), ())),
                                        preferred_element_type=F32), 0.0).astype(BF16)
              for h in hs] for b in bs]
        o = [[None] * GLA_HEADS for _ in bs]
        for b in bs:
            for h in hs:
                vh = v_ref[b, rows, vl[h]]
                st = st_ref[b, h]
                o[b][h] = (jnp.dot(s[b][h], vh, preferred_element_type=F32)
                           + lax.dot_general(q_dec[b][:, kl[h]], st.astype(BF16), (((1,), (1,)), ((), ())),
                                             preferred_element_type=F32))
                vk = lax.dot_general(vh, k_end[b][:, kl[h]], (((0,), (0,)), ((), ())),
                                     preferred_element_type=F32)
                st_ref[b, h] = st * decay[b][:, kl[h]] + vk
        for b in bs:
            for h in hs:
                gb = gb_ref[b, rows, vl[h]].astype(F32)
                o_ref[b, rows, vl[h]] = (_rms(o[b][h], gn_ref[...])
                                         * (gb * _sigmoid(gb))).astype(o_ref.dtype)
        return carry

    lax.fori_loop(0, nchunk, chunk, 0)


GLA_BATCH_PER_STEP = 8
GLA_TIME_TILE = 128


def _gla(q, k, v, gb, ad, aw, ab, gn):
    B, Tp, _ = q.shape
    nb = GLA_BATCH_PER_STEP if B % GLA_BATCH_PER_STEP == 0 else 1
    tc = GLA_TIME_TILE
    kw, vw = GLA_HEADS * GLA_DK, GLA_HEADS * GLA_DV
    spec = lambda w: pl.BlockSpec((nb, tc, w), lambda b, t: (b, t, 0))
    return pl.pallas_call(
        functools.partial(_gla_kernel, nchunk=tc // GLA_CHUNK, nb=nb),
        out_shape=jax.ShapeDtypeStruct((B, Tp, vw), BF16),
        grid=(B // nb, Tp // tc),
        in_specs=[spec(kw), spec(kw), spec(vw), spec(vw), spec(LANES),
                  _const_spec(aw.shape), _const_spec(ab.shape), _const_spec(gn.shape)],
        out_specs=spec(vw),
        scratch_shapes=[pltpu.VMEM((nb, GLA_HEADS, GLA_DV, GLA_DK), F32)],
        compiler_params=_params(("parallel", "arbitrary")),
        name="gla",
    )(q, k, v, gb, ad, aw, ab, gn)


def _even_out_value(h_ref, ya_ref, ob_ref, w_ref):
    n = ya_ref.shape[-1]
    return (h_ref[...]
            + jnp.dot(ya_ref[...], w_ref[:n, :], preferred_element_type=F32)
            + jnp.dot(ob_ref[...], w_ref[n:, :], preferred_element_type=F32))


def _even_out_odd_in_kernel(h_ref, ya_ref, ob_ref, wo_ref, *refs):
    hout_ref = refs[-5]
    h = _even_out_value(h_ref, ya_ref, ob_ref, wo_ref)
    hout_ref[...] = h
    _odd_in_body(h, *refs[:-5], *refs[-4:])


def _even_out_odd_in(h2, ya, ob, wo, params, cos_t, sin_t):
    M = h2.shape[0]
    tm = _row_tile(M)
    return pl.pallas_call(
        _even_out_odd_in_kernel,
        out_shape=(jax.ShapeDtypeStruct((M, D_MODEL), F32),) + _odd_in_outs(M),
        grid=(M // tm,),
        in_specs=[_row_spec(tm, D_MODEL), _row_spec(tm, ya.shape[-1]), _row_spec(tm, ob.shape[-1]),
                  _const_spec(wo.shape)] + _odd_in_specs(tm, params),
        out_specs=(_row_spec(tm, D_MODEL),) + tuple(_row_spec(tm, n) for n in ODD_IN_WIDTHS),
        compiler_params=_params(("parallel",)),
        name="even_out_odd_in",
    )(h2, ya, ob, wo, *params, cos_t, sin_t)


ROPE_HALF = MLA_ROPE // 2
ROPE_X2_LANE = HEAD_PAD // 2


def _head_lanes(nope, rope_part):
    zeros = lambda n: jnp.zeros(nope.shape[:-1] + (n,), nope.dtype)
    n1 = ROPE_X2_LANE - ROPE_HALF
    return jnp.concatenate([rope_part[..., :ROPE_HALF], nope[..., :n1], rope_part[..., ROPE_HALF:],
                            nope[..., n1:], zeros(HEAD_PAD - MLA_NOPE - MLA_ROPE)], axis=-1)


def _rope_table_kernel(pos_ref, invf_ref, c_ref, s_ref):
    ang = pos_ref[...] * invf_ref[...]
    lane = lax.broadcasted_iota(jnp.int32, ang.shape, 1)
    c_ref[...] = jnp.cos(ang)
    sn = jnp.sin(ang)
    s_ref[...] = jnp.where(lane < ROPE_X2_LANE, -sn, sn)


def _rope_tables(posf, invf):
    M = posf.shape[0]
    tm = _pick(M, (512, 256, 128))
    out = jax.ShapeDtypeStruct((M, HEAD_PAD), F32)
    return pl.pallas_call(
        _rope_table_kernel,
        out_shape=(out, out),
        grid=(M // tm,),
        in_specs=[pl.BlockSpec((tm, 1), lambda i: (i, 0)), _const_spec((1, HEAD_PAD))],
        out_specs=(pl.BlockSpec((tm, HEAD_PAD), lambda i: (i, 0)),) * 2,
        compiler_params=_params(("parallel",)),
        name="rope_tables",
    )(posf, invf)


def _rope(x, cos_t, sin_t):
    heads = [pltpu.roll(x[:, c:c + HEAD_PAD], ROPE_X2_LANE, axis=1) for c in range(0, x.shape[1], HEAD_PAD)]
    partner = heads[0] if len(heads) == 1 else jnp.concatenate(heads, axis=1)
    return x * cos_t + partner * sin_t


def _odd_in_body(h, g_ref, w_ref, qn_ref, wq_ref, kvn_ref, wk_ref, wv_ref, cos_ref, sin_ref,
                 q_ref, k_ref, v_ref, gate_ref):
    xn = _rms(h, g_ref[...]).astype(BF16)
    c0, c1, c2 = MLA_Q_RANK, MLA_Q_RANK + MLA_KV_RANK, MLA_Q_RANK + MLA_KV_RANK + HEAD_PAD
    gate_ref[...] = jnp.dot(xn, w_ref[:, c2:], preferred_element_type=F32).astype(gate_ref.dtype)
    cq = jnp.dot(xn, w_ref[:, :c0], preferred_element_type=F32)
    ckv = jnp.dot(xn, w_ref[:, c0:c1], preferred_element_type=F32)
    kr = jnp.dot(xn, w_ref[:, c1:c2], preferred_element_type=F32)
    cos_t, sin_t = cos_ref[...], sin_ref[...]
    kr = _rope(kr, cos_t, sin_t)
    cqn = _rms(cq, qn_ref[...]).astype(BF16)
    ckvn = _rms(ckv, kvn_ref[...]).astype(BF16)
    v_ref[...] = jnp.dot(ckvn, wv_ref[...], preferred_element_type=F32).astype(v_ref.dtype)
    scale = LOG2_E * (MLA_NOPE + MLA_ROPE) ** -0.5
    cos2 = jnp.concatenate([cos_t, cos_t], axis=1) * scale
    sin2 = jnp.concatenate([sin_t, sin_t], axis=1) * scale
    kr2 = jnp.concatenate([kr, kr], axis=1)
    for hp in range(MLA_HEADS // 2):
        lanes = slice(hp * 2 * HEAD_PAD, (hp + 1) * 2 * HEAD_PAD)
        qh = jnp.dot(cqn, wq_ref[:, lanes], preferred_element_type=F32)
        q_ref[:, lanes] = _rope(qh, cos2, sin2).astype(q_ref.dtype)
        kh = jnp.dot(ckvn, wk_ref[:, lanes], preferred_element_type=F32)
        k_ref[:, lanes] = (kh + kr2).astype(k_ref.dtype)


ODD_IN_WIDTHS = (MLA_HEADS * HEAD_PAD, MLA_HEADS * HEAD_PAD, MLA_HEADS * MLA_V, MLA_HEADS * MLA_V)


def _odd_in_outs(M):
    return tuple(jax.ShapeDtypeStruct((M, n), BF16) for n in ODD_IN_WIDTHS)


def _odd_in_specs(tm, params):
    return [_const_spec(p.shape) for p in params] + [_row_spec(tm, HEAD_PAD), _row_spec(tm, HEAD_PAD)]


def _attn_kernel(q_ref, k_ref, v_ref, o_ref, k_scr, vt_scr, qt_scr, bias_scr, s_a, s_b, mx_a, mx_b,
                 m_scr, acc_scr, o_scr, *, nh, n_q):
    T = ATT_TILE
    Tp = q_ref.shape[1]
    Tp2 = n_q * T
    SL = SOFTMAX_SLAB
    VE = ATT_V_EXT

    k_scr[0:Tp, :] = k_ref[0]
    if Tp2 > Tp:
        k_scr[Tp:Tp2, :] = jnp.zeros((Tp2 - Tp, k_scr.shape[1]), k_scr.dtype)
    ones_row = (lax.broadcasted_iota(jnp.int32, (VE - MLA_V, Tp2), 0) == 0).astype(vt_scr.dtype)
    for h in range(nh):
        vt_scr[h * VE + MLA_V:(h + 1) * VE, :] = ones_row
        if Tp2 > Tp:
            vt_scr[h * VE:h * VE + MLA_V, Tp:Tp2] = jnp.zeros((MLA_V, Tp2 - Tp), vt_scr.dtype)
    for c in range(0, Tp, LANES):
        vt = v_ref[0, c:c + LANES, :].T
        for h in range(nh):
            vt_scr[h * VE:h * VE + MLA_V, c:c + LANES] = vt[h * MLA_V:(h + 1) * MLA_V, :]
    for j in range(n_q):
        rows = min(T, Tp - j * T)
        for h in range(nh):
            qh = q_ref[0, j * T:j * T + rows, h * HEAD_PAD:(h + 1) * HEAD_PAD]
            qt_scr[j, h, :, 0:rows] = qh.T
            if rows < T:
                qt_scr[j, h, :, rows:T] = jnp.zeros((HEAD_PAD, T - rows), qt_scr.dtype)
    kk = lax.broadcasted_iota(jnp.int32, (T, T), 0)
    qq = lax.broadcasted_iota(jnp.int32, (T, T), 1)
    bias_scr[...] = jnp.where(kk <= qq, 0.0, NEG_BIG)

    def qk(buf, j, i, diag=False):
        s_ref, mx_ref = buf
        koff = pl.multiple_of(i * T, T)
        for h in range(nh):
            sv = jnp.dot(k_scr[pl.ds(koff, T), h * HEAD_PAD:(h + 1) * HEAD_PAD],
                         qt_scr[j, h], preferred_element_type=F32)
            if diag:
                sv = sv + bias_scr[...]
            s_ref[h] = sv
            parts = [None] * 4
            for n, r in enumerate(range(0, T, SL)):
                blk = sv[r:r + SL, :]
                parts[n % 4] = blk if parts[n % 4] is None else jnp.maximum(parts[n % 4], blk)
            mt = jnp.maximum(jnp.maximum(parts[0], parts[1]), jnp.maximum(parts[2], parts[3]))
            mx_ref[h] = jnp.max(mt, axis=0, keepdims=True)

    def softmax_pv(buf, j, i, diag):
        s_ref, mx_ref = buf
        koff = pl.multiple_of(i * T, T)
        for h in range(nh):
            if diag:
                m_new = mx_ref[h]
            else:
                m_old = m_scr[j, h]
                m_new = jnp.maximum(m_old, mx_ref[h])
                alpha = jnp.exp2(m_old - m_new)
            m_scr[j, h] = m_new
            p = jnp.concatenate([jnp.exp2((s_ref[h, r:r + SL, :] - m_new).astype(BF16))
                                 for r in range(0, T, SL)], axis=0)
            rows = slice(h * VE, (h + 1) * VE)
            pv = jnp.dot(vt_scr[rows, pl.ds(koff, T)], p, preferred_element_type=F32)
            acc_scr[j, rows, :] = pv if diag else alpha * acc_scr[j, rows, :] + pv

    bufs = ((s_a, mx_a), (s_b, mx_b))
    G = ATT_STEPS_PER_ITER

    def run(n_steps, first, advance, diag):
        def steps(count, jt):
            for t in range(count):
                nxt = advance(*jt)
                qk(bufs[(t + 1) % 2], jnp.minimum(nxt[0], n_q - 1), nxt[1], diag)
                softmax_pv(bufs[t % 2], jt[0], jt[1], diag)
                jt = nxt
            return jt

        qk(bufs[0], first[0], first[1], diag)
        jt = lax.fori_loop(0, n_steps // G, lambda n, c: steps(G, c),
                           (jnp.int32(first[0]), jnp.int32(first[1])))
        steps(n_steps % G, jt)

    run(n_q, (0, 0), lambda j, i: (j + 1, jnp.minimum(i + 1, n_q - 1)), True)

    def advance(j, i):
        last = i + 1 >= j
        return jnp.where(last, j + 1, j), jnp.where(last, 0, i + 1)

    if n_q > 1:
        run(n_q * (n_q - 1) // 2, (1, 0), advance, False)

    def finish(j, c):
        outs = []
        for h in range(nh):
            acc = acc_scr[j, h * VE:h * VE + MLA_V, :]
            den = acc_scr[j, h * VE + MLA_V:h * VE + MLA_V + 1, :]
            outs.append(acc / den)
        o_scr[pl.ds(pl.multiple_of(j * T, T), T), :] = jnp.concatenate(outs, axis=0).T.astype(o_scr.dtype)
        return c

    lax.fori_loop(0, n_q, finish, 0)
    o_ref[0] = o_scr[0:Tp, :]


ATT_HEADS_PER_STEP = 4
ATT_STEPS_PER_ITER = 12
ATT_V_EXT = MLA_V + 16


def _attention(q, k, v):
    B, Tp, _ = q.shape
    nh = ATT_HEADS_PER_STEP
    T = ATT_TILE
    n_q = -(-Tp // T)
    Tp2 = n_q * T
    qk_spec = pl.BlockSpec((1, Tp, nh * HEAD_PAD), lambda b, p: (b, 0, p))
    v_spec = pl.BlockSpec((1, Tp, nh * MLA_V), lambda b, p: (b, 0, p))
    return pl.pallas_call(
        functools.partial(_attn_kernel, nh=nh, n_q=n_q),
        out_shape=jax.ShapeDtypeStruct((B, Tp, MLA_HEADS * MLA_V), BF16),
        grid=(B, MLA_HEADS // nh),
        in_specs=[qk_spec, qk_spec, v_spec],
        out_specs=v_spec,
        scratch_shapes=[pltpu.VMEM((Tp2, nh * HEAD_PAD), BF16),
                        pltpu.VMEM((nh * ATT_V_EXT, Tp2), BF16),
                        pltpu.VMEM((n_q, nh, HEAD_PAD, T), BF16),
                        pltpu.VMEM((T, T), F32),
                        pltpu.VMEM((nh, T, T), F32),
                        pltpu.VMEM((nh, T, T), F32),
                        pltpu.VMEM((nh, 1, T), F32),
                        pltpu.VMEM((nh, 1, T), F32),
                        pltpu.VMEM((n_q, nh, 1, T), F32),
                        pltpu.VMEM((n_q, nh * ATT_V_EXT, T), F32),
                        pltpu.VMEM((Tp2, nh * MLA_V), BF16)],
        compiler_params=_params(("parallel", "parallel")),
        name="mla_attention",
    )(q, k, v)


def _odd_out_value(h_ref, o_ref, gate_ref, w_ref):
    gate = gate_ref[...].astype(F32)
    x = (o_ref[...].astype(F32) * (gate * _sigmoid(gate))).astype(BF16)
    return h_ref[...] + jnp.dot(x, w_ref[...], preferred_element_type=F32)


def _odd_out_even_in_kernel(h_ref, o_ref, gate_ref, wo_ref, g_ref, w_ref, hout_ref, *out_refs):
    h = _odd_out_value(h_ref, o_ref, gate_ref, wo_ref)
    hout_ref[...] = h
    _even_in_body(h, g_ref, w_ref, out_refs)


def _odd_out_even_in(h2, o, gate, wo, g, w):
    M = h2.shape[0]
    tm = _row_tile(M)
    return pl.pallas_call(
        _odd_out_even_in_kernel,
        out_shape=(jax.ShapeDtypeStruct((M, D_MODEL), F32),) + _even_in_outs(M),
        grid=(M // tm,),
        in_specs=[_row_spec(tm, D_MODEL), _row_spec(tm, o.shape[-1]), _row_spec(tm, gate.shape[-1]),
                  _const_spec(wo.shape), _const_spec(g.shape), _const_spec(w.shape)],
        out_specs=(_row_spec(tm, D_MODEL),) + tuple(_row_spec(tm, n) for n in EVEN_OUT_WIDTHS),
        compiler_params=_params(("parallel",)),
        name="odd_out_even_in",
    )(h2, o, gate, wo, g, w)


def _odd_out_last_kernel(h_ref, o_ref, gate_ref, w_ref, fn_ref, out_ref):
    out_ref[...] = _rms(_odd_out_value(h_ref, o_ref, gate_ref, w_ref), fn_ref[...])


def _odd_out_last(h2, o, gate, w, fn, B, Tp, S):
    tm = _pick(S, (512, 256, 128, 64))
    win = lambda n: pl.BlockSpec((pl.Element(tm), pl.Element(n)),
                                 lambda b, i: (pl.multiple_of(b * Tp + N_META + i * tm, N_META), 0))
    return pl.pallas_call(
        _odd_out_last_kernel,
        out_shape=jax.ShapeDtypeStruct((B * S, D_MODEL), F32),
        grid=(B, S // tm),
        in_specs=[win(D_MODEL), win(o.shape[-1]), win(gate.shape[-1]), _const_spec(w.shape),
                  _const_spec(fn.shape)],
        out_specs=pl.BlockSpec((tm, D_MODEL), lambda b, i: (b * (S // tm) + i, 0)),
        compiler_params=_params(("parallel", "parallel")),
        name="odd_out_last",
    )(h2, o, gate, w, fn)


def _split_cols(w, sizes):
    idx = np.cumsum(sizes)[:-1].tolist()
    return jnp.split(w, idx, axis=-1)


def _pack_even(w_in, gate_a_w, gate_x_w, alpha_w):
    xa, ga, q, k, v, ad, gb = _split_cols(
        w_in, (RNN_WIDTH, RNN_WIDTH, GLA_HEADS * GLA_DK, GLA_HEADS * GLA_DK, GLA_HEADS * GLA_DV,
               GLA_GATE_RANK, GLA_HEADS * GLA_DV))
    ad = jnp.pad(ad, ((0, 0), (0, LANES - GLA_GATE_RANK)))
    w = jnp.concatenate([xa, ga, q, k, v, gb, ad], axis=-1).astype(BF16)
    wg = jnp.concatenate([gate_a_w, gate_x_w], axis=-1).astype(BF16)
    aw = jnp.pad(alpha_w, ((0, LANES - GLA_GATE_RANK), (0, 0))).astype(BF16)
    return w, wg, aw


def _pack_odd(w_in, w_q_up, w_kv_up):
    cq, ckv, kr, gate = _split_cols(w_in, (MLA_Q_RANK, MLA_KV_RANK, MLA_ROPE, MLA_HEADS * MLA_V))
    kr = _head_lanes(jnp.zeros((kr.shape[0], MLA_NOPE), kr.dtype), kr)
    w = jnp.concatenate([cq, ckv, kr, gate], axis=-1).astype(BF16)
    wq = w_q_up.reshape(MLA_Q_RANK, MLA_HEADS, MLA_NOPE + MLA_ROPE)
    wq = _head_lanes(wq[..., :MLA_NOPE], wq[..., MLA_NOPE:]).reshape(MLA_Q_RANK, MLA_HEADS * HEAD_PAD)
    wkv = w_kv_up.reshape(MLA_KV_RANK, MLA_HEADS, MLA_NOPE + MLA_V)
    wk = _head_lanes(wkv[..., :MLA_NOPE], jnp.zeros(wkv.shape[:-1] + (MLA_ROPE,), wkv.dtype))
    wk = wk.reshape(MLA_KV_RANK, MLA_HEADS * HEAD_PAD)
    wv = wkv[..., MLA_NOPE:].reshape(MLA_KV_RANK, MLA_HEADS * MLA_V)
    return w, wq.astype(BF16), wk.astype(BF16), wv.astype(BF16)


def kernel(x, positions, meta_tokens, ab_norm, ab_w_in, ab_conv_w, ab_conv_b, ab_gate_a_w, ab_gate_a_b, ab_gate_x_w, ab_gate_x_b, ab_lru_lambda, ab_alpha_w, ab_alpha_b, ab_gla_norm, ab_w_out, c_norm, c_w_in, c_q_norm, c_w_q_up, c_kv_norm, c_w_kv_up, c_w_out, final_norm):
    B, S, D = x.shape
    T = N_META + S
    Tp = -(-T // LANES) * LANES
    M = B * Tp
    depth = ab_norm.shape[0] + c_norm.shape[0]
    row2 = lambda a: a.reshape(1, -1).astype(F32)

    meta = jnp.broadcast_to(meta_tokens.astype(x.dtype)[None], (B, N_META, D))
    h = jnp.concatenate([meta, x, jnp.zeros((B, Tp - T, D), x.dtype)], axis=1).reshape(M, D)

    meta_pos = jnp.broadcast_to(jnp.arange(N_META, dtype=positions.dtype)[None], (B, N_META))
    pos = jnp.concatenate([meta_pos, positions + N_META,
                           jnp.zeros((B, Tp - T), positions.dtype)], axis=1)
    inv_freq = ROPE_BASE ** (-jnp.arange(0, MLA_ROPE, 2, dtype=F32) / MLA_ROPE)
    invf = _head_lanes(jnp.zeros((1, MLA_NOPE), F32), jnp.concatenate([inv_freq, inv_freq])[None])
    cos_t, sin_t = _rope_tables(pos.astype(F32).reshape(M, 1), invf)

    if depth % 2 or ab_norm.shape[0] != c_norm.shape[0]:
        raise NotImplementedError("layers must alternate (RG-LRU || GLA), MLA and end on an MLA layer")
    r3 = lambda a: a.reshape(B, Tp, a.shape[-1])
    even_in = None
    even_packs = [_pack_even(ab_w_in[j], ab_gate_a_w[j], ab_gate_x_w[j], ab_alpha_w[j])
                  for j in range(depth // 2)]
    for j in range(depth // 2):
        w, wg, aw = even_packs[j]
        if even_in is None:
            even_in = _even_in(h, row2(ab_norm[j]), w)
        xa, ga, q, k, v, gb, ad = even_in
        ya = _rglru(r3(xa), r3(ga), ab_conv_w[j].astype(F32), row2(ab_conv_b[j]), wg,
                    row2(ab_gate_a_b[j]), row2(ab_gate_x_b[j]), row2(ab_lru_lambda[j]))
        ob = _gla(r3(q), r3(k), r3(v), r3(gb), r3(ad), aw, row2(ab_alpha_b[j]), row2(ab_gla_norm[j]))
        w, wq, wk, wv = _pack_odd(c_w_in[j], c_w_q_up[j], c_w_kv_up[j])
        odd_params = (row2(c_norm[j]), w, row2(c_q_norm[j]), wq, row2(c_kv_norm[j]), wk, wv)
        h, q, k, v, gate = _even_out_odd_in(h, ya.reshape(M, -1), ob.reshape(M, -1),
                                            ab_w_out[j].astype(BF16), odd_params, cos_t, sin_t)
        o = _attention(r3(q), r3(k), r3(v)).reshape(M, -1)
        wo = c_w_out[j].astype(BF16)
        if j == depth // 2 - 1:
            return _odd_out_last(h, o, gate, wo, row2(final_norm), B, Tp, S).reshape(B, S, D)
        h, *even_in = _odd_out_even_in(h, o, gate, wo, row2(ab_norm[j + 1]), even_packs[j + 1][0])
```

```python
import functools

import jax
import jax.numpy as jnp
import numpy as np
from jax import lax
from jax.experimental import pallas as pl
from jax.experimental.pallas import tpu as pltpu

F32 = jnp.float32
BF16 = jnp.bfloat16

D_MODEL = 1024
N_META = 16
EPS = 1e-6
RNN_WIDTH = D_MODEL
RNN_BLOCKS = 8
RNN_BLOCK = RNN_WIDTH // RNN_BLOCKS
CONV_WIDTH = 4
RGLRU_C = 8.0
GLA_HEADS = 4
GLA_DK = 128
GLA_DV = 256
GLA_GATE_RANK = 16
GLA_TAU = 16.0
GLA_CHUNK = 64
MLA_HEADS = 16
MLA_NOPE = 64
MLA_ROPE = 32
MLA_V = 64
MLA_Q_RANK = 512
MLA_KV_RANK = 256
ROPE_BASE = 10000.0

LANES = 128
HEAD_PAD = 128
ATT_TILE = 256
SOFTMAX_SLAB = 32
LOG2_E = 1.4426950408889634
NEG_BIG = -1e30
VMEM_LIMIT = 56 * 1024 * 1024


def _pick(n, candidates):
    for c in candidates:
        if n % c == 0:
            return c
    raise ValueError(f"no tile in {candidates} divides {n}")


def _rms(x, g):
    var = jnp.mean(x * x, axis=-1, keepdims=True)
    return x * lax.rsqrt(var + EPS) * g


def _sigmoid(x):
    return 1.0 / (1.0 + jnp.exp(-x))


def _const_spec(shape):
    nd = len(shape)
    return pl.BlockSpec(shape, lambda *_: (0,) * nd, pipeline_mode=pl.Buffered(1))


def _params(sem):
    return pltpu.CompilerParams(dimension_semantics=sem, vmem_limit_bytes=VMEM_LIMIT)


EVEN_OUT_WIDTHS = (RNN_WIDTH, RNN_WIDTH, GLA_HEADS * GLA_DK, GLA_HEADS * GLA_DK,
                   GLA_HEADS * GLA_DV, GLA_HEADS * GLA_DV, LANES)


def _even_in_body(h, g_ref, w_ref, out_refs):
    xn = _rms(h, g_ref[...]).astype(BF16)
    off = 0
    for ref in out_refs:
        n = ref.shape[-1]
        ref[...] = jnp.dot(xn, w_ref[:, off:off + n], preferred_element_type=F32).astype(ref.dtype)
        off += n


def _embed_even_in_kernel(x_ref, meta_ref, g_ref, w_ref, h_ref, *out_refs, tm, n_tiles, seq, win0):
    k_id = pl.program_id(1)
    for k in range(n_tiles):
        @pl.when(k_id == k)
        def _(k=k):
            t0 = k * tm
            n_meta = max(0, min(N_META - t0, tm))
            x_lo = t0 + n_meta - N_META
            n_x = max(0, min(seq - x_lo, tm - n_meta))
            if n_meta:
                h_ref[0:n_meta, :] = meta_ref[t0:t0 + n_meta, :]
            if n_x:
                off = x_lo - win0[k]
                h_ref[n_meta:n_meta + n_x, :] = x_ref[off:off + n_x, :]
            if n_meta + n_x < tm:
                h_ref[n_meta + n_x:tm, :] = jnp.zeros((tm - n_meta - n_x, h_ref.shape[1]), h_ref.dtype)
    _even_in_body(h_ref[...], g_ref, w_ref, out_refs)


def _embed_even_in(x, meta, g, w, Tp):
    B, S, D = x.shape
    tm = _pick(Tp, tuple(c for c in (544, 512, 384, 256, 128) if c <= S))
    n_tiles = Tp // tm
    M = B * Tp
    win0 = tuple(min(max(k * tm - N_META, 0), S - tm) for k in range(n_tiles))

    def x_index(b, k):
        start = jnp.clip(k * tm - N_META, 0, S - tm)
        return (pl.multiple_of(b * S + start, N_META), 0)

    out_spec = lambda n: pl.BlockSpec((tm, n), lambda b, k: (b * n_tiles + k, 0))
    return pl.pallas_call(
        functools.partial(_embed_even_in_kernel, tm=tm, n_tiles=n_tiles, seq=S, win0=win0),
        out_shape=(jax.ShapeDtypeStruct((M, D), F32),) + _even_in_outs(M),
        grid=(B, n_tiles),
        in_specs=[pl.BlockSpec((pl.Element(tm), pl.Element(D)), x_index),
                  _const_spec(meta.shape), _const_spec(g.shape), _const_spec(w.shape)],
        out_specs=(out_spec(D),) + tuple(out_spec(n) for n in EVEN_OUT_WIDTHS),
        compiler_params=_params(("parallel", "parallel")),
        name="embed_even_in",
    )(x.reshape(B * S, D), meta, g, w)


def _row_tile(M):
    return _pick(M, (512, 256, 128))


def _row_spec(tm, n):
    return pl.BlockSpec((tm, n), lambda i: (i, 0))


def _even_in_outs(M):
    return tuple(jax.ShapeDtypeStruct((M, n), BF16) for n in EVEN_OUT_WIDTHS)


def _lru_pitch(tt):
    return tt if (tt // 8) % 2 else tt + 8


def _rglru_kernel(xa_ref, ga_ref, cw_ref, cb_ref, wg_ref, ba_ref, bx_ref, lam_ref, y_ref,
                  flat, u_s, a_s, hist, h_s, *, tt, pitch, nblk):
    B = xa_ref.shape[0]
    t_idx = pl.program_id(1)

    @pl.when(t_idx == 0)
    def _():
        hist[...] = jnp.zeros_like(hist)
        h_s[...] = jnp.zeros_like(h_s)

    for j in range(nblk):
        lanes = slice(j * LANES, (j + 1) * LANES)
        for b in range(B):
            flat[j, b * pitch:b * pitch + tt, :] = xa_ref[b, :, lanes].astype(F32)

    taps = [[cw_ref[k:k + 1, j * LANES:(j + 1) * LANES] for k in range(CONV_WIDTH)] for j in range(nblk)]
    bias = [cb_ref[:, j * LANES:(j + 1) * LANES] for j in range(nblk)]

    def conv_step(t, carry):
        out = []
        for j in range(nblk):
            x1, x2, x3 = carry[j]
            x0 = flat[j, pl.ds(t, B, stride=pitch), :]
            w = taps[j]
            u_s[j, pl.ds(pl.multiple_of(t * B, B), B), :] = (
                bias[j] + w[3] * x0 + w[2] * x1 + w[1] * x2 + w[0] * x3)
            out.append((x0, x1, x2))
        return tuple(out)

    last = lax.fori_loop(0, tt, conv_step,
                         tuple((hist[j, 0], hist[j, 1], hist[j, 2]) for j in range(nblk)), unroll=8)
    for j in range(nblk):
        for k in range(CONV_WIDTH - 1):
            hist[j, k] = last[j][k]

    for j in range(nblk):
        lanes = slice(j * LANES, (j + 1) * LANES)
        y = u_s[j]
        g = jnp.dot(y.astype(BF16), wg_ref[j], preferred_element_type=F32)
        r = _sigmoid(g[:, :LANES] + ba_ref[:, lanes])
        i = _sigmoid(g[:, LANES:] + bx_ref[:, lanes])
        lam = lam_ref[:, lanes]
        softplus_neg_lam = jnp.maximum(-lam, 0.0) + jnp.log(1.0 + jnp.exp(-jnp.abs(lam)))
        a = jnp.exp(r * ((-RGLRU_C) * softplus_neg_lam))
        x = 1.0 - a * a
        a_s[j] = a
        u_s[j] = (x * lax.rsqrt(jnp.maximum(x, 1e-30))) * (i * y)

    def scan_step(t, hs):
        out = []
        for j in range(nblk):
            rows = pl.ds(pl.multiple_of(t * B, B), B)
            h = a_s[j, rows, :] * hs[j] + u_s[j, rows, :]
            flat[j, pl.ds(t, B, stride=pitch), :] = h
            out.append(h)
        return tuple(out)

    hs = lax.fori_loop(0, tt, scan_step, tuple(h_s[j] for j in range(nblk)), unroll=8)
    for j in range(nblk):
        h_s[j] = hs[j]

    for j in range(nblk):
        lanes = slice(j * LANES, (j + 1) * LANES)
        for b in range(B):
            ga = ga_ref[b, :, lanes].astype(F32)
            y_ref[b, :, lanes] = (flat[j, b * pitch:b * pitch + tt, :]
                                  * (ga * _sigmoid(ga))).astype(y_ref.dtype)


def _rglru(xa, ga, cw, cb, wg, ba, bx, lam):
    B, Tp, C = xa.shape
    tt = _pick(Tp, (272, 256, 192, 128))
    pitch = _lru_pitch(tt)
    cblk = 512
    nblk = cblk // LANES
    blk = pl.BlockSpec((B, tt, cblk), lambda c, t: (0, t, c))
    vec = lambda rows: pl.BlockSpec((rows, cblk), lambda c, t: (0, c))
    return pl.pallas_call(
        functools.partial(_rglru_kernel, tt=tt, pitch=pitch, nblk=nblk),
        out_shape=jax.ShapeDtypeStruct((B, Tp, C), BF16),
        grid=(C // cblk, Tp // tt),
        in_specs=[blk, blk, vec(CONV_WIDTH), vec(1),
                  pl.BlockSpec((nblk, RNN_BLOCK, 2 * RNN_BLOCK), lambda c, t: (c, 0, 0)),
                  vec(1), vec(1), vec(1)],
        out_specs=blk,
        scratch_shapes=[pltpu.VMEM((nblk, B * pitch, LANES), F32),
                        pltpu.VMEM((nblk, tt * B, LANES), F32),
                        pltpu.VMEM((nblk, tt * B, LANES), F32),
                        pltpu.VMEM((nblk, CONV_WIDTH - 1, B, LANES), F32),
                        pltpu.VMEM((nblk, B, LANES), F32)],
        compiler_params=_params(("parallel", "arbitrary")),
        name="rglru",
    )(xa, ga, cw, cb, wg, ba, bx, lam)


def _gla_kernel(q_ref, k_ref, v_ref, gb_ref, ad_ref, aw_ref, ab_ref, gn_ref, o_ref, st_ref, *, nchunk, nb):
    C = GLA_CHUNK

    @pl.when(pl.program_id(1) == 0)
    def _():
        st_ref[...] = jnp.zeros_like(st_ref)

    row = lax.broadcasted_iota(jnp.int32, (C, C), 0)
    col = lax.broadcasted_iota(jnp.int32, (C, C), 1)
    causal = row >= col
    tril = causal.astype(BF16)
    scale = GLA_DK ** -0.5

    def chunk(c, carry):
        rows = pl.ds(pl.multiple_of(c * C, C), C)
        bs = range(nb)
        hs = range(GLA_HEADS)
        kl = [slice(h * GLA_DK, (h + 1) * GLA_DK) for h in hs]
        vl = [slice(h * GLA_DV, (h + 1) * GLA_DV) for h in hs]
        pre = [jnp.dot(ad_ref[b, rows, :], aw_ref[...], preferred_element_type=F32) + ab_ref[...]
               for b in bs]
        la = [(jnp.minimum(p, 0.0) - jnp.log(1.0 + jnp.exp(-jnp.abs(p)))) * (1.0 / GLA_TAU) for p in pre]
        la_hi = [x.astype(BF16) for x in la]
        la_lo = [(x - hi.astype(F32)).astype(BF16) for x, hi in zip(la, la_hi)]
        bc = [jnp.dot(tril, hi, preferred_element_type=F32) + jnp.dot(tril, lo, preferred_element_type=F32)
              for hi, lo in zip(la_hi, la_lo)]
        b_last = [x[C - 1:C, :] for x in bc]
        q = [q_ref[b, rows, :].astype(F32) for b in bs]
        k = [k_ref[b, rows, :].astype(F32) for b in bs]
        q_dec = [(q[b] * (jnp.exp(bc[b]) * scale)).astype(BF16) for b in bs]
        k_inv = [(k[b] * jnp.exp(-bc[b])).astype(BF16) for b in bs]
        k_end = [(k[b] * jnp.exp(b_last[b] - bc[b])).astype(BF16) for b in bs]
        decay = [jnp.exp(x) for x in b_last]
        s = [[jnp.where(causal,
                        lax.dot_general(q_dec[b][:, kl[h]], k_inv[b][:, kl[h]], (((1,), (1,)), ((), ())),
                                        preferred_element_type=F32), 0.0).astype(BF16)
              for h in hs] for b in bs]
        o = [[None] * GLA_HEADS for _ in bs]
        for b in bs:
            for h in hs:
                vh = v_ref[b, rows, vl[h]]
                st = st_ref[b, h]
                o[b][h] = (jnp.dot(s[b][h], vh, preferred_element_type=F32)
                           + lax.dot_general(q_dec[b][:, kl[h]], st.astype(BF16), (((1,), (1,)), ((), ())),
                                             preferred_element_type=F32))
                vk = lax.dot_general(vh, k_end[b][:, kl[h]], (((0,), (0,)), ((), ())),
                                     preferred_element_type=F32)
                st_ref[b, h] = st * decay[b][:, kl[h]] + vk
        for b in bs:
            for h in hs:
                gb = gb_ref[b, rows, vl[h]].astype(F32)
                o_ref[b, rows, vl[h]] = (_rms(o[b][h], gn_ref[...])
                                         * (gb * _sigmoid(gb))).astype(o_ref.dtype)
        return carry

    lax.fori_loop(0, nchunk, chunk, 0)


GLA_BATCH_PER_STEP = 8
GLA_TIME_TILE = 128


def _gla(q, k, v, gb, ad, aw, ab, gn):
    B, Tp, _ = q.shape
    nb = GLA_BATCH_PER_STEP if B % GLA_BATCH_PER_STEP == 0 else 1
    tc = GLA_TIME_TILE
    kw, vw = GLA_HEADS * GLA_DK, GLA_HEADS * GLA_DV
    spec = lambda w: pl.BlockSpec((nb, tc, w), lambda b, t: (b, t, 0))
    return pl.pallas_call(
        functools.partial(_gla_kernel, nchunk=tc // GLA_CHUNK, nb=nb),
        out_shape=jax.ShapeDtypeStruct((B, Tp, vw), BF16),
        grid=(B // nb, Tp // tc),
        in_specs=[spec(kw), spec(kw), spec(vw), spec(vw), spec(LANES),
                  _const_spec(aw.shape), _const_spec(ab.shape), _const_spec(gn.shape)],
        out_specs=spec(vw),
        scratch_shapes=[pltpu.VMEM((nb, GLA_HEADS, GLA_DV, GLA_DK), F32)],
        compiler_params=_params(("parallel", "arbitrary")),
        name="gla",
    )(q, k, v, gb, ad, aw, ab, gn)


def _even_out_value(h_ref, ya_ref, ob_ref, w_ref):
    n = ya_ref.shape[-1]
    return (h_ref[...]
            + jnp.dot(ya_ref[...], w_ref[:n, :], preferred_element_type=F32)
            + jnp.dot(ob_ref[...], w_ref[n:, :], preferred_element_type=F32))


def _even_out_odd_in_kernel(h_ref, ya_ref, ob_ref, wo_ref, *refs):
    hout_ref = refs[-5]
    h = _even_out_value(h_ref, ya_ref, ob_ref, wo_ref)
    hout_ref[...] = h
    _odd_in_body(h, *refs[:-5], *refs[-4:])


def _even_out_odd_in(h2, ya, ob, wo, params, cos_t, sin_t):
    M = h2.shape[0]
    tm = _row_tile(M)
    return pl.pallas_call(
        _even_out_odd_in_kernel,
        out_shape=(jax.ShapeDtypeStruct((M, D_MODEL), F32),) + _odd_in_outs(M),
        grid=(M // tm,),
        in_specs=[_row_spec(tm, D_MODEL), _row_spec(tm, ya.shape[-1]), _row_spec(tm, ob.shape[-1]),
                  _const_spec(wo.shape)] + _odd_in_specs(tm, params),
        out_specs=(_row_spec(tm, D_MODEL),) + tuple(_row_spec(tm, n) for n in ODD_IN_WIDTHS),
        compiler_params=_params(("parallel",)),
        name="even_out_odd_in",
    )(h2, ya, ob, wo, *params, cos_t, sin_t)


ROPE_HALF = MLA_ROPE // 2
ROPE_X2_LANE = HEAD_PAD // 2


def _head_lanes(nope, rope_part):
    zeros = lambda n: jnp.zeros(nope.shape[:-1] + (n,), nope.dtype)
    n1 = ROPE_X2_LANE - ROPE_HALF
    return jnp.concatenate([rope_part[..., :ROPE_HALF], nope[..., :n1], rope_part[..., ROPE_HALF:],
                            nope[..., n1:], zeros(HEAD_PAD - MLA_NOPE - MLA_ROPE)], axis=-1)


def _rope_table_kernel(pos_ref, invf_ref, c_ref, s_ref):
    ang = pos_ref[...] * invf_ref[...]
    lane = lax.broadcasted_iota(jnp.int32, ang.shape, 1)
    c_ref[...] = jnp.cos(ang)
    sn = jnp.sin(ang)
    s_ref[...] = jnp.where(lane < ROPE_X2_LANE, -sn, sn)


def _rope_tables(posf, invf):
    M = posf.shape[0]
    tm = _pick(M, (512, 256, 128))
    out = jax.ShapeDtypeStruct((M, HEAD_PAD), F32)
    return pl.pallas_call(
        _rope_table_kernel,
        out_shape=(out, out),
        grid=(M // tm,),
        in_specs=[pl.BlockSpec((tm, 1), lambda i: (i, 0)), _const_spec((1, HEAD_PAD))],
        out_specs=(pl.BlockSpec((tm, HEAD_PAD), lambda i: (i, 0)),) * 2,
        compiler_params=_params(("parallel",)),
        name="rope_tables",
    )(posf, invf)


def _rope(x, cos_t, sin_t):
    heads = [pltpu.roll(x[:, c:c + HEAD_PAD], ROPE_X2_LANE, axis=1) for c in range(0, x.shape[1], HEAD_PAD)]
    partner = heads[0] if len(heads) == 1 else jnp.concatenate(heads, axis=1)
    return x * cos_t + partner * sin_t


def _odd_in_body(h, g_ref, w_ref, qn_ref, wq_ref, kvn_ref, wk_ref, wv_ref, cos_ref, sin_ref,
                 q_ref, k_ref, v_ref, gate_ref):
    xn = _rms(h, g_ref[...]).astype(BF16)
    c0, c1, c2 = MLA_Q_RANK, MLA_Q_RANK + MLA_KV_RANK, MLA_Q_RANK + MLA_KV_RANK + HEAD_PAD
    gate_ref[...] = jnp.dot(xn, w_ref[:, c2:], preferred_element_type=F32).astype(gate_ref.dtype)
    cq = jnp.dot(xn, w_ref[:, :c0], preferred_element_type=F32)
    ckv = jnp.dot(xn, w_ref[:, c0:c1], preferred_element_type=F32)
    kr = jnp.dot(xn, w_ref[:, c1:c2], preferred_element_type=F32)
    cos_t, sin_t = cos_ref[...], sin_ref[...]
    kr = _rope(kr, cos_t, sin_t)
    cqn = _rms(cq, qn_ref[...]).astype(BF16)
    ckvn = _rms(ckv, kvn_ref[...]).astype(BF16)
    v_ref[...] = jnp.dot(ckvn, wv_ref[...], preferred_element_type=F32).astype(v_ref.dtype)
    scale = LOG2_E * (MLA_NOPE + MLA_ROPE) ** -0.5
    cos2 = jnp.concatenate([cos_t, cos_t], axis=1) * scale
    sin2 = jnp.concatenate([sin_t, sin_t], axis=1) * scale
    kr2 = jnp.concatenate([kr, kr], axis=1)
    for hp in range(MLA_HEADS // 2):
        lanes = slice(hp * 2 * HEAD_PAD, (hp + 1) * 2 * HEAD_PAD)
        qh = jnp.dot(cqn, wq_ref[:, lanes], preferred_element_type=F32)
        q_ref[:, lanes] = _rope(qh, cos2, sin2).astype(q_ref.dtype)
        kh = jnp.dot(ckvn, wk_ref[:, lanes], preferred_element_type=F32)
        k_ref[:, lanes] = (kh + kr2).astype(k_ref.dtype)


ODD_IN_WIDTHS = (MLA_HEADS * HEAD_PAD, MLA_HEADS * HEAD_PAD, MLA_HEADS * MLA_V, MLA_HEADS * MLA_V)


def _odd_in_outs(M):
    return tuple(jax.ShapeDtypeStruct((M, n), BF16) for n in ODD_IN_WIDTHS)


def _odd_in_specs(tm, params):
    return [_const_spec(p.shape) for p in params] + [_row_spec(tm, HEAD_PAD), _row_spec(tm, HEAD_PAD)]


def _attn_kernel(q_ref, k_ref, v_ref, o_ref, k_scr, vt_scr, qt_scr, bias_scr, s_a, s_b, mx_a, mx_b,
                 m_scr, acc_scr, o_scr, *, nh, n_q):
    T = ATT_TILE
    Tp = q_ref.shape[1]
    Tp2 = n_q * T
    SL = SOFTMAX_SLAB
    VE = ATT_V_EXT

    k_scr[0:Tp, :] = k_ref[0]
    if Tp2 > Tp:
        k_scr[Tp:Tp2, :] = jnp.zeros((Tp2 - Tp, k_scr.shape[1]), k_scr.dtype)
    ones_row = (lax.broadcasted_iota(jnp.int32, (VE - MLA_V, Tp2), 0) == 0).astype(vt_scr.dtype)
    for h in range(nh):
        vt_scr[h * VE + MLA_V:(h + 1) * VE, :] = ones_row
        if Tp2 > Tp:
            vt_scr[h * VE:h * VE + MLA_V, Tp:Tp2] = jnp.zeros((MLA_V, Tp2 - Tp), vt_scr.dtype)
    for c in range(0, Tp, LANES):
        vt = v_ref[0, c:c + LANES, :].T
        for h in range(nh):
            vt_scr[h * VE:h * VE + MLA_V, c:c + LANES] = vt[h * MLA_V:(h + 1) * MLA_V, :]
    for j in range(n_q):
        rows = min(T, Tp - j * T)
        for h in range(nh):
            qh = q_ref[0, j * T:j * T + rows, h * HEAD_PAD:(h + 1) * HEAD_PAD]
            qt_scr[j, h, :, 0:rows] = qh.T
            if rows < T:
                qt_scr[j, h, :, rows:T] = jnp.zeros((HEAD_PAD, T - rows), qt_scr.dtype)
    kk = lax.broadcasted_iota(jnp.int32, (T, T), 0)
    qq = lax.broadcasted_iota(jnp.int32, (T, T), 1)
    bias_scr[...] = jnp.where(kk <= qq, 0.0, NEG_BIG)

    def qk(buf, j, i, diag=False, heads=range(nh)):
        s_ref, mx_ref = buf
        koff = pl.multiple_of(i * T, T)
        for h in heads:
            sv = jnp.dot(k_scr[pl.ds(koff, T), h * HEAD_PAD:(h + 1) * HEAD_PAD],
                         qt_scr[j, h], preferred_element_type=F32)
            if diag:
                sv = sv + bias_scr[...]
            s_ref[h] = sv
            parts = [None] * 4
            for n, r in enumerate(range(0, T, SL)):
                blk = sv[r:r + SL, :]
                parts[n % 4] = blk if parts[n % 4] is None else jnp.maximum(parts[n % 4], blk)
            mt = jnp.maximum(jnp.maximum(parts[0], parts[1]), jnp.maximum(parts[2], parts[3]))
            mx_ref[h] = jnp.max(mt, axis=0, keepdims=True)

    def softmax_pv(buf, j, i, diag, heads=range(nh)):
        s_ref, mx_ref = buf
        koff = pl.multiple_of(i * T, T)
        for h in heads:
            if diag:
                m_new = mx_ref[h]
            else:
                m_old = m_scr[j, h]
                m_new = jnp.maximum(m_old, mx_ref[h])
                alpha = jnp.exp2(m_old - m_new)
            m_scr[j, h] = m_new
            p = jnp.concatenate([jnp.exp2((s_ref[h, r:r + SL, :] - m_new).astype(BF16))
                                 for r in range(0, T, SL)], axis=0)
            rows = slice(h * VE, (h + 1) * VE)
            pv = jnp.dot(vt_scr[rows, pl.ds(koff, T)], p, preferred_element_type=F32)
            acc_scr[j, rows, :] = pv if diag else alpha * acc_scr[j, rows, :] + pv

    bufs = ((s_a, mx_a), (s_b, mx_b))
    G = ATT_STEPS_PER_ITER

    def run(n_steps, first, advance, diag):
        def steps(count, jt):
            for t in range(count):
                nxt = advance(*jt)
                for h in range(nh):
                    qk(bufs[(t + 1) % 2], jnp.minimum(nxt[0], n_q - 1), nxt[1], diag, (h,))
                    softmax_pv(bufs[t % 2], jt[0], jt[1], diag, (h,))
                jt = nxt
            return jt

        qk(bufs[0], first[0], first[1], diag)
        jt = lax.fori_loop(0, n_steps // G, lambda n, c: steps(G, c),
                           (jnp.int32(first[0]), jnp.int32(first[1])))
        steps(n_steps % G, jt)

    run(n_q, (0, 0), lambda j, i: (j + 1, jnp.minimum(i + 1, n_q - 1)), True)

    def advance(j, i):
        last = i + 1 >= j
        return jnp.where(last, j + 1, j), jnp.where(last, 0, i + 1)

    if n_q > 1:
        run(n_q * (n_q - 1) // 2, (1, 0), advance, False)

    def finish(j, c):
        outs = []
        for h in range(nh):
            acc = acc_scr[j, h * VE:h * VE + MLA_V, :]
            den = acc_scr[j, h * VE + MLA_V:h * VE + MLA_V + 1, :]
            outs.append(acc / den)
        o_scr[pl.ds(pl.multiple_of(j * T, T), T), :] = jnp.concatenate(outs, axis=0).T.astype(o_scr.dtype)
        return c

    lax.fori_loop(0, n_q, finish, 0, unroll=3)
    o_ref[0] = o_scr[0:Tp, :]


ATT_HEADS_PER_STEP = 4
ATT_STEPS_PER_ITER = 12
ATT_V_EXT = MLA_V + 16


def _attention(q, k, v):
    B, Tp, _ = q.shape
    nh = ATT_HEADS_PER_STEP
    T = ATT_TILE
    n_q = -(-Tp // T)
    Tp2 = n_q * T
    qk_spec = pl.BlockSpec((1, Tp, nh * HEAD_PAD), lambda b, p: (b, 0, p))
    v_spec = pl.BlockSpec((1, Tp, nh * MLA_V), lambda b, p: (b, 0, p))
    return pl.pallas_call(
        functools.partial(_attn_kernel, nh=nh, n_q=n_q),
        out_shape=jax.ShapeDtypeStruct((B, Tp, MLA_HEADS * MLA_V), BF16),
        grid=(B, MLA_HEADS // nh),
        in_specs=[qk_spec, qk_spec, v_spec],
        out_specs=v_spec,
        scratch_shapes=[pltpu.VMEM((Tp2, nh * HEAD_PAD), BF16),
                        pltpu.VMEM((nh * ATT_V_EXT, Tp2), BF16),
                        pltpu.VMEM((n_q, nh, HEAD_PAD, T), BF16),
                        pltpu.VMEM((T, T), F32),
                        pltpu.VMEM((nh, T, T), F32),
                        pltpu.VMEM((nh, T, T), F32),
                        pltpu.VMEM((nh, 1, T), F32),
                        pltpu.VMEM((nh, 1, T), F32),
                        pltpu.VMEM((n_q, nh, 1, T), F32),
                        pltpu.VMEM((n_q, nh * ATT_V_EXT, T), F32),
                        pltpu.VMEM((Tp2, nh * MLA_V), BF16)],
        compiler_params=_params(("parallel", "parallel")),
        name="mla_attention",
    )(q, k, v)


def _odd_out_value(h_ref, o_ref, gate_ref, w_ref):
    gate = gate_ref[...].astype(F32)
    x = (o_ref[...].astype(F32) * (gate * _sigmoid(gate))).astype(BF16)
    return h_ref[...] + jnp.dot(x, w_ref[...], preferred_element_type=F32)


def _odd_out_even_in_kernel(h_ref, o_ref, gate_ref, wo_ref, g_ref, w_ref, hout_ref, *out_refs):
    h = _odd_out_value(h_ref, o_ref, gate_ref, wo_ref)
    hout_ref[...] = h
    _even_in_body(h, g_ref, w_ref, out_refs)


def _odd_out_even_in(h2, o, gate, wo, g, w):
    M = h2.shape[0]
    tm = _row_tile(M)
    return pl.pallas_call(
        _odd_out_even_in_kernel,
        out_shape=(jax.ShapeDtypeStruct((M, D_MODEL), F32),) + _even_in_outs(M),
        grid=(M // tm,),
        in_specs=[_row_spec(tm, D_MODEL), _row_spec(tm, o.shape[-1]), _row_spec(tm, gate.shape[-1]),
                  _const_spec(wo.shape), _const_spec(g.shape), _const_spec(w.shape)],
        out_specs=(_row_spec(tm, D_MODEL),) + tuple(_row_spec(tm, n) for n in EVEN_OUT_WIDTHS),
        compiler_params=_params(("parallel",)),
        name="odd_out_even_in",
    )(h2, o, gate, wo, g, w)


def _odd_out_last_kernel(h_ref, o_ref, gate_ref, w_ref, fn_ref, out_ref):
    out_ref[...] = _rms(_odd_out_value(h_ref, o_ref, gate_ref, w_ref), fn_ref[...])


def _odd_out_last(h2, o, gate, w, fn, B, Tp, S):
    tm = _pick(S, (512, 256, 128, 64))
    win = lambda n: pl.BlockSpec((pl.Element(tm), pl.Element(n)),
                                 lambda b, i: (pl.multiple_of(b * Tp + N_META + i * tm, N_META), 0))
    return pl.pallas_call(
        _odd_out_last_kernel,
        out_shape=jax.ShapeDtypeStruct((B * S, D_MODEL), F32),
        grid=(B, S // tm),
        in_specs=[win(D_MODEL), win(o.shape[-1]), win(gate.shape[-1]), _const_spec(w.shape),
                  _const_spec(fn.shape)],
        out_specs=pl.BlockSpec((tm, D_MODEL), lambda b, i: (b * (S // tm) + i, 0)),
        compiler_params=_params(("parallel", "parallel")),
        name="odd_out_last",
    )(h2, o, gate, w, fn)


def _split_cols(w, sizes):
    idx = np.cumsum(sizes)[:-1].tolist()
    return jnp.split(w, idx, axis=-1)


def _pack_even(w_in, gate_a_w, gate_x_w, alpha_w):
    xa, ga, q, k, v, ad, gb = _split_cols(
        w_in, (RNN_WIDTH, RNN_WIDTH, GLA_HEADS * GLA_DK, GLA_HEADS * GLA_DK, GLA_HEADS * GLA_DV,
               GLA_GATE_RANK, GLA_HEADS * GLA_DV))
    ad = jnp.pad(ad, ((0, 0), (0, LANES - GLA_GATE_RANK)))
    w = jnp.concatenate([xa, ga, q, k, v, gb, ad], axis=-1).astype(BF16)
    wg = jnp.concatenate([gate_a_w, gate_x_w], axis=-1).astype(BF16)
    aw = jnp.pad(alpha_w, ((0, LANES - GLA_GATE_RANK), (0, 0))).astype(BF16)
    return w, wg, aw


def _pack_odd(w_in, w_q_up, w_kv_up):
    cq, ckv, kr, gate = _split_cols(w_in, (MLA_Q_RANK, MLA_KV_RANK, MLA_ROPE, MLA_HEADS * MLA_V))
    kr = _head_lanes(jnp.zeros((kr.shape[0], MLA_NOPE), kr.dtype), kr)
    w = jnp.concatenate([cq, ckv, kr, gate], axis=-1).astype(BF16)
    wq = w_q_up.reshape(MLA_Q_RANK, MLA_HEADS, MLA_NOPE + MLA_ROPE)
    wq = _head_lanes(wq[..., :MLA_NOPE], wq[..., MLA_NOPE:]).reshape(MLA_Q_RANK, MLA_HEADS * HEAD_PAD)
    wkv = w_kv_up.reshape(MLA_KV_RANK, MLA_HEADS, MLA_NOPE + MLA_V)
    wk = _head_lanes(wkv[..., :MLA_NOPE], jnp.zeros(wkv.shape[:-1] + (MLA_ROPE,), wkv.dtype))
    wk = wk.reshape(MLA_KV_RANK, MLA_HEADS * HEAD_PAD)
    wv = wkv[..., MLA_NOPE:].reshape(MLA_KV_RANK, MLA_HEADS * MLA_V)
    return w, wq.astype(BF16), wk.astype(BF16), wv.astype(BF16)


def kernel(x, positions, meta_tokens, ab_norm, ab_w_in, ab_conv_w, ab_conv_b, ab_gate_a_w, ab_gate_a_b, ab_gate_x_w, ab_gate_x_b, ab_lru_lambda, ab_alpha_w, ab_alpha_b, ab_gla_norm, ab_w_out, c_norm, c_w_in, c_q_norm, c_w_q_up, c_kv_norm, c_w_kv_up, c_w_out, final_norm):
    B, S, D = x.shape
    T = N_META + S
    Tp = -(-T // LANES) * LANES
    M = B * Tp
    depth = ab_norm.shape[0] + c_norm.shape[0]
    row2 = lambda a: a.reshape(1, -1).astype(F32)


    meta_pos = jnp.broadcast_to(jnp.arange(N_META, dtype=positions.dtype)[None], (B, N_META))
    pos = jnp.concatenate([meta_pos, positions + N_META,
                           jnp.zeros((B, Tp - T), positions.dtype)], axis=1)
    inv_freq = ROPE_BASE ** (-jnp.arange(0, MLA_ROPE, 2, dtype=F32) / MLA_ROPE)
    invf = _head_lanes(jnp.zeros((1, MLA_NOPE), F32), jnp.concatenate([inv_freq, inv_freq])[None])
    cos_t, sin_t = _rope_tables(pos.astype(F32).reshape(M, 1), invf)

    if depth % 2 or ab_norm.shape[0] != c_norm.shape[0]:
        raise NotImplementedError("layers must alternate (RG-LRU || GLA), MLA and end on an MLA layer")
    r3 = lambda a: a.reshape(B, Tp, a.shape[-1])
    even_packs = [_pack_even(ab_w_in[j], ab_gate_a_w[j], ab_gate_x_w[j], ab_alpha_w[j])
                  for j in range(depth // 2)]
    h, *even_in = _embed_even_in(x, meta_tokens.astype(x.dtype), row2(ab_norm[0]), even_packs[0][0], Tp)
    for j in range(depth // 2):
        w, wg, aw = even_packs[j]
        xa, ga, q, k, v, gb, ad = even_in
        ya = _rglru(r3(xa), r3(ga), ab_conv_w[j].astype(F32), row2(ab_conv_b[j]), wg,
                    row2(ab_gate_a_b[j]), row2(ab_gate_x_b[j]), row2(ab_lru_lambda[j]))
        ob = _gla(r3(q), r3(k), r3(v), r3(gb), r3(ad), aw, row2(ab_alpha_b[j]), row2(ab_gla_norm[j]))
        w, wq, wk, wv = _pack_odd(c_w_in[j], c_w_q_up[j], c_w_kv_up[j])
        odd_params = (row2(c_norm[j]), w, row2(c_q_norm[j]), wq, row2(c_kv_norm[j]), wk, wv)
        h, q, k, v, gate = _even_out_odd_in(h, ya.reshape(M, -1), ob.reshape(M, -1),
                                            ab_w_out[j].astype(BF16), odd_params, cos_t, sin_t)
        o = _attention(r3(q), r3(k), r3(v)).reshape(M, -1)
        wo = c_w_out[j].astype(BF16)
        if j == depth // 2 - 1:
            return _odd_out_last(h, o, gate, wo, row2(final_norm), B, Tp, S).reshape(B, S, D)
        h, *even_in = _odd_out_even_in(h, o, gate, wo, row2(ab_norm[j + 1]), even_packs[j + 1][0])
```

```python
import functools

import jax
import jax.numpy as jnp
import numpy as np
from jax import lax
from jax.experimental import pallas as pl
from jax.experimental.pallas import tpu as pltpu

F32 = jnp.float32
BF16 = jnp.bfloat16

D_MODEL = 1024
N_META = 16
EPS = 1e-6
RNN_WIDTH = D_MODEL
RNN_BLOCKS = 8
RNN_BLOCK = RNN_WIDTH // RNN_BLOCKS
CONV_WIDTH = 4
RGLRU_C = 8.0
GLA_HEADS = 4
GLA_DK = 128
GLA_DV = 256
GLA_GATE_RANK = 16
GLA_TAU = 16.0
GLA_CHUNK = 64
MLA_HEADS = 16
MLA_NOPE = 64
MLA_ROPE = 32
MLA_V = 64
MLA_Q_RANK = 512
MLA_KV_RANK = 256
ROPE_BASE = 10000.0

LANES = 128
HEAD_PAD = 128
ATT_TILE = 256
SOFTMAX_SLAB = 32
LOG2_E = 1.4426950408889634
NEG_BIG = -1e30
VMEM_LIMIT = 56 * 1024 * 1024


def _pick(n, candidates):
    for c in candidates:
        if n % c == 0:
            return c
    raise ValueError(f"no tile in {candidates} divides {n}")


def _rms(x, g):
    var = jnp.mean(x * x, axis=-1, keepdims=True)
    return x * lax.rsqrt(var + EPS) * g


def _sigmoid(x):
    return 1.0 / (1.0 + jnp.exp2(x * (-LOG2_E)))


def _const_spec(shape):
    nd = len(shape)
    return pl.BlockSpec(shape, lambda *_: (0,) * nd, pipeline_mode=pl.Buffered(1))


def _params(sem):
    return pltpu.CompilerParams(dimension_semantics=sem, vmem_limit_bytes=VMEM_LIMIT)


EVEN_OUT_WIDTHS = (RNN_WIDTH, RNN_WIDTH, GLA_HEADS * GLA_DK, GLA_HEADS * GLA_DK,
                   GLA_HEADS * GLA_DV, GLA_HEADS * GLA_DV, LANES)


def _even_in_body(h, g_ref, w_ref, out_refs):
    xn = _rms(h, g_ref[...]).astype(BF16)
    off = 0
    for ref in out_refs:
        n = ref.shape[-1]
        ref[...] = jnp.dot(xn, w_ref[:, off:off + n], preferred_element_type=F32).astype(ref.dtype)
        off += n


def _embed_even_in_kernel(x_ref, meta_ref, g_ref, w_ref, h_ref, *out_refs, tm, n_tiles, seq, win0):
    k_id = pl.program_id(1)
    for k in range(n_tiles):
        @pl.when(k_id == k)
        def _(k=k):
            t0 = k * tm
            n_meta = max(0, min(N_META - t0, tm))
            x_lo = t0 + n_meta - N_META
            n_x = max(0, min(seq - x_lo, tm - n_meta))
            if n_meta:
                h_ref[0:n_meta, :] = meta_ref[t0:t0 + n_meta, :]
            if n_x:
                off = x_lo - win0[k]
                h_ref[n_meta:n_meta + n_x, :] = x_ref[off:off + n_x, :]
            if n_meta + n_x < tm:
                h_ref[n_meta + n_x:tm, :] = jnp.zeros((tm - n_meta - n_x, h_ref.shape[1]), h_ref.dtype)
    slab = _slab_rows(tm)
    for r in range(0, tm, slab):
        rows = pl.ds(r, slab)
        _even_in_body(h_ref[rows, :], g_ref, w_ref, [ref.at[rows] for ref in out_refs])


def _embed_even_in(x, meta, g, w, Tp):
    B, S, D = x.shape
    tm = _pick(Tp, tuple(c for c in (544, 512, 384, 256, 128) if c <= S))
    n_tiles = Tp // tm
    M = B * Tp
    win0 = tuple(min(max(k * tm - N_META, 0), S - tm) for k in range(n_tiles))

    def x_index(b, k):
        start = jnp.clip(k * tm - N_META, 0, S - tm)
        return (pl.multiple_of(b * S + start, N_META), 0)

    out_spec = lambda n: pl.BlockSpec((tm, n), lambda b, k: (b * n_tiles + k, 0))
    return pl.pallas_call(
        functools.partial(_embed_even_in_kernel, tm=tm, n_tiles=n_tiles, seq=S, win0=win0),
        out_shape=(jax.ShapeDtypeStruct((M, D), F32),) + _even_in_outs(M),
        grid=(B, n_tiles),
        in_specs=[pl.BlockSpec((pl.Element(tm), pl.Element(D)), x_index),
                  _const_spec(meta.shape), _const_spec(g.shape), _const_spec(w.shape)],
        out_specs=(out_spec(D),) + tuple(out_spec(n) for n in EVEN_OUT_WIDTHS),
        compiler_params=_params(("parallel", "parallel")),
        name="embed_even_in",
    )(x.reshape(B * S, D), meta, g, w)


def _slab_rows(tm):
    return next(tm // n for n in (2, 1) if (tm // n) % 16 == 0 and tm % n == 0)


def _row_tile(M):
    return _pick(M, (512, 256, 128))


def _row_spec(tm, n):
    return pl.BlockSpec((tm, n), lambda i: (i, 0))


def _even_in_outs(M):
    return tuple(jax.ShapeDtypeStruct((M, n), BF16) for n in EVEN_OUT_WIDTHS)


def _lru_pitch(tt):
    return tt if (tt // 8) % 2 else tt + 8


def _rglru_kernel(xa_ref, ga_ref, cw_ref, cb_ref, wg_ref, ba_ref, bx_ref, lam_ref, y_ref,
                  flat, u_s, a_s, hist, h_s, *, tt, pitch, nblk):
    B = xa_ref.shape[0]
    t_idx = pl.program_id(1)

    @pl.when(t_idx == 0)
    def _():
        hist[...] = jnp.zeros_like(hist)
        h_s[...] = jnp.zeros_like(h_s)

    for j in range(nblk):
        lanes = slice(j * LANES, (j + 1) * LANES)
        for b in range(B):
            flat[j, b * pitch:b * pitch + tt, :] = xa_ref[b, :, lanes].astype(F32)

    taps = [[cw_ref[k:k + 1, j * LANES:(j + 1) * LANES] for k in range(CONV_WIDTH)] for j in range(nblk)]
    bias = [cb_ref[:, j * LANES:(j + 1) * LANES] for j in range(nblk)]

    def conv_step(t, carry):
        out = []
        for j in range(nblk):
            x1, x2, x3 = carry[j]
            x0 = flat[j, pl.ds(t, B, stride=pitch), :]
            w = taps[j]
            u_s[j, pl.ds(pl.multiple_of(t * B, B), B), :] = (
                bias[j] + w[3] * x0 + w[2] * x1 + w[1] * x2 + w[0] * x3)
            out.append((x0, x1, x2))
        return tuple(out)

    last = lax.fori_loop(0, tt, conv_step,
                         tuple((hist[j, 0], hist[j, 1], hist[j, 2]) for j in range(nblk)), unroll=8)
    for j in range(nblk):
        for k in range(CONV_WIDTH - 1):
            hist[j, k] = last[j][k]

    for j in range(nblk):
        lanes = slice(j * LANES, (j + 1) * LANES)
        y = u_s[j]
        g = jnp.dot(y.astype(BF16), wg_ref[j], preferred_element_type=F32)
        r = 0.5 + 0.5 * jnp.tanh(0.5 * (g[:, :LANES] + ba_ref[:, lanes]))
        i = _sigmoid(g[:, LANES:] + bx_ref[:, lanes])
        lam = lam_ref[:, lanes]
        softplus_neg_lam = jnp.maximum(-lam, 0.0) + jnp.log(1.0 + jnp.exp(-jnp.abs(lam)))
        a = jnp.exp2(r * ((-RGLRU_C * LOG2_E) * softplus_neg_lam))
        x = 1.0 - a * a
        a_s[j] = a
        u_s[j] = (x * lax.rsqrt(jnp.maximum(x, 1e-30))) * (i * y)

    def scan_step(t, hs):
        out = []
        for j in range(nblk):
            rows = pl.ds(pl.multiple_of(t * B, B), B)
            h = a_s[j, rows, :] * hs[j] + u_s[j, rows, :]
            flat[j, pl.ds(t, B, stride=pitch), :] = h
            out.append(h)
        return tuple(out)

    hs = lax.fori_loop(0, tt, scan_step, tuple(h_s[j] for j in range(nblk)), unroll=8)
    for j in range(nblk):
        h_s[j] = hs[j]

    for j in range(nblk):
        lanes = slice(j * LANES, (j + 1) * LANES)
        for b in range(B):
            ga = ga_ref[b, :, lanes].astype(F32)
            y_ref[b, :, lanes] = (flat[j, b * pitch:b * pitch + tt, :]
                                  * (ga * _sigmoid(ga))).astype(y_ref.dtype)


def _rglru(xa, ga, cw, cb, wg, ba, bx, lam):
    B, Tp, C = xa.shape
    tt = _pick(Tp, (272, 256, 192, 128))
    pitch = _lru_pitch(tt)
    cblk = 512
    nblk = cblk // LANES
    blk = pl.BlockSpec((B, tt, cblk), lambda c, t: (0, t, c))
    vec = lambda rows: pl.BlockSpec((rows, cblk), lambda c, t: (0, c))
    return pl.pallas_call(
        functools.partial(_rglru_kernel, tt=tt, pitch=pitch, nblk=nblk),
        out_shape=jax.ShapeDtypeStruct((B, Tp, C), BF16),
        grid=(C // cblk, Tp // tt),
        in_specs=[blk, blk, vec(CONV_WIDTH), vec(1),
                  pl.BlockSpec((nblk, RNN_BLOCK, 2 * RNN_BLOCK), lambda c, t: (c, 0, 0)),
                  vec(1), vec(1), vec(1)],
        out_specs=blk,
        scratch_shapes=[pltpu.VMEM((nblk, B * pitch, LANES), F32),
                        pltpu.VMEM((nblk, tt * B, LANES), F32),
                        pltpu.VMEM((nblk, tt * B, LANES), F32),
                        pltpu.VMEM((nblk, CONV_WIDTH - 1, B, LANES), F32),
                        pltpu.VMEM((nblk, B, LANES), F32)],
        compiler_params=_params(("parallel", "arbitrary")),
        name="rglru",
    )(xa, ga, cw, cb, wg, ba, bx, lam)


def _gla_kernel(q_ref, k_ref, v_ref, gb_ref, ad_ref, aw_ref, ab_ref, gn_ref, o_ref, st_ref, *, nchunk, nb):
    C = GLA_CHUNK

    @pl.when(pl.program_id(1) == 0)
    def _():
        st_ref[...] = jnp.zeros_like(st_ref)

    row = lax.broadcasted_iota(jnp.int32, (C, C), 0)
    col = lax.broadcasted_iota(jnp.int32, (C, C), 1)
    causal = row >= col
    tril = causal.astype(BF16)
    scale = GLA_DK ** -0.5

    def chunk(c, carry):
        rows = pl.ds(pl.multiple_of(c * C, C), C)
        bs = range(nb)
        hs = range(GLA_HEADS)
        kl = [slice(h * GLA_DK, (h + 1) * GLA_DK) for h in hs]
        vl = [slice(h * GLA_DV, (h + 1) * GLA_DV) for h in hs]
        pre = [jnp.dot(ad_ref[b, rows, :], aw_ref[...], preferred_element_type=F32) + ab_ref[...]
               for b in bs]
        la = [(jnp.minimum(p, 0.0) - jnp.log(1.0 + jnp.exp2(jnp.abs(p) * (-LOG2_E)))) * (LOG2_E / GLA_TAU)
              for p in pre]
        la_hi = [x.astype(BF16) for x in la]
        la_lo = [(x - hi.astype(F32)).astype(BF16) for x, hi in zip(la, la_hi)]
        bc = [jnp.dot(tril, hi, preferred_element_type=F32) + jnp.dot(tril, lo, preferred_element_type=F32)
              for hi, lo in zip(la_hi, la_lo)]
        b_last = [x[C - 1:C, :] for x in bc]
        q = [q_ref[b, rows, :].astype(F32) for b in bs]
        k = [k_ref[b, rows, :].astype(F32) for b in bs]
        q_dec = [(q[b] * (jnp.exp2(bc[b]) * scale)).astype(BF16) for b in bs]
        k_inv = [(k[b] * jnp.exp2(-bc[b])).astype(BF16) for b in bs]
        k_end = [(k[b] * jnp.exp2(b_last[b] - bc[b])).astype(BF16) for b in bs]
        decay = [jnp.exp2(x) for x in b_last]
        s = [[jnp.where(causal,
                        lax.dot_general(q_dec[b][:, kl[h]], k_inv[b][:, kl[h]], (((1,), (1,)), ((), ())),
                                        preferred_element_type=F32), 0.0).astype(BF16)
              for h in hs] for b in bs]
        o = [[None] * GLA_HEADS for _ in bs]
        for b in bs:
            for h in hs:
                vh = v_ref[b, rows, vl[h]]
                st = st_ref[b, h]
                o[b][h] = (jnp.dot(s[b][h], vh, preferred_element_type=F32)
                           + lax.dot_general(q_dec[b][:, kl[h]], st.astype(BF16), (((1,), (1,)), ((), ())),
                                             preferred_element_type=F32))
                vk = lax.dot_general(vh, k_end[b][:, kl[h]], (((0,), (0,)), ((), ())),
                                     preferred_element_type=F32)
                st_ref[b, h] = st * decay[b][:, kl[h]] + vk
        for b in bs:
            for h in hs:
                gb = gb_ref[b, rows, vl[h]].astype(F32)
                o_ref[b, rows, vl[h]] = (_rms(o[b][h], gn_ref[...])
                                         * (gb * _sigmoid(gb))).astype(o_ref.dtype)
        return carry

    lax.fori_loop(0, nchunk, chunk, 0)


GLA_BATCH_PER_STEP = 8
GLA_TIME_TILE = 128


def _gla(q, k, v, gb, ad, aw, ab, gn):
    B, Tp, _ = q.shape
    nb = GLA_BATCH_PER_STEP if B % GLA_BATCH_PER_STEP == 0 else 1
    tc = GLA_TIME_TILE
    kw, vw = GLA_HEADS * GLA_DK, GLA_HEADS * GLA_DV
    spec = lambda w: pl.BlockSpec((nb, tc, w), lambda b, t: (b, t, 0))
    return pl.pallas_call(
        functools.partial(_gla_kernel, nchunk=tc // GLA_CHUNK, nb=nb),
        out_shape=jax.ShapeDtypeStruct((B, Tp, vw), BF16),
        grid=(B // nb, Tp // tc),
        in_specs=[spec(kw), spec(kw), spec(vw), spec(vw), spec(LANES),
                  _const_spec(aw.shape), _const_spec(ab.shape), _const_spec(gn.shape)],
        out_specs=spec(vw),
        scratch_shapes=[pltpu.VMEM((nb, GLA_HEADS, GLA_DV, GLA_DK), F32)],
        compiler_params=_params(("parallel", "arbitrary")),
        name="gla",
    )(q, k, v, gb, ad, aw, ab, gn)


def _even_out_value(h_ref, ya_ref, ob_ref, w_ref):
    n = ya_ref.shape[-1]
    return (h_ref[...]
            + jnp.dot(ya_ref[...], w_ref[:n, :], preferred_element_type=F32)
            + jnp.dot(ob_ref[...], w_ref[n:, :], preferred_element_type=F32))


def _even_out_odd_in_kernel(h_ref, ya_ref, ob_ref, wo_ref, *refs):
    hout_ref = refs[-5]
    h = _even_out_value(h_ref, ya_ref, ob_ref, wo_ref)
    hout_ref[...] = h
    _odd_in_body(h, *refs[:-5], *refs[-4:])


def _even_out_odd_in(h2, ya, ob, wo, params, cos_t, sin_t):
    M = h2.shape[0]
    tm = _row_tile(M)
    return pl.pallas_call(
        _even_out_odd_in_kernel,
        out_shape=(jax.ShapeDtypeStruct((M, D_MODEL), F32),) + _odd_in_outs(M),
        grid=(M // tm,),
        in_specs=[_row_spec(tm, D_MODEL), _row_spec(tm, ya.shape[-1]), _row_spec(tm, ob.shape[-1]),
                  _const_spec(wo.shape)] + _odd_in_specs(tm, params),
        out_specs=(_row_spec(tm, D_MODEL),) + tuple(_row_spec(tm, n) for n in ODD_IN_WIDTHS),
        compiler_params=_params(("parallel",)),
        name="even_out_odd_in",
    )(h2, ya, ob, wo, *params, cos_t, sin_t)


ROPE_HALF = MLA_ROPE // 2
ROPE_X2_LANE = HEAD_PAD // 2


def _head_lanes(nope, rope_part):
    zeros = lambda n: jnp.zeros(nope.shape[:-1] + (n,), nope.dtype)
    n1 = ROPE_X2_LANE - ROPE_HALF
    return jnp.concatenate([rope_part[..., :ROPE_HALF], nope[..., :n1], rope_part[..., ROPE_HALF:],
                            nope[..., n1:], zeros(HEAD_PAD - MLA_NOPE - MLA_ROPE)], axis=-1)


def _rope_table_kernel(pos_ref, invf_ref, c_ref, s_ref):
    ang = pos_ref[...] * invf_ref[...]
    lane = lax.broadcasted_iota(jnp.int32, ang.shape, 1)
    c_ref[...] = jnp.cos(ang)
    sn = jnp.sin(ang)
    s_ref[...] = jnp.where(lane < ROPE_X2_LANE, -sn, sn)


def _rope_tables(posf, invf):
    M = posf.shape[0]
    tm = _pick(M, (512, 256, 128))
    out = jax.ShapeDtypeStruct((M, HEAD_PAD), F32)
    return pl.pallas_call(
        _rope_table_kernel,
        out_shape=(out, out),
        grid=(M // tm,),
        in_specs=[pl.BlockSpec((tm, 1), lambda i: (i, 0)), _const_spec((1, HEAD_PAD))],
        out_specs=(pl.BlockSpec((tm, HEAD_PAD), lambda i: (i, 0)),) * 2,
        compiler_params=_params(("parallel",)),
        name="rope_tables",
    )(posf, invf)


def _rope(x, cos_t, sin_t):
    heads = [pltpu.roll(x[:, c:c + HEAD_PAD], ROPE_X2_LANE, axis=1) for c in range(0, x.shape[1], HEAD_PAD)]
    partner = heads[0] if len(heads) == 1 else jnp.concatenate(heads, axis=1)
    return x * cos_t + partner * sin_t


def _odd_in_body(h, g_ref, w_ref, qn_ref, wq_ref, kvn_ref, wk_ref, wv_ref, cos_ref, sin_ref,
                 q_ref, k_ref, v_ref, gate_ref):
    xn = _rms(h, g_ref[...]).astype(BF16)
    c0, c1, c2 = MLA_Q_RANK, MLA_Q_RANK + MLA_KV_RANK, MLA_Q_RANK + MLA_KV_RANK + HEAD_PAD
    gate_ref[...] = jnp.dot(xn, w_ref[:, c2:], preferred_element_type=F32).astype(gate_ref.dtype)
    cq = jnp.dot(xn, w_ref[:, :c0], preferred_element_type=F32)
    ckv = jnp.dot(xn, w_ref[:, c0:c1], preferred_element_type=F32)
    kr = jnp.dot(xn, w_ref[:, c1:c2], preferred_element_type=F32)
    cos_t, sin_t = cos_ref[...], sin_ref[...]
    kr = _rope(kr, cos_t, sin_t)
    cqn = _rms(cq, qn_ref[...]).astype(BF16)
    ckvn = _rms(ckv, kvn_ref[...]).astype(BF16)
    v_ref[...] = jnp.dot(ckvn, wv_ref[...], preferred_element_type=F32).astype(v_ref.dtype)
    scale = LOG2_E * (MLA_NOPE + MLA_ROPE) ** -0.5
    cos2 = jnp.concatenate([cos_t, cos_t], axis=1) * scale
    sin2 = jnp.concatenate([sin_t, sin_t], axis=1) * scale
    kr2 = jnp.concatenate([kr, kr], axis=1)
    for hp in range(MLA_HEADS // 2):
        lanes = slice(hp * 2 * HEAD_PAD, (hp + 1) * 2 * HEAD_PAD)
        qh = jnp.dot(cqn, wq_ref[:, lanes], preferred_element_type=F32)
        q_ref[:, lanes] = _rope(qh, cos2, sin2).astype(q_ref.dtype)
        kh = jnp.dot(ckvn, wk_ref[:, lanes], preferred_element_type=F32)
        k_ref[:, lanes] = (kh + kr2).astype(k_ref.dtype)


ODD_IN_WIDTHS = (MLA_HEADS * HEAD_PAD, MLA_HEADS * HEAD_PAD, MLA_HEADS * MLA_V, MLA_HEADS * MLA_V)


def _odd_in_outs(M):
    return tuple(jax.ShapeDtypeStruct((M, n), BF16) for n in ODD_IN_WIDTHS)


def _odd_in_specs(tm, params):
    return [_const_spec(p.shape) for p in params] + [_row_spec(tm, HEAD_PAD), _row_spec(tm, HEAD_PAD)]


def _attn_kernel(q_ref, k_ref, v_ref, o_ref, k_scr, vt_scr, qt_scr, bias_scr, s_a, s_b, mx_a, mx_b,
                 m_scr, acc_scr, o_scr, *, nh, n_q):
    T = ATT_TILE
    Tp = q_ref.shape[1]
    Tp2 = n_q * T
    SL = SOFTMAX_SLAB
    VE = ATT_V_EXT

    k_scr[0:Tp, :] = k_ref[0]
    if Tp2 > Tp:
        k_scr[Tp:Tp2, :] = jnp.zeros((Tp2 - Tp, k_scr.shape[1]), k_scr.dtype)
    ones_row = (lax.broadcasted_iota(jnp.int32, (VE - MLA_V, Tp2), 0) == 0).astype(vt_scr.dtype)
    for h in range(nh):
        vt_scr[h * VE + MLA_V:(h + 1) * VE, :] = ones_row
        if Tp2 > Tp:
            vt_scr[h * VE:h * VE + MLA_V, Tp:Tp2] = jnp.zeros((MLA_V, Tp2 - Tp), vt_scr.dtype)
    for c in range(0, Tp, LANES):
        vt = v_ref[0, c:c + LANES, :].T
        for h in range(nh):
            vt_scr[h * VE:h * VE + MLA_V, c:c + LANES] = vt[h * MLA_V:(h + 1) * MLA_V, :]
    for j in range(n_q):
        rows = min(T, Tp - j * T)
        for h in range(nh):
            qh = q_ref[0, j * T:j * T + rows, h * HEAD_PAD:(h + 1) * HEAD_PAD]
            qt_scr[j, h, :, 0:rows] = qh.T
            if rows < T:
                qt_scr[j, h, :, rows:T] = jnp.zeros((HEAD_PAD, T - rows), qt_scr.dtype)
    kk = lax.broadcasted_iota(jnp.int32, (T, T), 0)
    qq = lax.broadcasted_iota(jnp.int32, (T, T), 1)
    bias_scr[...] = jnp.where(kk <= qq, 0.0, NEG_BIG)

    def qk(buf, j, i, diag=False, heads=range(nh)):
        s_ref, mx_ref = buf
        koff = pl.multiple_of(i * T, T)
        for h in heads:
            sv = jnp.dot(k_scr[pl.ds(koff, T), h * HEAD_PAD:(h + 1) * HEAD_PAD],
                         qt_scr[j, h], preferred_element_type=F32)
            if diag:
                sv = sv + bias_scr[...]
            s_ref[h] = sv
            parts = [None] * 4
            for n, r in enumerate(range(0, T, SL)):
                blk = sv[r:r + SL, :]
                parts[n % 4] = blk if parts[n % 4] is None else jnp.maximum(parts[n % 4], blk)
            mt = jnp.maximum(jnp.maximum(parts[0], parts[1]), jnp.maximum(parts[2], parts[3]))
            mx_ref[h] = jnp.max(mt, axis=0, keepdims=True)

    def softmax_pv(buf, j, i, diag, heads=range(nh)):
        s_ref, mx_ref = buf
        koff = pl.multiple_of(i * T, T)
        for h in heads:
            if diag:
                m_new = mx_ref[h]
            else:
                m_old = m_scr[j, h]
                m_new = jnp.maximum(m_old, mx_ref[h])
                alpha = jnp.exp2(m_old - m_new)
            m_scr[j, h] = m_new
            p = jnp.concatenate([jnp.exp2((s_ref[h, r:r + SL, :] - m_new).astype(BF16))
                                 for r in range(0, T, SL)], axis=0)
            rows = slice(h * VE, (h + 1) * VE)
            pv = jnp.dot(vt_scr[rows, pl.ds(koff, T)], p, preferred_element_type=F32)
            acc_scr[j, rows, :] = pv if diag else alpha * acc_scr[j, rows, :] + pv

    bufs = ((s_a, mx_a), (s_b, mx_b))
    G = ATT_STEPS_PER_ITER

    def run(n_steps, first, advance, diag):
        def steps(count, jt):
            for t in range(count):
                nxt = advance(*jt)
                for h in range(nh):
                    qk(bufs[(t + 1) % 2], jnp.minimum(nxt[0], n_q - 1), nxt[1], diag, (h,))
                    softmax_pv(bufs[t % 2], jt[0], jt[1], diag, (h,))
                jt = nxt
            return jt

        qk(bufs[0], first[0], first[1], diag)
        jt = lax.fori_loop(0, n_steps // G, lambda n, c: steps(G, c),
                           (jnp.int32(first[0]), jnp.int32(first[1])))
        steps(n_steps % G, jt)

    run(n_q, (0, 0), lambda j, i: (j + 1, jnp.minimum(i + 1, n_q - 1)), True)

    def advance(j, i):
        last = i + 1 >= j
        return jnp.where(last, j + 1, j), jnp.where(last, 0, i + 1)

    if n_q > 1:
        run(n_q * (n_q - 1) // 2, (1, 0), advance, False)

    def finish(j, c):
        outs = []
        for h in range(nh):
            acc = acc_scr[j, h * VE:h * VE + MLA_V, :]
            den = acc_scr[j, h * VE + MLA_V:h * VE + MLA_V + 1, :]
            outs.append(acc / den)
        o_scr[pl.ds(pl.multiple_of(j * T, T), T), :] = jnp.concatenate(outs, axis=0).T.astype(o_scr.dtype)
        return c

    lax.fori_loop(0, n_q, finish, 0, unroll=3)
    o_ref[0] = o_scr[0:Tp, :]


ATT_HEADS_PER_STEP = 4
ATT_STEPS_PER_ITER = 12
ATT_V_EXT = MLA_V + 16


def _attention(q, k, v):
    B, Tp, _ = q.shape
    nh = ATT_HEADS_PER_STEP
    T = ATT_TILE
    n_q = -(-Tp // T)
    Tp2 = n_q * T
    qk_spec = pl.BlockSpec((1, Tp, nh * HEAD_PAD), lambda b, p: (b, 0, p))
    v_spec = pl.BlockSpec((1, Tp, nh * MLA_V), lambda b, p: (b, 0, p))
    return pl.pallas_call(
        functools.partial(_attn_kernel, nh=nh, n_q=n_q),
        out_shape=jax.ShapeDtypeStruct((B, Tp, MLA_HEADS * MLA_V), BF16),
        grid=(B, MLA_HEADS // nh),
        in_specs=[qk_spec, qk_spec, v_spec],
        out_specs=v_spec,
        scratch_shapes=[pltpu.VMEM((Tp2, nh * HEAD_PAD), BF16),
                        pltpu.VMEM((nh * ATT_V_EXT, Tp2), BF16),
                        pltpu.VMEM((n_q, nh, HEAD_PAD, T), BF16),
                        pltpu.VMEM((T, T), F32),
                        pltpu.VMEM((nh, T, T), F32),
                        pltpu.VMEM((nh, T, T), F32),
                        pltpu.VMEM((nh, 1, T), F32),
                        pltpu.VMEM((nh, 1, T), F32),
                        pltpu.VMEM((n_q, nh, 1, T), F32),
                        pltpu.VMEM((n_q, nh * ATT_V_EXT, T), F32),
                        pltpu.VMEM((Tp2, nh * MLA_V), BF16)],
        compiler_params=_params(("parallel", "parallel")),
        name="mla_attention",
    )(q, k, v)


def _odd_out_value(h_ref, o_ref, gate_ref, w_ref):
    gate = gate_ref[...].astype(F32)
    x = (o_ref[...].astype(F32) * (gate * _sigmoid(gate))).astype(BF16)
    return h_ref[...] + jnp.dot(x, w_ref[...], preferred_element_type=F32)


def _odd_out_even_in_kernel(h_ref, o_ref, gate_ref, wo_ref, g_ref, w_ref, hout_ref, *out_refs):
    tm = h_ref.shape[0]
    slab = _slab_rows(tm)
    for r in range(0, tm, slab):
        rows = pl.ds(r, slab)
        h = _odd_out_value(h_ref.at[rows], o_ref.at[rows], gate_ref.at[rows], wo_ref)
        hout_ref[rows, :] = h
        _even_in_body(h, g_ref, w_ref, [ref.at[rows] for ref in out_refs])


def _odd_out_even_in(h2, o, gate, wo, g, w):
    M = h2.shape[0]
    tm = _row_tile(M)
    return pl.pallas_call(
        _odd_out_even_in_kernel,
        out_shape=(jax.ShapeDtypeStruct((M, D_MODEL), F32),) + _even_in_outs(M),
        grid=(M // tm,),
        in_specs=[_row_spec(tm, D_MODEL), _row_spec(tm, o.shape[-1]), _row_spec(tm, gate.shape[-1]),
                  _const_spec(wo.shape), _const_spec(g.shape), _const_spec(w.shape)],
        out_specs=(_row_spec(tm, D_MODEL),) + tuple(_row_spec(tm, n) for n in EVEN_OUT_WIDTHS),
        compiler_params=_params(("parallel",)),
        name="odd_out_even_in",
    )(h2, o, gate, wo, g, w)


def _odd_out_last_kernel(h_ref, o_ref, gate_ref, w_ref, fn_ref, out_ref):
    out_ref[...] = _rms(_odd_out_value(h_ref, o_ref, gate_ref, w_ref), fn_ref[...])


def _odd_out_last(h2, o, gate, w, fn, B, Tp, S):
    tm = _pick(S, (512, 256, 128, 64))
    win = lambda n: pl.BlockSpec((pl.Element(tm), pl.Element(n)),
                                 lambda b, i: (pl.multiple_of(b * Tp + N_META + i * tm, N_META), 0))
    return pl.pallas_call(
        _odd_out_last_kernel,
        out_shape=jax.ShapeDtypeStruct((B * S, D_MODEL), F32),
        grid=(B, S // tm),
        in_specs=[win(D_MODEL), win(o.shape[-1]), win(gate.shape[-1]), _const_spec(w.shape),
                  _const_spec(fn.shape)],
        out_specs=pl.BlockSpec((tm, D_MODEL), lambda b, i: (b * (S // tm) + i, 0)),
        compiler_params=_params(("parallel", "parallel")),
        name="odd_out_last",
    )(h2, o, gate, w, fn)


def _split_cols(w, sizes):
    idx = np.cumsum(sizes)[:-1].tolist()
    return jnp.split(w, idx, axis=-1)


def _pack_even(w_in, gate_a_w, gate_x_w, alpha_w):
    xa, ga, q, k, v, ad, gb = _split_cols(
        w_in, (RNN_WIDTH, RNN_WIDTH, GLA_HEADS * GLA_DK, GLA_HEADS * GLA_DK, GLA_HEADS * GLA_DV,
               GLA_GATE_RANK, GLA_HEADS * GLA_DV))
    ad = jnp.pad(ad, ((0, 0), (0, LANES - GLA_GATE_RANK)))
    w = jnp.concatenate([xa, ga, q, k, v, gb, ad], axis=-1).astype(BF16)
    wg = jnp.concatenate([gate_a_w, gate_x_w], axis=-1).astype(BF16)
    aw = jnp.pad(alpha_w, ((0, LANES - GLA_GATE_RANK), (0, 0))).astype(BF16)
    return w, wg, aw


def _pack_odd(w_in, w_q_up, w_kv_up):
    cq, ckv, kr, gate = _split_cols(w_in, (MLA_Q_RANK, MLA_KV_RANK, MLA_ROPE, MLA_HEADS * MLA_V))
    kr = _head_lanes(jnp.zeros((kr.shape[0], MLA_NOPE), kr.dtype), kr)
    w = jnp.concatenate([cq, ckv, kr, gate], axis=-1).astype(BF16)
    wq = w_q_up.reshape(MLA_Q_RANK, MLA_HEADS, MLA_NOPE + MLA_ROPE)
    wq = _head_lanes(wq[..., :MLA_NOPE], wq[..., MLA_NOPE:]).reshape(MLA_Q_RANK, MLA_HEADS * HEAD_PAD)
    wkv = w_kv_up.reshape(MLA_KV_RANK, MLA_HEADS, MLA_NOPE + MLA_V)
    wk = _head_lanes(wkv[..., :MLA_NOPE], jnp.zeros(wkv.shape[:-1] + (MLA_ROPE,), wkv.dtype))
    wk = wk.reshape(MLA_KV_RANK, MLA_HEADS * HEAD_PAD)
    wv = wkv[..., MLA_NOPE:].reshape(MLA_KV_RANK, MLA_HEADS * MLA_V)
    return w, wq.astype(BF16), wk.astype(BF16), wv.astype(BF16)


def kernel(x, positions, meta_tokens, ab_norm, ab_w_in, ab_conv_w, ab_conv_b, ab_gate_a_w, ab_gate_a_b, ab_gate_x_w, ab_gate_x_b, ab_lru_lambda, ab_alpha_w, ab_alpha_b, ab_gla_norm, ab_w_out, c_norm, c_w_in, c_q_norm, c_w_q_up, c_kv_norm, c_w_kv_up, c_w_out, final_norm):
    B, S, D = x.shape
    T = N_META + S
    Tp = -(-T // LANES) * LANES
    M = B * Tp
    depth = ab_norm.shape[0] + c_norm.shape[0]
    row2 = lambda a: a.reshape(1, -1).astype(F32)


    meta_pos = jnp.broadcast_to(jnp.arange(N_META, dtype=positions.dtype)[None], (B, N_META))
    pos = jnp.concatenate([meta_pos, positions + N_META,
                           jnp.zeros((B, Tp - T), positions.dtype)], axis=1)
    inv_freq = ROPE_BASE ** (-jnp.arange(0, MLA_ROPE, 2, dtype=F32) / MLA_ROPE)
    invf = _head_lanes(jnp.zeros((1, MLA_NOPE), F32), jnp.concatenate([inv_freq, inv_freq])[None])
    cos_t, sin_t = _rope_tables(pos.astype(F32).reshape(M, 1), invf)

    if depth % 2 or ab_norm.shape[0] != c_norm.shape[0]:
        raise NotImplementedError("layers must alternate (RG-LRU || GLA), MLA and end on an MLA layer")
    r3 = lambda a: a.reshape(B, Tp, a.shape[-1])
    even_packs = [_pack_even(ab_w_in[j], ab_gate_a_w[j], ab_gate_x_w[j], ab_alpha_w[j])
                  for j in range(depth // 2)]
    h, *even_in = _embed_even_in(x, meta_tokens.astype(x.dtype), row2(ab_norm[0]), even_packs[0][0], Tp)
    for j in range(depth // 2):
        w, wg, aw = even_packs[j]
        xa, ga, q, k, v, gb, ad = even_in
        ya = _rglru(r3(xa), r3(ga), ab_conv_w[j].astype(F32), row2(ab_conv_b[j]), wg,
                    row2(ab_gate_a_b[j]), row2(ab_gate_x_b[j]), row2(ab_lru_lambda[j]))
        ob = _gla(r3(q), r3(k), r3(v), r3(gb), r3(ad), aw, row2(ab_alpha_b[j]), row2(ab_gla_norm[j]))
        w, wq, wk, wv = _pack_odd(c_w_in[j], c_w_q_up[j], c_w_kv_up[j])
        odd_params = (row2(c_norm[j]), w, row2(c_q_norm[j]), wq, row2(c_kv_norm[j]), wk, wv)
        h, q, k, v, gate = _even_out_odd_in(h, ya.reshape(M, -1), ob.reshape(M, -1),
                                            ab_w_out[j].astype(BF16), odd_params, cos_t, sin_t)
        o = _attention(r3(q), r3(k), r3(v)).reshape(M, -1)
        wo = c_w_out[j].astype(BF16)
        if j == depth // 2 - 1:
            return _odd_out_last(h, o, gate, wo, row2(final_norm), B, Tp, S).reshape(B, S, D)
        h, *even_in = _odd_out_even_in(h, o, gate, wo, row2(ab_norm[j + 1]), even_packs[j + 1][0])
```

```python
import functools

import jax
import jax.numpy as jnp
import numpy as np
from jax import lax
from jax.experimental import pallas as pl
from jax.experimental.pallas import tpu as pltpu

F32 = jnp.float32
BF16 = jnp.bfloat16

D_MODEL = 1024
N_META = 16
EPS = 1e-6
RNN_WIDTH = D_MODEL
RNN_BLOCKS = 8
RNN_BLOCK = RNN_WIDTH // RNN_BLOCKS
CONV_WIDTH = 4
RGLRU_C = 8.0
GLA_HEADS = 4
GLA_DK = 128
GLA_DV = 256
GLA_GATE_RANK = 16
GLA_TAU = 16.0
GLA_CHUNK = 64
MLA_HEADS = 16
MLA_NOPE = 64
MLA_ROPE = 32
MLA_V = 64
MLA_Q_RANK = 512
MLA_KV_RANK = 256
ROPE_BASE = 10000.0

LANES = 128
HEAD_PAD = 128
ATT_TILE = 256
SOFTMAX_SLAB = 32
LOG2_E = 1.4426950408889634
NEG_BIG = -1e30
VMEM_LIMIT = 56 * 1024 * 1024


def _pick(n, candidates):
    for c in candidates:
        if n % c == 0:
            return c
    raise ValueError(f"no tile in {candidates} divides {n}")


def _rms(x, g):
    var = jnp.mean(x * x, axis=-1, keepdims=True)
    return x * lax.rsqrt(var + EPS) * g


def _sigmoid(x):
    return 1.0 / (1.0 + jnp.exp2(x * (-LOG2_E)))


def _const_spec(shape):
    nd = len(shape)
    return pl.BlockSpec(shape, lambda *_: (0,) * nd, pipeline_mode=pl.Buffered(1))


def _params(sem):
    return pltpu.CompilerParams(dimension_semantics=sem, vmem_limit_bytes=VMEM_LIMIT)


EVEN_OUT_WIDTHS = (RNN_WIDTH, RNN_WIDTH, GLA_HEADS * GLA_DK, GLA_HEADS * GLA_DK,
                   GLA_HEADS * GLA_DV, GLA_HEADS * GLA_DV, LANES)


def _even_in_body(h, g_ref, w_ref, out_refs):
    xn = _rms(h, g_ref[...]).astype(BF16)
    off = 0
    for ref in out_refs:
        n = ref.shape[-1]
        ref[...] = jnp.dot(xn, w_ref[:, off:off + n], preferred_element_type=F32).astype(ref.dtype)
        off += n


def _embed_even_in_kernel(x_ref, meta_ref, g_ref, w_ref, h_ref, *out_refs, tm, n_tiles, seq, win0):
    k_id = pl.program_id(1)
    for k in range(n_tiles):
        @pl.when(k_id == k)
        def _(k=k):
            t0 = k * tm
            n_meta = max(0, min(N_META - t0, tm))
            x_lo = t0 + n_meta - N_META
            n_x = max(0, min(seq - x_lo, tm - n_meta))
            if n_meta:
                h_ref[0:n_meta, :] = meta_ref[t0:t0 + n_meta, :]
            if n_x:
                off = x_lo - win0[k]
                h_ref[n_meta:n_meta + n_x, :] = x_ref[off:off + n_x, :]
            if n_meta + n_x < tm:
                h_ref[n_meta + n_x:tm, :] = jnp.zeros((tm - n_meta - n_x, h_ref.shape[1]), h_ref.dtype)
    slab = _slab_rows(tm)
    for r in range(0, tm, slab):
        rows = pl.ds(r, slab)
        _even_in_body(h_ref[rows, :], g_ref, w_ref, [ref.at[rows] for ref in out_refs])


def _embed_even_in(x, meta, g, w, Tp):
    B, S, D = x.shape
    tm = _pick(Tp, tuple(c for c in (544, 512, 384, 256, 128) if c <= S))
    n_tiles = Tp // tm
    M = B * Tp
    win0 = tuple(min(max(k * tm - N_META, 0), S - tm) for k in range(n_tiles))

    def x_index(b, k):
        start = jnp.clip(k * tm - N_META, 0, S - tm)
        return (pl.multiple_of(b * S + start, N_META), 0)

    out_spec = lambda n: pl.BlockSpec((tm, n), lambda b, k: (b * n_tiles + k, 0))
    return pl.pallas_call(
        functools.partial(_embed_even_in_kernel, tm=tm, n_tiles=n_tiles, seq=S, win0=win0),
        out_shape=(jax.ShapeDtypeStruct((M, D), F32),) + _even_in_outs(M),
        grid=(B, n_tiles),
        in_specs=[pl.BlockSpec((pl.Element(tm), pl.Element(D)), x_index),
                  _const_spec(meta.shape), _const_spec(g.shape), _const_spec(w.shape)],
        out_specs=(out_spec(D),) + tuple(out_spec(n) for n in EVEN_OUT_WIDTHS),
        compiler_params=_params(("parallel", "parallel")),
        name="embed_even_in",
    )(x.reshape(B * S, D), meta, g, w)


def _slab_rows(tm):
    return next(tm // n for n in (2, 1) if (tm // n) % 16 == 0 and tm % n == 0)


def _row_tile(M):
    return _pick(M, (512, 256, 128))


def _row_spec(tm, n):
    return pl.BlockSpec((tm, n), lambda i: (i, 0))


def _even_in_outs(M):
    return tuple(jax.ShapeDtypeStruct((M, n), BF16) for n in EVEN_OUT_WIDTHS)


def _lru_pitch(tt):
    return tt if (tt // 8) % 2 else tt + 8


def _rglru_kernel(xa_ref, ga_ref, cw_ref, cb_ref, wg_ref, ba_ref, bx_ref, lam_ref, y_ref,
                  flat, u_s, a_s, hist, h_s, *, tt, pitch, nblk):
    B = xa_ref.shape[0]
    t_idx = pl.program_id(1)

    @pl.when(t_idx == 0)
    def _():
        hist[...] = jnp.zeros_like(hist)
        h_s[...] = jnp.zeros_like(h_s)

    for j in range(nblk):
        lanes = slice(j * LANES, (j + 1) * LANES)
        for b in range(B):
            flat[j, b * pitch:b * pitch + tt, :] = xa_ref[b, :, lanes].astype(F32)

    taps = [[cw_ref[k:k + 1, j * LANES:(j + 1) * LANES] for k in range(CONV_WIDTH)] for j in range(nblk)]
    bias = [cb_ref[:, j * LANES:(j + 1) * LANES] for j in range(nblk)]

    def conv_step(t, carry):
        out = []
        for j in range(nblk):
            x1, x2, x3 = carry[j]
            x0 = flat[j, pl.ds(t, B, stride=pitch), :]
            w = taps[j]
            u_s[j, pl.ds(pl.multiple_of(t * B, B), B), :] = (
                bias[j] + w[3] * x0 + w[2] * x1 + w[1] * x2 + w[0] * x3)
            out.append((x0, x1, x2))
        return tuple(out)

    last = lax.fori_loop(0, tt, conv_step,
                         tuple((hist[j, 0], hist[j, 1], hist[j, 2]) for j in range(nblk)), unroll=8)
    for j in range(nblk):
        for k in range(CONV_WIDTH - 1):
            hist[j, k] = last[j][k]

    for j in range(nblk):
        lanes = slice(j * LANES, (j + 1) * LANES)
        y = u_s[j]
        g = jnp.dot(y.astype(BF16), wg_ref[j], preferred_element_type=F32)
        r = 0.5 + 0.5 * jnp.tanh(0.5 * (g[:, :LANES] + ba_ref[:, lanes]))
        i = _sigmoid(g[:, LANES:] + bx_ref[:, lanes])
        lam = lam_ref[:, lanes]
        softplus_neg_lam = jnp.maximum(-lam, 0.0) + jnp.log(1.0 + jnp.exp(-jnp.abs(lam)))
        a = jnp.exp2(r * ((-RGLRU_C * LOG2_E) * softplus_neg_lam))
        x = 1.0 - a * a
        a_s[j] = a
        u_s[j] = (x * lax.rsqrt(jnp.maximum(x, 1e-30))) * (i * y)

    def scan_step(t, hs):
        out = []
        for j in range(nblk):
            rows = pl.ds(pl.multiple_of(t * B, B), B)
            h = a_s[j, rows, :] * hs[j] + u_s[j, rows, :]
            flat[j, pl.ds(t, B, stride=pitch), :] = h
            out.append(h)
        return tuple(out)

    hs = lax.fori_loop(0, tt, scan_step, tuple(h_s[j] for j in range(nblk)), unroll=8)
    for j in range(nblk):
        h_s[j] = hs[j]

    for j in range(nblk):
        lanes = slice(j * LANES, (j + 1) * LANES)
        for b in range(B):
            ga = ga_ref[b, :, lanes].astype(F32)
            y_ref[b, :, lanes] = (flat[j, b * pitch:b * pitch + tt, :]
                                  * (ga * _sigmoid(ga))).astype(y_ref.dtype)


def _rglru(xa, ga, cw, cb, wg, ba, bx, lam):
    B, Tp, C = xa.shape
    tt = _pick(Tp, (272, 256, 192, 128))
    pitch = _lru_pitch(tt)
    cblk = 512
    nblk = cblk // LANES
    blk = pl.BlockSpec((B, tt, cblk), lambda c, t: (0, t, c))
    vec = lambda rows: pl.BlockSpec((rows, cblk), lambda c, t: (0, c))
    return pl.pallas_call(
        functools.partial(_rglru_kernel, tt=tt, pitch=pitch, nblk=nblk),
        out_shape=jax.ShapeDtypeStruct((B, Tp, C), BF16),
        grid=(C // cblk, Tp // tt),
        in_specs=[blk, blk, vec(CONV_WIDTH), vec(1),
                  pl.BlockSpec((nblk, RNN_BLOCK, 2 * RNN_BLOCK), lambda c, t: (c, 0, 0)),
                  vec(1), vec(1), vec(1)],
        out_specs=blk,
        scratch_shapes=[pltpu.VMEM((nblk, B * pitch, LANES), F32),
                        pltpu.VMEM((nblk, tt * B, LANES), F32),
                        pltpu.VMEM((nblk, tt * B, LANES), F32),
                        pltpu.VMEM((nblk, CONV_WIDTH - 1, B, LANES), F32),
                        pltpu.VMEM((nblk, B, LANES), F32)],
        compiler_params=_params(("parallel", "arbitrary")),
        name="rglru",
    )(xa, ga, cw, cb, wg, ba, bx, lam)


def _gla_kernel(q_ref, k_ref, v_ref, gb_ref, ad_ref, aw_ref, ab_ref, gn_ref, o_ref, st_ref, *, nchunk, nb):
    C = GLA_CHUNK

    @pl.when(pl.program_id(1) == 0)
    def _():
        st_ref[...] = jnp.zeros_like(st_ref)

    row = lax.broadcasted_iota(jnp.int32, (C, C), 0)
    col = lax.broadcasted_iota(jnp.int32, (C, C), 1)
    causal = row >= col
    tril = causal.astype(BF16)
    scale = GLA_DK ** -0.5

    def chunk(c, carry):
        rows = pl.ds(pl.multiple_of(c * C, C), C)
        bs = range(nb)
        hs = range(GLA_HEADS)
        kl = [slice(h * GLA_DK, (h + 1) * GLA_DK) for h in hs]
        vl = [slice(h * GLA_DV, (h + 1) * GLA_DV) for h in hs]
        pre = [jnp.dot(ad_ref[b, rows, :], aw_ref[...], preferred_element_type=F32) + ab_ref[...]
               for b in bs]
        la = [(jnp.minimum(p, 0.0) - jnp.log(1.0 + jnp.exp2(jnp.abs(p) * (-LOG2_E)))) * (LOG2_E / GLA_TAU)
              for p in pre]
        la_hi = [x.astype(BF16) for x in la]
        la_lo = [(x - hi.astype(F32)).astype(BF16) for x, hi in zip(la, la_hi)]
        bc = [jnp.dot(tril, hi, preferred_element_type=F32) + jnp.dot(tril, lo, preferred_element_type=F32)
              for hi, lo in zip(la_hi, la_lo)]
        b_last = [x[C - 1:C, :] for x in bc]
        q = [q_ref[b, rows, :].astype(F32) for b in bs]
        k = [k_ref[b, rows, :].astype(F32) for b in bs]
        q_dec = [(q[b] * (jnp.exp2(bc[b]) * scale)).astype(BF16) for b in bs]
        k_inv = [(k[b] * jnp.exp2(-bc[b])).astype(BF16) for b in bs]
        k_end = [(k[b] * jnp.exp2(b_last[b] - bc[b])).astype(BF16) for b in bs]
        decay = [jnp.exp2(x) for x in b_last]
        s = [[jnp.where(causal,
                        lax.dot_general(q_dec[b][:, kl[h]], k_inv[b][:, kl[h]], (((1,), (1,)), ((), ())),
                                        preferred_element_type=F32), 0.0).astype(BF16)
              for h in hs] for b in bs]
        o = [[None] * GLA_HEADS for _ in bs]
        for b in bs:
            for h in hs:
                vh = v_ref[b, rows, vl[h]]
                st = st_ref[b, h]
                o[b][h] = (jnp.dot(s[b][h], vh, preferred_element_type=F32)
                           + lax.dot_general(q_dec[b][:, kl[h]], st.astype(BF16), (((1,), (1,)), ((), ())),
                                             preferred_element_type=F32))
                vk = lax.dot_general(vh, k_end[b][:, kl[h]], (((0,), (0,)), ((), ())),
                                     preferred_element_type=F32)
                st_ref[b, h] = st * decay[b][:, kl[h]] + vk
        for b in bs:
            for h in hs:
                gb = gb_ref[b, rows, vl[h]].astype(F32)
                o_ref[b, rows, vl[h]] = (_rms(o[b][h], gn_ref[...])
                                         * (gb * _sigmoid(gb))).astype(o_ref.dtype)
        return carry

    lax.fori_loop(0, nchunk, chunk, 0)


GLA_BATCH_PER_STEP = 8
GLA_TIME_TILE = 128


def _gla(q, k, v, gb, ad, aw, ab, gn):
    B, Tp, _ = q.shape
    nb = GLA_BATCH_PER_STEP if B % GLA_BATCH_PER_STEP == 0 else 1
    tc = GLA_TIME_TILE
    kw, vw = GLA_HEADS * GLA_DK, GLA_HEADS * GLA_DV
    spec = lambda w: pl.BlockSpec((nb, tc, w), lambda b, t: (b, t, 0))
    return pl.pallas_call(
        functools.partial(_gla_kernel, nchunk=tc // GLA_CHUNK, nb=nb),
        out_shape=jax.ShapeDtypeStruct((B, Tp, vw), BF16),
        grid=(B // nb, Tp // tc),
        in_specs=[spec(kw), spec(kw), spec(vw), spec(vw), spec(LANES),
                  _const_spec(aw.shape), _const_spec(ab.shape), _const_spec(gn.shape)],
        out_specs=spec(vw),
        scratch_shapes=[pltpu.VMEM((nb, GLA_HEADS, GLA_DV, GLA_DK), F32)],
        compiler_params=_params(("parallel", "arbitrary")),
        name="gla",
    )(q, k, v, gb, ad, aw, ab, gn)


def _even_out_value(h_ref, ya_ref, ob_ref, w_ref):
    n = ya_ref.shape[-1]
    return (h_ref[...]
            + jnp.dot(ya_ref[...], w_ref[:n, :], preferred_element_type=F32)
            + jnp.dot(ob_ref[...], w_ref[n:, :], preferred_element_type=F32))


def _even_out_odd_in_kernel(h_ref, ya_ref, ob_ref, wo_ref, *refs):
    hout_ref = refs[-5]
    h = _even_out_value(h_ref, ya_ref, ob_ref, wo_ref)
    hout_ref[...] = h
    _odd_in_body(h, *refs[:-5], *refs[-4:])


def _even_out_odd_in(h2, ya, ob, wo, params, cos_t, sin_t):
    M = h2.shape[0]
    tm = _row_tile(M)
    return pl.pallas_call(
        _even_out_odd_in_kernel,
        out_shape=(jax.ShapeDtypeStruct((M, D_MODEL), F32),) + _odd_in_outs(M),
        grid=(M // tm,),
        in_specs=[_row_spec(tm, D_MODEL), _row_spec(tm, ya.shape[-1]), _row_spec(tm, ob.shape[-1]),
                  _const_spec(wo.shape)] + _odd_in_specs(tm, params),
        out_specs=(_row_spec(tm, D_MODEL),) + tuple(_row_spec(tm, n) for n in ODD_IN_WIDTHS),
        compiler_params=_params(("parallel",)),
        name="even_out_odd_in",
    )(h2, ya, ob, wo, *params, cos_t, sin_t)


ROPE_HALF = MLA_ROPE // 2
ROPE_X2_LANE = HEAD_PAD // 2


def _head_lanes(nope, rope_part):
    zeros = lambda n: jnp.zeros(nope.shape[:-1] + (n,), nope.dtype)
    n1 = ROPE_X2_LANE - ROPE_HALF
    return jnp.concatenate([rope_part[..., :ROPE_HALF], nope[..., :n1], rope_part[..., ROPE_HALF:],
                            nope[..., n1:], zeros(HEAD_PAD - MLA_NOPE - MLA_ROPE)], axis=-1)


ROPE_GROUPS = LANES // MLA_ROPE


def _rope_table_kernel(pos_ref, freq_ref, phase_ref, c_ref, s_ref):
    tm = pos_ref.shape[0]
    q = tm // ROPE_GROUPS
    lane = lax.broadcasted_iota(jnp.int32, (q, LANES), 1)
    pos = pos_ref[0:q, :]
    for i in range(1, ROPE_GROUPS):
        pos = jnp.where(lane >= i * MLA_ROPE, pos_ref[i * q:(i + 1) * q, :], pos)
    t = jnp.cos(pos * freq_ref[...] - phase_ref[...])
    x1 = lane < ROPE_HALF
    x2 = (lane >= ROPE_X2_LANE) & (lane < ROPE_X2_LANE + ROPE_HALF)
    for i in range(ROPE_GROUPS):
        ti = pltpu.roll(t, (LANES - i * MLA_ROPE) % LANES, axis=1) if i else t
        cos_x2 = pltpu.roll(ti, ROPE_X2_LANE, axis=1)
        sin_x1 = pltpu.roll(ti, LANES - ROPE_HALF, axis=1)
        sin_x2 = pltpu.roll(ti, ROPE_X2_LANE - ROPE_HALF, axis=1)
        rows = slice(i * q, (i + 1) * q)
        c_ref[rows, :] = jnp.where(x1, ti, jnp.where(x2, cos_x2, 1.0))
        s_ref[rows, :] = jnp.where(x1, -sin_x1, jnp.where(x2, sin_x2, 0.0))


def _rope_tables(posf, inv_freq):
    M = posf.shape[0]
    tm = _pick(M, (1088, 1024, 512, 256, 128))
    out = jax.ShapeDtypeStruct((M, HEAD_PAD), F32)
    zeros = jnp.zeros((ROPE_HALF,), F32)
    freq = jnp.tile(jnp.concatenate([inv_freq, inv_freq]), ROPE_GROUPS).reshape(1, LANES)
    phase = jnp.tile(jnp.concatenate([zeros, zeros + np.pi / 2]), ROPE_GROUPS).reshape(1, LANES)
    return pl.pallas_call(
        _rope_table_kernel,
        out_shape=(out, out),
        grid=(M // tm,),
        in_specs=[pl.BlockSpec((tm, 1), lambda i: (i, 0)), _const_spec((1, LANES)), _const_spec((1, LANES))],
        out_specs=(pl.BlockSpec((tm, HEAD_PAD), lambda i: (i, 0)),) * 2,
        compiler_params=_params(("parallel",)),
        name="rope_tables",
    )(posf, freq, phase)


def _rope(x, cos_t, sin_t):
    heads = [pltpu.roll(x[:, c:c + HEAD_PAD], ROPE_X2_LANE, axis=1) for c in range(0, x.shape[1], HEAD_PAD)]
    partner = heads[0] if len(heads) == 1 else jnp.concatenate(heads, axis=1)
    return x * cos_t + partner * sin_t


def _odd_in_body(h, g_ref, w_ref, qn_ref, wq_ref, kvn_ref, wk_ref, wv_ref, cos_ref, sin_ref,
                 q_ref, k_ref, v_ref, gate_ref):
    xn = _rms(h, g_ref[...]).astype(BF16)
    c0, c1, c2 = MLA_Q_RANK, MLA_Q_RANK + MLA_KV_RANK, MLA_Q_RANK + MLA_KV_RANK + HEAD_PAD
    gate_ref[...] = jnp.dot(xn, w_ref[:, c2:], preferred_element_type=F32).astype(gate_ref.dtype)
    cq = jnp.dot(xn, w_ref[:, :c0], preferred_element_type=F32)
    ckv = jnp.dot(xn, w_ref[:, c0:c1], preferred_element_type=F32)
    kr = jnp.dot(xn, w_ref[:, c1:c2], preferred_element_type=F32)
    cos_t, sin_t = cos_ref[...], sin_ref[...]
    kr = _rope(kr, cos_t, sin_t)
    cqn = _rms(cq, qn_ref[...]).astype(BF16)
    ckvn = _rms(ckv, kvn_ref[...]).astype(BF16)
    v_ref[...] = jnp.dot(ckvn, wv_ref[...], preferred_element_type=F32).astype(v_ref.dtype)
    scale = LOG2_E * (MLA_NOPE + MLA_ROPE) ** -0.5
    cos2 = jnp.concatenate([cos_t, cos_t], axis=1) * scale
    sin2 = jnp.concatenate([sin_t, sin_t], axis=1) * scale
    kr2 = jnp.concatenate([kr, kr], axis=1)
    for hp in range(MLA_HEADS // 2):
        lanes = slice(hp * 2 * HEAD_PAD, (hp + 1) * 2 * HEAD_PAD)
        qh = jnp.dot(cqn, wq_ref[:, lanes], preferred_element_type=F32)
        q_ref[:, lanes] = _rope(qh, cos2, sin2).astype(q_ref.dtype)
        kh = jnp.dot(ckvn, wk_ref[:, lanes], preferred_element_type=F32)
        k_ref[:, lanes] = (kh + kr2).astype(k_ref.dtype)


ODD_IN_WIDTHS = (MLA_HEADS * HEAD_PAD, MLA_HEADS * HEAD_PAD, MLA_HEADS * MLA_V, MLA_HEADS * MLA_V)


def _odd_in_outs(M):
    return tuple(jax.ShapeDtypeStruct((M, n), BF16) for n in ODD_IN_WIDTHS)


def _odd_in_specs(tm, params):
    return [_const_spec(p.shape) for p in params] + [_row_spec(tm, HEAD_PAD), _row_spec(tm, HEAD_PAD)]


def _attn_kernel(q_ref, k_ref, v_ref, o_ref, k_scr, vt_scr, qt_scr, bias_scr, s_a, s_b, mx_a, mx_b,
                 m_scr, acc_scr, o_scr, *, nh, n_q):
    T = ATT_TILE
    Tp = q_ref.shape[1]
    Tp2 = n_q * T
    SL = SOFTMAX_SLAB
    VE = ATT_V_EXT

    k_scr[0:Tp, :] = k_ref[0]
    if Tp2 > Tp:
        k_scr[Tp:Tp2, :] = jnp.zeros((Tp2 - Tp, k_scr.shape[1]), k_scr.dtype)
    ones_row = (lax.broadcasted_iota(jnp.int32, (VE - MLA_V, Tp2), 0) == 0).astype(vt_scr.dtype)
    for h in range(nh):
        vt_scr[h * VE + MLA_V:(h + 1) * VE, :] = ones_row
        if Tp2 > Tp:
            vt_scr[h * VE:h * VE + MLA_V, Tp:Tp2] = jnp.zeros((MLA_V, Tp2 - Tp), vt_scr.dtype)
    for c in range(0, Tp, LANES):
        vt = v_ref[0, c:c + LANES, :].T
        for h in range(nh):
            vt_scr[h * VE:h * VE + MLA_V, c:c + LANES] = vt[h * MLA_V:(h + 1) * MLA_V, :]
    for j in range(n_q):
        rows = min(T, Tp - j * T)
        for h in range(nh):
            qh = q_ref[0, j * T:j * T + rows, h * HEAD_PAD:(h + 1) * HEAD_PAD]
            qt_scr[j, h, :, 0:rows] = qh.T
            if rows < T:
                qt_scr[j, h, :, rows:T] = jnp.zeros((HEAD_PAD, T - rows), qt_scr.dtype)
    kk = lax.broadcasted_iota(jnp.int32, (T, T), 0)
    qq = lax.broadcasted_iota(jnp.int32, (T, T), 1)
    bias_scr[...] = jnp.where(kk <= qq, 0.0, NEG_BIG)

    def qk(buf, j, i, diag=False, heads=range(nh)):
        s_ref, mx_ref = buf
        koff = pl.multiple_of(i * T, T)
        for h in heads:
            sv = jnp.dot(k_scr[pl.ds(koff, T), h * HEAD_PAD:(h + 1) * HEAD_PAD],
                         qt_scr[j, h], preferred_element_type=F32)
            if diag:
                sv = sv + bias_scr[...]
            s_ref[h] = sv
            parts = [None] * 4
            for n, r in enumerate(range(0, T, SL)):
                blk = sv[r:r + SL, :]
                parts[n % 4] = blk if parts[n % 4] is None else jnp.maximum(parts[n % 4], blk)
            mt = jnp.maximum(jnp.maximum(parts[0], parts[1]), jnp.maximum(parts[2], parts[3]))
            mx_ref[h] = jnp.max(mt, axis=0, keepdims=True)

    def softmax_pv(buf, j, i, diag, heads=range(nh)):
        s_ref, mx_ref = buf
        koff = pl.multiple_of(i * T, T)
        for h in heads:
            if diag:
                m_new = mx_ref[h]
            else:
                m_old = m_scr[j, h]
                m_new = jnp.maximum(m_old, mx_ref[h])
                alpha = jnp.exp2(m_old - m_new)
            m_scr[j, h] = m_new
            p = jnp.concatenate([jnp.exp2((s_ref[h, r:r + SL, :] - m_new).astype(BF16))
                                 for r in range(0, T, SL)], axis=0)
            rows = slice(h * VE, (h + 1) * VE)
            pv = jnp.dot(vt_scr[rows, pl.ds(koff, T)], p, preferred_element_type=F32)
            acc_scr[j, rows, :] = pv if diag else alpha * acc_scr[j, rows, :] + pv

    bufs = ((s_a, mx_a), (s_b, mx_b))
    G = ATT_STEPS_PER_ITER

    def run(n_steps, first, advance, diag):
        def steps(count, jt):
            for t in range(count):
                nxt = advance(*jt)
                for h in range(nh):
                    qk(bufs[(t + 1) % 2], jnp.minimum(nxt[0], n_q - 1), nxt[1], diag, (h,))
                    softmax_pv(bufs[t % 2], jt[0], jt[1], diag, (h,))
                jt = nxt
            return jt

        qk(bufs[0], first[0], first[1], diag)
        jt = lax.fori_loop(0, n_steps // G, lambda n, c: steps(G, c),
                           (jnp.int32(first[0]), jnp.int32(first[1])))
        steps(n_steps % G, jt)

    run(n_q, (0, 0), lambda j, i: (j + 1, jnp.minimum(i + 1, n_q - 1)), True)

    def advance(j, i):
        last = i + 1 >= j
        return jnp.where(last, j + 1, j), jnp.where(last, 0, i + 1)

    if n_q > 1:
        run(n_q * (n_q - 1) // 2, (1, 0), advance, False)

    def finish(j, c):
        outs = []
        for h in range(nh):
            acc = acc_scr[j, h * VE:h * VE + MLA_V, :]
            den = acc_scr[j, h * VE + MLA_V:h * VE + MLA_V + 1, :]
            outs.append(acc / den)
        o_scr[pl.ds(pl.multiple_of(j * T, T), T), :] = jnp.concatenate(outs, axis=0).T.astype(o_scr.dtype)
        return c

    lax.fori_loop(0, n_q, finish, 0, unroll=3)
    o_ref[0] = o_scr[0:Tp, :]


ATT_HEADS_PER_STEP = 8
ATT_STEPS_PER_ITER = 12
ATT_V_EXT = MLA_V + 16


def _attention(q, k, v):
    B, Tp, _ = q.shape
    nh = ATT_HEADS_PER_STEP
    T = ATT_TILE
    n_q = -(-Tp // T)
    Tp2 = n_q * T
    qk_spec = pl.BlockSpec((1, Tp, nh * HEAD_PAD), lambda b, p: (b, 0, p))
    v_spec = pl.BlockSpec((1, Tp, nh * MLA_V), lambda b, p: (b, 0, p))
    return pl.pallas_call(
        functools.partial(_attn_kernel, nh=nh, n_q=n_q),
        out_shape=jax.ShapeDtypeStruct((B, Tp, MLA_HEADS * MLA_V), BF16),
        grid=(B, MLA_HEADS // nh),
        in_specs=[qk_spec, qk_spec, v_spec],
        out_specs=v_spec,
        scratch_shapes=[pltpu.VMEM((Tp2, nh * HEAD_PAD), BF16),
                        pltpu.VMEM((nh * ATT_V_EXT, Tp2), BF16),
                        pltpu.VMEM((n_q, nh, HEAD_PAD, T), BF16),
                        pltpu.VMEM((T, T), F32),
                        pltpu.VMEM((nh, T, T), F32),
                        pltpu.VMEM((nh, T, T), F32),
                        pltpu.VMEM((nh, 1, T), F32),
                        pltpu.VMEM((nh, 1, T), F32),
                        pltpu.VMEM((n_q, nh, 1, T), F32),
                        pltpu.VMEM((n_q, nh * ATT_V_EXT, T), F32),
                        pltpu.VMEM((Tp2, nh * MLA_V), BF16)],
        compiler_params=_params(("parallel", "parallel")),
        name="mla_attention",
    )(q, k, v)


def _odd_out_value(h_ref, o_ref, gate_ref, w_ref):
    gate = gate_ref[...].astype(F32)
    x = (o_ref[...].astype(F32) * (gate * _sigmoid(gate))).astype(BF16)
    return h_ref[...] + jnp.dot(x, w_ref[...], preferred_element_type=F32)


def _odd_out_even_in_kernel(h_ref, o_ref, gate_ref, wo_ref, g_ref, w_ref, hout_ref, *out_refs):
    tm = h_ref.shape[0]
    slab = _slab_rows(tm)
    for r in range(0, tm, slab):
        rows = pl.ds(r, slab)
        h = _odd_out_value(h_ref.at[rows], o_ref.at[rows], gate_ref.at[rows], wo_ref)
        hout_ref[rows, :] = h
        _even_in_body(h, g_ref, w_ref, [ref.at[rows] for ref in out_refs])


def _odd_out_even_in(h2, o, gate, wo, g, w):
    M = h2.shape[0]
    tm = _row_tile(M)
    return pl.pallas_call(
        _odd_out_even_in_kernel,
        out_shape=(jax.ShapeDtypeStruct((M, D_MODEL), F32),) + _even_in_outs(M),
        grid=(M // tm,),
        in_specs=[_row_spec(tm, D_MODEL), _row_spec(tm, o.shape[-1]), _row_spec(tm, gate.shape[-1]),
                  _const_spec(wo.shape), _const_spec(g.shape), _const_spec(w.shape)],
        out_specs=(_row_spec(tm, D_MODEL),) + tuple(_row_spec(tm, n) for n in EVEN_OUT_WIDTHS),
        compiler_params=_params(("parallel",)),
        name="odd_out_even_in",
    )(h2, o, gate, wo, g, w)


def _odd_out_last_kernel(h_ref, o_ref, gate_ref, w_ref, fn_ref, out_ref):
    out_ref[...] = _rms(_odd_out_value(h_ref, o_ref, gate_ref, w_ref), fn_ref[...])


def _odd_out_last(h2, o, gate, w, fn, B, Tp, S):
    tm = _pick(S, (512, 256, 128, 64))
    win = lambda n: pl.BlockSpec((pl.Element(tm), pl.Element(n)),
                                 lambda b, i: (pl.multiple_of(b * Tp + N_META + i * tm, N_META), 0))
    return pl.pallas_call(
        _odd_out_last_kernel,
        out_shape=jax.ShapeDtypeStruct((B * S, D_MODEL), F32),
        grid=(B, S // tm),
        in_specs=[win(D_MODEL), win(o.shape[-1]), win(gate.shape[-1]), _const_spec(w.shape),
                  _const_spec(fn.shape)],
        out_specs=pl.BlockSpec((tm, D_MODEL), lambda b, i: (b * (S // tm) + i, 0)),
        compiler_params=_params(("parallel", "parallel")),
        name="odd_out_last",
    )(h2, o, gate, w, fn)


def _split_cols(w, sizes):
    idx = np.cumsum(sizes)[:-1].tolist()
    return jnp.split(w, idx, axis=-1)


def _pack_even(w_in, gate_a_w, gate_x_w, alpha_w):
    xa, ga, q, k, v, ad, gb = _split_cols(
        w_in, (RNN_WIDTH, RNN_WIDTH, GLA_HEADS * GLA_DK, GLA_HEADS * GLA_DK, GLA_HEADS * GLA_DV,
               GLA_GATE_RANK, GLA_HEADS * GLA_DV))
    ad = jnp.pad(ad, ((0, 0), (0, LANES - GLA_GATE_RANK)))
    w = jnp.concatenate([xa, ga, q, k, v, gb, ad], axis=-1).astype(BF16)
    wg = jnp.concatenate([gate_a_w, gate_x_w], axis=-1).astype(BF16)
    aw = jnp.pad(alpha_w, ((0, LANES - GLA_GATE_RANK), (0, 0))).astype(BF16)
    return w, wg, aw


def _pack_odd(w_in, w_q_up, w_kv_up):
    cq, ckv, kr, gate = _split_cols(w_in, (MLA_Q_RANK, MLA_KV_RANK, MLA_ROPE, MLA_HEADS * MLA_V))
    kr = _head_lanes(jnp.zeros((kr.shape[0], MLA_NOPE), kr.dtype), kr)
    w = jnp.concatenate([cq, ckv, kr, gate], axis=-1).astype(BF16)
    wq = w_q_up.reshape(MLA_Q_RANK, MLA_HEADS, MLA_NOPE + MLA_ROPE)
    wq = _head_lanes(wq[..., :MLA_NOPE], wq[..., MLA_NOPE:]).reshape(MLA_Q_RANK, MLA_HEADS * HEAD_PAD)
    wkv = w_kv_up.reshape(MLA_KV_RANK, MLA_HEADS, MLA_NOPE + MLA_V)
    wk = _head_lanes(wkv[..., :MLA_NOPE], jnp.zeros(wkv.shape[:-1] + (MLA_ROPE,), wkv.dtype))
    wk = wk.reshape(MLA_KV_RANK, MLA_HEADS * HEAD_PAD)
    wv = wkv[..., MLA_NOPE:].reshape(MLA_KV_RANK, MLA_HEADS * MLA_V)
    return w, wq.astype(BF16), wk.astype(BF16), wv.astype(BF16)


def kernel(x, positions, meta_tokens, ab_norm, ab_w_in, ab_conv_w, ab_conv_b, ab_gate_a_w, ab_gate_a_b, ab_gate_x_w, ab_gate_x_b, ab_lru_lambda, ab_alpha_w, ab_alpha_b, ab_gla_norm, ab_w_out, c_norm, c_w_in, c_q_norm, c_w_q_up, c_kv_norm, c_w_kv_up, c_w_out, final_norm):
    B, S, D = x.shape
    T = N_META + S
    Tp = -(-T // LANES) * LANES
    M = B * Tp
    depth = ab_norm.shape[0] + c_norm.shape[0]
    row2 = lambda a: a.reshape(1, -1).astype(F32)


    meta_pos = jnp.broadcast_to(jnp.arange(N_META, dtype=positions.dtype)[None], (B, N_META))
    pos = jnp.concatenate([meta_pos, positions + N_META,
                           jnp.zeros((B, Tp - T), positions.dtype)], axis=1)
    inv_freq = ROPE_BASE ** (-jnp.arange(0, MLA_ROPE, 2, dtype=F32) / MLA_ROPE)
    cos_t, sin_t = _rope_tables(pos.astype(F32).reshape(M, 1), inv_freq)

    if depth % 2 or ab_norm.shape[0] != c_norm.shape[0]:
        raise NotImplementedError("layers must alternate (RG-LRU || GLA), MLA and end on an MLA layer")
    r3 = lambda a: a.reshape(B, Tp, a.shape[-1])
    even_packs = [_pack_even(ab_w_in[j], ab_gate_a_w[j], ab_gate_x_w[j], ab_alpha_w[j])
                  for j in range(depth // 2)]
    h, *even_in = _embed_even_in(x, meta_tokens.astype(x.dtype), row2(ab_norm[0]), even_packs[0][0], Tp)
    for j in range(depth // 2):
        w, wg, aw = even_packs[j]
        xa, ga, q, k, v, gb, ad = even_in
        ya = _rglru(r3(xa), r3(ga), ab_conv_w[j].astype(F32), row2(ab_conv_b[j]), wg,
                    row2(ab_gate_a_b[j]), row2(ab_gate_x_b[j]), row2(ab_lru_lambda[j]))
        ob = _gla(r3(q), r3(k), r3(v), r3(gb), r3(ad), aw, row2(ab_alpha_b[j]), row2(ab_gla_norm[j]))
        w, wq, wk, wv = _pack_odd(c_w_in[j], c_w_q_up[j], c_w_kv_up[j])
        odd_params = (row2(c_norm[j]), w, row2(c_q_norm[j]), wq, row2(c_kv_norm[j]), wk, wv)
        h, q, k, v, gate = _even_out_odd_in(h, ya.reshape(M, -1), ob.reshape(M, -1),
                                            ab_w_out[j].astype(BF16), odd_params, cos_t, sin_t)
        o = _attention(r3(q), r3(k), r3(v)).reshape(M, -1)
        wo = c_w_out[j].astype(BF16)
        if j == depth // 2 - 1:
            return _odd_out_last(h, o, gate, wo, row2(final_norm), B, Tp, S).reshape(B, S, D)
        h, *even_in = _odd_out_even_in(h, o, gate, wo, row2(ab_norm[j + 1]), even_packs[j + 1][0])
```

```python
import functools

import jax
import jax.numpy as jnp
import numpy as np
from jax import lax
from jax.experimental import pallas as pl
from jax.experimental.pallas import tpu as pltpu

F32 = jnp.float32
BF16 = jnp.bfloat16

D_MODEL = 1024
N_META = 16
EPS = 1e-6
RNN_WIDTH = D_MODEL
RNN_BLOCKS = 8
RNN_BLOCK = RNN_WIDTH // RNN_BLOCKS
CONV_WIDTH = 4
RGLRU_C = 8.0
GLA_HEADS = 4
GLA_DK = 128
GLA_DV = 256
GLA_GATE_RANK = 16
GLA_TAU = 16.0
GLA_CHUNK = 64
MLA_HEADS = 16
MLA_NOPE = 64
MLA_ROPE = 32
MLA_V = 64
MLA_Q_RANK = 512
MLA_KV_RANK = 256
ROPE_BASE = 10000.0

LANES = 128
HEAD_PAD = 128
ATT_TILE = 256
SOFTMAX_SLAB = 32
LOG2_E = 1.4426950408889634
NEG_BIG = -1e30
VMEM_LIMIT = 56 * 1024 * 1024


def _pick(n, candidates):
    for c in candidates:
        if n % c == 0:
            return c
    raise ValueError(f"no tile in {candidates} divides {n}")


def _rms(x, g):
    var = jnp.mean(x * x, axis=-1, keepdims=True)
    return x * lax.rsqrt(var + EPS) * g


def _sigmoid(x):
    return 1.0 / (1.0 + jnp.exp2(x * (-LOG2_E)))


def _const_spec(shape):
    nd = len(shape)
    return pl.BlockSpec(shape, lambda *_: (0,) * nd, pipeline_mode=pl.Buffered(1))


def _params(sem):
    return pltpu.CompilerParams(dimension_semantics=sem, vmem_limit_bytes=VMEM_LIMIT)


EVEN_OUT_WIDTHS = (RNN_WIDTH, RNN_WIDTH, GLA_HEADS * GLA_DK, GLA_HEADS * GLA_DK,
                   GLA_HEADS * GLA_DV, GLA_HEADS * GLA_DV, LANES)


def _even_in_body(h, g_ref, w_ref, out_refs):
    xn = _rms(h, g_ref[...]).astype(BF16)
    off = 0
    for ref in out_refs:
        n = ref.shape[-1]
        ref[...] = jnp.dot(xn, w_ref[:, off:off + n], preferred_element_type=F32).astype(ref.dtype)
        off += n


def _embed_even_in_kernel(x_ref, meta_ref, g_ref, w_ref, h_ref, *out_refs, tm, n_tiles, seq, win0):
    k_id = pl.program_id(1)
    for k in range(n_tiles):
        @pl.when(k_id == k)
        def _(k=k):
            t0 = k * tm
            n_meta = max(0, min(N_META - t0, tm))
            x_lo = t0 + n_meta - N_META
            n_x = max(0, min(seq - x_lo, tm - n_meta))
            if n_meta:
                h_ref[0:n_meta, :] = meta_ref[t0:t0 + n_meta, :]
            if n_x:
                off = x_lo - win0[k]
                h_ref[n_meta:n_meta + n_x, :] = x_ref[off:off + n_x, :]
            if n_meta + n_x < tm:
                h_ref[n_meta + n_x:tm, :] = jnp.zeros((tm - n_meta - n_x, h_ref.shape[1]), h_ref.dtype)
    slab = _slab_rows(tm)
    for r in range(0, tm, slab):
        rows = pl.ds(r, slab)
        _even_in_body(h_ref[rows, :], g_ref, w_ref, [ref.at[rows] for ref in out_refs])


def _embed_even_in(x, meta, g, w, Tp):
    B, S, D = x.shape
    tm = _pick(Tp, tuple(c for c in (544, 512, 384, 256, 128) if c <= S))
    n_tiles = Tp // tm
    M = B * Tp
    win0 = tuple(min(max(k * tm - N_META, 0), S - tm) for k in range(n_tiles))

    def x_index(b, k):
        start = jnp.clip(k * tm - N_META, 0, S - tm)
        return (pl.multiple_of(b * S + start, N_META), 0)

    out_spec = lambda n: pl.BlockSpec((tm, n), lambda b, k: (b * n_tiles + k, 0))
    return pl.pallas_call(
        functools.partial(_embed_even_in_kernel, tm=tm, n_tiles=n_tiles, seq=S, win0=win0),
        out_shape=(jax.ShapeDtypeStruct((M, D), F32),) + _even_in_outs(M),
        grid=(B, n_tiles),
        in_specs=[pl.BlockSpec((pl.Element(tm), pl.Element(D)), x_index),
                  _const_spec(meta.shape), _const_spec(g.shape), _const_spec(w.shape)],
        out_specs=(out_spec(D),) + tuple(out_spec(n) for n in EVEN_OUT_WIDTHS),
        compiler_params=_params(("parallel", "parallel")),
        name="embed_even_in",
    )(x.reshape(B * S, D), meta, g, w)


def _slab_rows(tm):
    return next(tm // n for n in (2, 1) if (tm // n) % 16 == 0 and tm % n == 0)


def _row_tile(M):
    return _pick(M, (512, 256, 128))


def _row_spec(tm, n):
    return pl.BlockSpec((tm, n), lambda i: (i, 0))


def _even_in_outs(M):
    return tuple(jax.ShapeDtypeStruct((M, n), BF16) for n in EVEN_OUT_WIDTHS)


def _lru_pitch(tt):
    return tt if (tt // 8) % 2 else tt + 8


def _rglru_kernel(xa_ref, ga_ref, cw_ref, cb_ref, wg_ref, ba_ref, bx_ref, lam_ref, y_ref,
                  flat, u_s, a_s, hist, h_s, *, tt, pitch, nblk):
    B = xa_ref.shape[0]
    t_idx = pl.program_id(1)

    @pl.when(t_idx == 0)
    def _():
        hist[...] = jnp.zeros_like(hist)
        h_s[...] = jnp.zeros_like(h_s)

    for j in range(nblk):
        lanes = slice(j * LANES, (j + 1) * LANES)
        for b in range(B):
            flat[j, b * pitch:b * pitch + tt, :] = xa_ref[b, :, lanes].astype(F32)

    taps = [[cw_ref[k:k + 1, j * LANES:(j + 1) * LANES] for k in range(CONV_WIDTH)] for j in range(nblk)]
    bias = [cb_ref[:, j * LANES:(j + 1) * LANES] for j in range(nblk)]

    def conv_step(t, carry):
        out = []
        for j in range(nblk):
            x1, x2, x3 = carry[j]
            x0 = flat[j, pl.ds(t, B, stride=pitch), :]
            w = taps[j]
            u_s[j, pl.ds(pl.multiple_of(t * B, B), B), :] = (
                bias[j] + w[3] * x0 + w[2] * x1 + w[1] * x2 + w[0] * x3)
            out.append((x0, x1, x2))
        return tuple(out)

    last = lax.fori_loop(0, tt, conv_step,
                         tuple((hist[j, 0], hist[j, 1], hist[j, 2]) for j in range(nblk)), unroll=8)
    for j in range(nblk):
        for k in range(CONV_WIDTH - 1):
            hist[j, k] = last[j][k]

    for j in range(nblk):
        lanes = slice(j * LANES, (j + 1) * LANES)
        y = u_s[j]
        g = jnp.dot(y.astype(BF16), wg_ref[j], preferred_element_type=F32)
        r = 0.5 + 0.5 * jnp.tanh(0.5 * (g[:, :LANES] + ba_ref[:, lanes]))
        i = _sigmoid(g[:, LANES:] + bx_ref[:, lanes])
        lam = lam_ref[:, lanes]
        softplus_neg_lam = jnp.maximum(-lam, 0.0) + jnp.log(1.0 + jnp.exp(-jnp.abs(lam)))
        a = jnp.exp2(r * ((-RGLRU_C * LOG2_E) * softplus_neg_lam))
        x = 1.0 - a * a
        a_s[j] = a
        u_s[j] = (x * lax.rsqrt(jnp.maximum(x, 1e-30))) * (i * y)

    def scan_step(t, hs):
        out = []
        for j in range(nblk):
            rows = pl.ds(pl.multiple_of(t * B, B), B)
            h = a_s[j, rows, :] * hs[j] + u_s[j, rows, :]
            flat[j, pl.ds(t, B, stride=pitch), :] = h
            out.append(h)
        return tuple(out)

    hs = lax.fori_loop(0, tt, scan_step, tuple(h_s[j] for j in range(nblk)), unroll=8)
    for j in range(nblk):
        h_s[j] = hs[j]

    for j in range(nblk):
        lanes = slice(j * LANES, (j + 1) * LANES)
        for b in range(B):
            ga = ga_ref[b, :, lanes].astype(F32)
            y_ref[b, :, lanes] = (flat[j, b * pitch:b * pitch + tt, :]
                                  * (ga * _sigmoid(ga))).astype(y_ref.dtype)


def _rglru(xa, ga, cw, cb, wg, ba, bx, lam):
    B, Tp, C = xa.shape
    tt = _pick(Tp, (272, 256, 192, 128))
    pitch = _lru_pitch(tt)
    cblk = 512
    nblk = cblk // LANES
    blk = pl.BlockSpec((B, tt, cblk), lambda c, t: (0, t, c))
    vec = lambda rows: pl.BlockSpec((rows, cblk), lambda c, t: (0, c))
    return pl.pallas_call(
        functools.partial(_rglru_kernel, tt=tt, pitch=pitch, nblk=nblk),
        out_shape=jax.ShapeDtypeStruct((B, Tp, C), BF16),
        grid=(C // cblk, Tp // tt),
        in_specs=[blk, blk, vec(CONV_WIDTH), vec(1),
                  pl.BlockSpec((nblk, RNN_BLOCK, 2 * RNN_BLOCK), lambda c, t: (c, 0, 0)),
                  vec(1), vec(1), vec(1)],
        out_specs=blk,
        scratch_shapes=[pltpu.VMEM((nblk, B * pitch, LANES), F32),
                        pltpu.VMEM((nblk, tt * B, LANES), F32),
                        pltpu.VMEM((nblk, tt * B, LANES), F32),
                        pltpu.VMEM((nblk, CONV_WIDTH - 1, B, LANES), F32),
                        pltpu.VMEM((nblk, B, LANES), F32)],
        compiler_params=_params(("parallel", "arbitrary")),
        name="rglru",
    )(xa, ga, cw, cb, wg, ba, bx, lam)


def _gla_kernel(q_ref, k_ref, v_ref, gb_ref, ad_ref, aw_ref, ab_ref, gn_ref, o_ref, st_ref, *, nchunk, nb):
    C = GLA_CHUNK

    @pl.when(pl.program_id(1) == 0)
    def _():
        st_ref[...] = jnp.zeros_like(st_ref)

    row = lax.broadcasted_iota(jnp.int32, (C, C), 0)
    col = lax.broadcasted_iota(jnp.int32, (C, C), 1)
    causal = row >= col
    tril = causal.astype(BF16)
    scale = GLA_DK ** -0.5

    def chunk(c, carry):
        rows = pl.ds(pl.multiple_of(c * C, C), C)
        bs = range(nb)
        hs = range(GLA_HEADS)
        kl = [slice(h * GLA_DK, (h + 1) * GLA_DK) for h in hs]
        vl = [slice(h * GLA_DV, (h + 1) * GLA_DV) for h in hs]
        pre = [jnp.dot(ad_ref[b, rows, :], aw_ref[...], preferred_element_type=F32) + ab_ref[...]
               for b in bs]
        la = [(jnp.minimum(p, 0.0) - jnp.log(1.0 + jnp.exp2(jnp.abs(p) * (-LOG2_E)))) * (LOG2_E / GLA_TAU)
              for p in pre]
        la_hi = [x.astype(BF16) for x in la]
        la_lo = [(x - hi.astype(F32)).astype(BF16) for x, hi in zip(la, la_hi)]
        bc = [jnp.dot(tril, hi, preferred_element_type=F32) + jnp.dot(tril, lo, preferred_element_type=F32)
              for hi, lo in zip(la_hi, la_lo)]
        b_last = [x[C - 1:C, :] for x in bc]
        q = [q_ref[b, rows, :].astype(F32) for b in bs]
        k = [k_ref[b, rows, :].astype(F32) for b in bs]
        q_dec = [(q[b] * (jnp.exp2(bc[b]) * scale)).astype(BF16) for b in bs]
        k_inv = [(k[b] * jnp.exp2(-bc[b])).astype(BF16) for b in bs]
        k_end = [(k[b] * jnp.exp2(b_last[b] - bc[b])).astype(BF16) for b in bs]
        decay = [jnp.exp2(x) for x in b_last]
        s = [[jnp.where(causal,
                        lax.dot_general(q_dec[b][:, kl[h]], k_inv[b][:, kl[h]], (((1,), (1,)), ((), ())),
                                        preferred_element_type=F32), 0.0).astype(BF16)
              for h in hs] for b in bs]
        o = [[None] * GLA_HEADS for _ in bs]
        for b in bs:
            for h in hs:
                vh = v_ref[b, rows, vl[h]]
                st = st_ref[b, h]
                o[b][h] = (jnp.dot(s[b][h], vh, preferred_element_type=F32)
                           + lax.dot_general(q_dec[b][:, kl[h]], st.astype(BF16), (((1,), (1,)), ((), ())),
                                             preferred_element_type=F32))
                vk = lax.dot_general(vh, k_end[b][:, kl[h]], (((0,), (0,)), ((), ())),
                                     preferred_element_type=F32)
                st_ref[b, h] = st * decay[b][:, kl[h]] + vk
        for b in bs:
            for h in hs:
                gb = gb_ref[b, rows, vl[h]].astype(F32)
                o_ref[b, rows, vl[h]] = (_rms(o[b][h], gn_ref[...])
                                         * (gb * _sigmoid(gb))).astype(o_ref.dtype)
        return carry

    lax.fori_loop(0, nchunk, chunk, 0)


GLA_BATCH_PER_STEP = 8
GLA_TIME_TILE = 128


def _gla(q, k, v, gb, ad, aw, ab, gn):
    B, Tp, _ = q.shape
    nb = GLA_BATCH_PER_STEP if B % GLA_BATCH_PER_STEP == 0 else 1
    tc = GLA_TIME_TILE
    kw, vw = GLA_HEADS * GLA_DK, GLA_HEADS * GLA_DV
    spec = lambda w: pl.BlockSpec((nb, tc, w), lambda b, t: (b, t, 0))
    return pl.pallas_call(
        functools.partial(_gla_kernel, nchunk=tc // GLA_CHUNK, nb=nb),
        out_shape=jax.ShapeDtypeStruct((B, Tp, vw), BF16),
        grid=(B // nb, Tp // tc),
        in_specs=[spec(kw), spec(kw), spec(vw), spec(vw), spec(LANES),
                  _const_spec(aw.shape), _const_spec(ab.shape), _const_spec(gn.shape)],
        out_specs=spec(vw),
        scratch_shapes=[pltpu.VMEM((nb, GLA_HEADS, GLA_DV, GLA_DK), F32)],
        compiler_params=_params(("parallel", "arbitrary")),
        name="gla",
    )(q, k, v, gb, ad, aw, ab, gn)


def _even_out_value(h_ref, ya_ref, ob_ref, w_ref):
    n = ya_ref.shape[-1]
    return (h_ref[...]
            + jnp.dot(ya_ref[...], w_ref[:n, :], preferred_element_type=F32)
            + jnp.dot(ob_ref[...], w_ref[n:, :], preferred_element_type=F32))


def _even_out_odd_in_kernel(h_ref, ya_ref, ob_ref, wo_ref, *refs):
    hout_ref = refs[-5]
    h = _even_out_value(h_ref, ya_ref, ob_ref, wo_ref)
    hout_ref[...] = h
    _odd_in_body(h, *refs[:-5], *refs[-4:])


def _even_out_odd_in(h2, ya, ob, wo, params, cos_t, sin_t):
    M = h2.shape[0]
    tm = _row_tile(M)
    return pl.pallas_call(
        _even_out_odd_in_kernel,
        out_shape=(jax.ShapeDtypeStruct((M, D_MODEL), F32),) + _odd_in_outs(M),
        grid=(M // tm,),
        in_specs=[_row_spec(tm, D_MODEL), _row_spec(tm, ya.shape[-1]), _row_spec(tm, ob.shape[-1]),
                  _const_spec(wo.shape)] + _odd_in_specs(tm, params),
        out_specs=(_row_spec(tm, D_MODEL),) + tuple(_row_spec(tm, n) for n in ODD_IN_WIDTHS),
        compiler_params=_params(("parallel",)),
        name="even_out_odd_in",
    )(h2, ya, ob, wo, *params, cos_t, sin_t)


ROPE_HALF = MLA_ROPE // 2
ROPE_X2_LANE = HEAD_PAD // 2


def _head_lanes(nope, rope_part):
    zeros = lambda n: jnp.zeros(nope.shape[:-1] + (n,), nope.dtype)
    n1 = ROPE_X2_LANE - ROPE_HALF
    return jnp.concatenate([rope_part[..., :ROPE_HALF], nope[..., :n1], rope_part[..., ROPE_HALF:],
                            nope[..., n1:], zeros(HEAD_PAD - MLA_NOPE - MLA_ROPE)], axis=-1)


ROPE_GROUPS = LANES // MLA_ROPE


def _rope_table_kernel(pos_ref, freq_ref, phase_ref, c_ref, s_ref):
    tm = pos_ref.shape[0]
    q = tm // ROPE_GROUPS
    lane = lax.broadcasted_iota(jnp.int32, (q, LANES), 1)
    pos = pos_ref[0:q, :]
    for i in range(1, ROPE_GROUPS):
        pos = jnp.where(lane >= i * MLA_ROPE, pos_ref[i * q:(i + 1) * q, :], pos)
    t = jnp.cos(pos * freq_ref[...] - phase_ref[...])
    x1 = lane < ROPE_HALF
    x2 = (lane >= ROPE_X2_LANE) & (lane < ROPE_X2_LANE + ROPE_HALF)
    for i in range(ROPE_GROUPS):
        ti = pltpu.roll(t, (LANES - i * MLA_ROPE) % LANES, axis=1) if i else t
        cos_x2 = pltpu.roll(ti, ROPE_X2_LANE, axis=1)
        sin_x1 = pltpu.roll(ti, LANES - ROPE_HALF, axis=1)
        sin_x2 = pltpu.roll(ti, ROPE_X2_LANE - ROPE_HALF, axis=1)
        rows = slice(i * q, (i + 1) * q)
        c_ref[rows, :] = jnp.where(x1, ti, jnp.where(x2, cos_x2, 1.0))
        s_ref[rows, :] = jnp.where(x1, -sin_x1, jnp.where(x2, sin_x2, 0.0))


def _rope_tables(posf, inv_freq):
    M = posf.shape[0]
    tm = _pick(M, (1088, 1024, 512, 256, 128))
    out = jax.ShapeDtypeStruct((M, HEAD_PAD), F32)
    zeros = jnp.zeros((ROPE_HALF,), F32)
    freq = jnp.tile(jnp.concatenate([inv_freq, inv_freq]), ROPE_GROUPS).reshape(1, LANES)
    phase = jnp.tile(jnp.concatenate([zeros, zeros + np.pi / 2]), ROPE_GROUPS).reshape(1, LANES)
    return pl.pallas_call(
        _rope_table_kernel,
        out_shape=(out, out),
        grid=(M // tm,),
        in_specs=[pl.BlockSpec((tm, 1), lambda i: (i, 0)), _const_spec((1, LANES)), _const_spec((1, LANES))],
        out_specs=(pl.BlockSpec((tm, HEAD_PAD), lambda i: (i, 0)),) * 2,
        compiler_params=_params(("parallel",)),
        name="rope_tables",
    )(posf, freq, phase)


def _rope(x, cos_t, sin_t):
    heads = [pltpu.roll(x[:, c:c + HEAD_PAD], ROPE_X2_LANE, axis=1) for c in range(0, x.shape[1], HEAD_PAD)]
    partner = heads[0] if len(heads) == 1 else jnp.concatenate(heads, axis=1)
    return x * cos_t + partner * sin_t


def _odd_in_body(h, g_ref, w_ref, qn_ref, wq_ref, kvn_ref, wk_ref, wv_ref, cos_ref, sin_ref,
                 q_ref, k_ref, v_ref, gate_ref):
    xn = _rms(h, g_ref[...]).astype(BF16)
    c0, c1, c2 = MLA_Q_RANK, MLA_Q_RANK + MLA_KV_RANK, MLA_Q_RANK + MLA_KV_RANK + HEAD_PAD
    gate_ref[...] = jnp.dot(xn, w_ref[:, c2:], preferred_element_type=F32).astype(gate_ref.dtype)
    cq = jnp.dot(xn, w_ref[:, :c0], preferred_element_type=F32)
    ckv = jnp.dot(xn, w_ref[:, c0:c1], preferred_element_type=F32)
    kr = jnp.dot(xn, w_ref[:, c1:c2], preferred_element_type=F32)
    cos_t, sin_t = cos_ref[...], sin_ref[...]
    kr = _rope(kr, cos_t, sin_t)
    cqn = _rms(cq, qn_ref[...]).astype(BF16)
    ckvn = _rms(ckv, kvn_ref[...]).astype(BF16)
    v_ref[...] = jnp.dot(ckvn, wv_ref[...], preferred_element_type=F32).astype(v_ref.dtype)
    scale = LOG2_E * (MLA_NOPE + MLA_ROPE) ** -0.5
    cos2 = jnp.concatenate([cos_t, cos_t], axis=1) * scale
    sin2 = jnp.concatenate([sin_t, sin_t], axis=1) * scale
    kr2 = jnp.concatenate([kr, kr], axis=1)
    for hp in range(MLA_HEADS // 2):
        lanes = slice(hp * 2 * HEAD_PAD, (hp + 1) * 2 * HEAD_PAD)
        qh = jnp.dot(cqn, wq_ref[:, lanes], preferred_element_type=F32)
        q_ref[:, lanes] = _rope(qh, cos2, sin2).astype(q_ref.dtype)
        kh = jnp.dot(ckvn, wk_ref[:, lanes], preferred_element_type=F32)
        k_ref[:, lanes] = (kh + kr2).astype(k_ref.dtype)


ODD_IN_WIDTHS = (MLA_HEADS * HEAD_PAD, MLA_HEADS * HEAD_PAD, MLA_HEADS * MLA_V, MLA_HEADS * MLA_V)


def _odd_in_outs(M):
    return tuple(jax.ShapeDtypeStruct((M, n), BF16) for n in ODD_IN_WIDTHS)


def _odd_in_specs(tm, params):
    return [_const_spec(p.shape) for p in params] + [_row_spec(tm, HEAD_PAD), _row_spec(tm, HEAD_PAD)]


def _attn_kernel(q_ref, k_ref, v_ref, o_ref, k_scr, vt_scr, qt_scr, bias_scr, s_a, s_b, mx_a, mx_b,
                 m_scr, acc_scr, o_scr, *, nh, n_q):
    T = ATT_TILE
    Tp = q_ref.shape[1]
    Tp2 = n_q * T
    SL = SOFTMAX_SLAB
    VE = ATT_V_EXT

    k_scr[0:Tp, :] = k_ref[0]
    if Tp2 > Tp:
        k_scr[Tp:Tp2, :] = jnp.zeros((Tp2 - Tp, k_scr.shape[1]), k_scr.dtype)
    ones_row = (lax.broadcasted_iota(jnp.int32, (VE - MLA_V, Tp2), 0) == 0).astype(vt_scr.dtype)
    for h in range(nh):
        vt_scr[h * VE + MLA_V:(h + 1) * VE, :] = ones_row
        if Tp2 > Tp:
            vt_scr[h * VE:h * VE + MLA_V, Tp:Tp2] = jnp.zeros((MLA_V, Tp2 - Tp), vt_scr.dtype)
    for c in range(0, Tp, LANES):
        vt = v_ref[0, c:c + LANES, :].T
        for h in range(nh):
            vt_scr[h * VE:h * VE + MLA_V, c:c + LANES] = vt[h * MLA_V:(h + 1) * MLA_V, :]
    for j in range(n_q):
        rows = min(T, Tp - j * T)
        for h in range(nh):
            qh = q_ref[0, j * T:j * T + rows, h * HEAD_PAD:(h + 1) * HEAD_PAD]
            qt_scr[j, h, :, 0:rows] = qh.T
            if rows < T:
                qt_scr[j, h, :, rows:T] = jnp.zeros((HEAD_PAD, T - rows), qt_scr.dtype)
    kk = lax.broadcasted_iota(jnp.int32, (T, T), 0)
    qq = lax.broadcasted_iota(jnp.int32, (T, T), 1)
    bias_scr[...] = jnp.where(kk <= qq, 0.0, NEG_BIG)

    def qk(buf, j, i, diag=False, heads=range(nh)):
        s_ref, mx_ref = buf
        koff = pl.multiple_of(i * T, T)
        for h in heads:
            sv = jnp.dot(k_scr[pl.ds(koff, T), h * HEAD_PAD:(h + 1) * HEAD_PAD],
                         qt_scr[j, h], preferred_element_type=F32)
            if diag:
                sv = sv + bias_scr[...]
            s_ref[h] = sv
            parts = [None] * 4
            for n, r in enumerate(range(0, T, SL)):
                blk = sv[r:r + SL, :]
                parts[n % 4] = blk if parts[n % 4] is None else jnp.maximum(parts[n % 4], blk)
            mt = jnp.maximum(jnp.maximum(parts[0], parts[1]), jnp.maximum(parts[2], parts[3]))
            mx_ref[h] = jnp.max(mt, axis=0, keepdims=True)

    def softmax_pv(buf, j, i, diag, heads=range(nh)):
        s_ref, mx_ref = buf
        koff = pl.multiple_of(i * T, T)
        for h in heads:
            if diag:
                m_new = mx_ref[h]
            else:
                m_old = m_scr[j, h]
                m_new = jnp.maximum(m_old, mx_ref[h])
                alpha = jnp.exp2(m_old - m_new)
            m_scr[j, h] = m_new
            p = jnp.concatenate([jnp.exp2((s_ref[h, r:r + SL, :] - m_new).astype(BF16))
                                 for r in range(0, T, SL)], axis=0)
            rows = slice(h * VE, (h + 1) * VE)
            pv = jnp.dot(vt_scr[rows, pl.ds(koff, T)], p, preferred_element_type=F32)
            acc_scr[j, rows, :] = pv if diag else alpha * acc_scr[j, rows, :] + pv

    bufs = ((s_a, mx_a), (s_b, mx_b))
    G = ATT_STEPS_PER_ITER

    def run(n_steps, first, advance, diag):
        def steps(count, jt):
            for t in range(count):
                nxt = advance(*jt)
                for h in range(nh):
                    qk(bufs[(t + 1) % 2], jnp.minimum(nxt[0], n_q - 1), nxt[1], diag, (h,))
                    softmax_pv(bufs[t % 2], jt[0], jt[1], diag, (h,))
                jt = nxt
            return jt

        qk(bufs[0], first[0], first[1], diag)
        jt = lax.fori_loop(0, n_steps // G, lambda n, c: steps(G, c),
                           (jnp.int32(first[0]), jnp.int32(first[1])))
        steps(n_steps % G, jt)

    run(n_q, (0, 0), lambda j, i: (j + 1, jnp.minimum(i + 1, n_q - 1)), True)

    def advance(j, i):
        last = i + 1 >= j
        return jnp.where(last, j + 1, j), jnp.where(last, 0, i + 1)

    if n_q > 1:
        run(n_q * (n_q - 1) // 2, (1, 0), advance, False)

    def finish(j, c):
        outs = []
        for h in range(nh):
            acc = acc_scr[j, h * VE:h * VE + MLA_V, :]
            den = acc_scr[j, h * VE + MLA_V:h * VE + MLA_V + 1, :]
            outs.append(acc / den)
        o_scr[pl.ds(pl.multiple_of(j * T, T), T), :] = jnp.concatenate(outs, axis=0).T.astype(o_scr.dtype)
        return c

    lax.fori_loop(0, n_q, finish, 0, unroll=3)
    o_ref[0] = o_scr[0:Tp, :]


ATT_HEADS_PER_STEP = 8
ATT_STEPS_PER_ITER = 12
ATT_V_EXT = MLA_V + 16


def _attention(q, k, v):
    B, Tp, _ = q.shape
    nh = ATT_HEADS_PER_STEP
    T = ATT_TILE
    n_q = -(-Tp // T)
    Tp2 = n_q * T
    qk_spec = pl.BlockSpec((1, Tp, nh * HEAD_PAD), lambda b, p: (b, 0, p))
    v_spec = pl.BlockSpec((1, Tp, nh * MLA_V), lambda b, p: (b, 0, p))
    return pl.pallas_call(
        functools.partial(_attn_kernel, nh=nh, n_q=n_q),
        out_shape=jax.ShapeDtypeStruct((B, Tp, MLA_HEADS * MLA_V), BF16),
        grid=(B, MLA_HEADS // nh),
        in_specs=[qk_spec, qk_spec, v_spec],
        out_specs=v_spec,
        scratch_shapes=[pltpu.VMEM((Tp2, nh * HEAD_PAD), BF16),
                        pltpu.VMEM((nh * ATT_V_EXT, Tp2), BF16),
                        pltpu.VMEM((n_q, nh, HEAD_PAD, T), BF16),
                        pltpu.VMEM((T, T), F32),
                        pltpu.VMEM((nh, T, T), F32),
                        pltpu.VMEM((nh, T, T), F32),
                        pltpu.VMEM((nh, 1, T), F32),
                        pltpu.VMEM((nh, 1, T), F32),
                        pltpu.VMEM((n_q, nh, 1, T), F32),
                        pltpu.VMEM((n_q, nh * ATT_V_EXT, T), F32),
                        pltpu.VMEM((Tp2, nh * MLA_V), BF16)],
        compiler_params=_params(("parallel", "parallel")),
        name="mla_attention",
    )(q, k, v)


def _odd_out_value(h_ref, o_ref, gate_ref, w_ref):
    gate = gate_ref[...].astype(F32)
    x = (o_ref[...].astype(F32) * (gate * _sigmoid(gate))).astype(BF16)
    return h_ref[...] + jnp.dot(x, w_ref[...], preferred_element_type=F32)


def _odd_out_even_in_kernel(h_ref, o_ref, gate_ref, wo_ref, g_ref, w_ref, hout_ref, *out_refs):
    tm = h_ref.shape[0]
    slab = _slab_rows(tm)
    for r in range(0, tm, slab):
        rows = pl.ds(r, slab)
        h = _odd_out_value(h_ref.at[rows], o_ref.at[rows], gate_ref.at[rows], wo_ref)
        hout_ref[rows, :] = h
        _even_in_body(h, g_ref, w_ref, [ref.at[rows] for ref in out_refs])


def _odd_out_even_in(h2, o, gate, wo, g, w):
    M = h2.shape[0]
    tm = _row_tile(M)
    return pl.pallas_call(
        _odd_out_even_in_kernel,
        out_shape=(jax.ShapeDtypeStruct((M, D_MODEL), F32),) + _even_in_outs(M),
        grid=(M // tm,),
        in_specs=[_row_spec(tm, D_MODEL), _row_spec(tm, o.shape[-1]), _row_spec(tm, gate.shape[-1]),
                  _const_spec(wo.shape), _const_spec(g.shape), _const_spec(w.shape)],
        out_specs=(_row_spec(tm, D_MODEL),) + tuple(_row_spec(tm, n) for n in EVEN_OUT_WIDTHS),
        compiler_params=_params(("parallel",)),
        name="odd_out_even_in",
    )(h2, o, gate, wo, g, w)


def _odd_out_last_kernel(h_ref, o_ref, gate_ref, w_ref, fn_ref, out_ref):
    out_ref[...] = _rms(_odd_out_value(h_ref, o_ref, gate_ref, w_ref), fn_ref[...])


def _odd_out_last(h2, o, gate, w, fn, B, Tp, S):
    tm = _pick(S, (1024, 512, 256, 128, 64))
    win = lambda n: pl.BlockSpec((pl.Element(tm), pl.Element(n)),
                                 lambda b, i: (pl.multiple_of(b * Tp + N_META + i * tm, N_META), 0))
    return pl.pallas_call(
        _odd_out_last_kernel,
        out_shape=jax.ShapeDtypeStruct((B * S, D_MODEL), F32),
        grid=(B, S // tm),
        in_specs=[win(D_MODEL), win(o.shape[-1]), win(gate.shape[-1]), _const_spec(w.shape),
                  _const_spec(fn.shape)],
        out_specs=pl.BlockSpec((tm, D_MODEL), lambda b, i: (b * (S // tm) + i, 0)),
        compiler_params=_params(("parallel", "parallel")),
        name="odd_out_last",
    )(h2, o, gate, w, fn)


def _split_cols(w, sizes):
    idx = np.cumsum(sizes)[:-1].tolist()
    return jnp.split(w, idx, axis=-1)


def _pack_even(w_in, gate_a_w, gate_x_w, alpha_w):
    xa, ga, q, k, v, ad, gb = _split_cols(
        w_in, (RNN_WIDTH, RNN_WIDTH, GLA_HEADS * GLA_DK, GLA_HEADS * GLA_DK, GLA_HEADS * GLA_DV,
               GLA_GATE_RANK, GLA_HEADS * GLA_DV))
    ad = jnp.pad(ad, ((0, 0), (0, LANES - GLA_GATE_RANK)))
    w = jnp.concatenate([xa, ga, q, k, v, gb, ad], axis=-1).astype(BF16)
    wg = jnp.concatenate([gate_a_w, gate_x_w], axis=-1).astype(BF16)
    aw = jnp.pad(alpha_w, ((0, LANES - GLA_GATE_RANK), (0, 0))).astype(BF16)
    return w, wg, aw


def _pack_odd(w_in, w_q_up, w_kv_up):
    cq, ckv, kr, gate = _split_cols(w_in, (MLA_Q_RANK, MLA_KV_RANK, MLA_ROPE, MLA_HEADS * MLA_V))
    kr = _head_lanes(jnp.zeros((kr.shape[0], MLA_NOPE), kr.dtype), kr)
    w = jnp.concatenate([cq, ckv, kr, gate], axis=-1).astype(BF16)
    wq = w_q_up.reshape(MLA_Q_RANK, MLA_HEADS, MLA_NOPE + MLA_ROPE)
    wq = _head_lanes(wq[..., :MLA_NOPE], wq[..., MLA_NOPE:]).reshape(MLA_Q_RANK, MLA_HEADS * HEAD_PAD)
    wkv = w_kv_up.reshape(MLA_KV_RANK, MLA_HEADS, MLA_NOPE + MLA_V)
    wk = _head_lanes(wkv[..., :MLA_NOPE], jnp.zeros(wkv.shape[:-1] + (MLA_ROPE,), wkv.dtype))
    wk = wk.reshape(MLA_KV_RANK, MLA_HEADS * HEAD_PAD)
    wv = wkv[..., MLA_NOPE:].reshape(MLA_KV_RANK, MLA_HEADS * MLA_V)
    return w, wq.astype(BF16), wk.astype(BF16), wv.astype(BF16)


def kernel(x, positions, meta_tokens, ab_norm, ab_w_in, ab_conv_w, ab_conv_b, ab_gate_a_w, ab_gate_a_b, ab_gate_x_w, ab_gate_x_b, ab_lru_lambda, ab_alpha_w, ab_alpha_b, ab_gla_norm, ab_w_out, c_norm, c_w_in, c_q_norm, c_w_q_up, c_kv_norm, c_w_kv_up, c_w_out, final_norm):
    B, S, D = x.shape
    T = N_META + S
    Tp = -(-T // LANES) * LANES
    M = B * Tp
    depth = ab_norm.shape[0] + c_norm.shape[0]
    row2 = lambda a: a.reshape(1, -1).astype(F32)


    meta_pos = jnp.broadcast_to(jnp.arange(N_META, dtype=positions.dtype)[None], (B, N_META))
    pos = jnp.concatenate([meta_pos, positions + N_META,
                           jnp.zeros((B, Tp - T), positions.dtype)], axis=1)
    inv_freq = ROPE_BASE ** (-jnp.arange(0, MLA_ROPE, 2, dtype=F32) / MLA_ROPE)
    cos_t, sin_t = _rope_tables(pos.astype(F32).reshape(M, 1), inv_freq)

    if depth % 2 or ab_norm.shape[0] != c_norm.shape[0]:
        raise NotImplementedError("layers must alternate (RG-LRU || GLA), MLA and end on an MLA layer")
    r3 = lambda a: a.reshape(B, Tp, a.shape[-1])
    even_packs = [_pack_even(ab_w_in[j], ab_gate_a_w[j], ab_gate_x_w[j], ab_alpha_w[j])
                  for j in range(depth // 2)]
    h, *even_in = _embed_even_in(x, meta_tokens.astype(x.dtype), row2(ab_norm[0]), even_packs[0][0], Tp)
    for j in range(depth // 2):
        w, wg, aw = even_packs[j]
        xa, ga, q, k, v, gb, ad = even_in
        ya = _rglru(r3(xa), r3(ga), ab_conv_w[j].astype(F32), row2(ab_conv_b[j]), wg,
                    row2(ab_gate_a_b[j]), row2(ab_gate_x_b[j]), row2(ab_lru_lambda[j]))
        ob = _gla(r3(q), r3(k), r3(v), r3(gb), r3(ad), aw, row2(ab_alpha_b[j]), row2(ab_gla_norm[j]))
        w, wq, wk, wv = _pack_odd(c_w_in[j], c_w_q_up[j], c_w_kv_up[j])
        odd_params = (row2(c_norm[j]), w, row2(c_q_norm[j]), wq, row2(c_kv_norm[j]), wk, wv)
        h, q, k, v, gate = _even_out_odd_in(h, ya.reshape(M, -1), ob.reshape(M, -1),
                                            ab_w_out[j].astype(BF16), odd_params, cos_t, sin_t)
        o = _attention(r3(q), r3(k), r3(v)).reshape(M, -1)
        wo = c_w_out[j].astype(BF16)
        if j == depth // 2 - 1:
            return _odd_out_last(h, o, gate, wo, row2(final_norm), B, Tp, S).reshape(B, S, D)
        h, *even_in = _odd_out_even_in(h, o, gate, wo, row2(ab_norm[j + 1]), even_packs[j + 1][0])
```

```python
import functools

import jax
import jax.numpy as jnp
import numpy as np
from jax import lax
from jax.experimental import pallas as pl
from jax.experimental.pallas import tpu as pltpu

F32 = jnp.float32
BF16 = jnp.bfloat16

D_MODEL = 1024
N_META = 16
EPS = 1e-6
RNN_WIDTH = D_MODEL
RNN_BLOCKS = 8
RNN_BLOCK = RNN_WIDTH // RNN_BLOCKS
CONV_WIDTH = 4
RGLRU_C = 8.0
GLA_HEADS = 4
GLA_DK = 128
GLA_DV = 256
GLA_GATE_RANK = 16
GLA_TAU = 16.0
GLA_CHUNK = 64
MLA_HEADS = 16
MLA_NOPE = 64
MLA_ROPE = 32
MLA_V = 64
MLA_Q_RANK = 512
MLA_KV_RANK = 256
ROPE_BASE = 10000.0

LANES = 128
HEAD_PAD = 128
ATT_TILE = 256
SOFTMAX_SLAB = 32
LOG2_E = 1.4426950408889634
NEG_BIG = -1e30
VMEM_LIMIT = 56 * 1024 * 1024


def _pick(n, candidates):
    for c in candidates:
        if n % c == 0:
            return c
    raise ValueError(f"no tile in {candidates} divides {n}")


def _rms(x, g):
    var = jnp.mean(x * x, axis=-1, keepdims=True)
    return x * lax.rsqrt(var + EPS) * g


def _sigmoid(x):
    return 1.0 / (1.0 + jnp.exp2(x * (-LOG2_E)))


def _const_spec(shape):
    nd = len(shape)
    return pl.BlockSpec(shape, lambda *_: (0,) * nd, pipeline_mode=pl.Buffered(1))


def _params(sem):
    return pltpu.CompilerParams(dimension_semantics=sem, vmem_limit_bytes=VMEM_LIMIT)


EVEN_OUT_WIDTHS = (RNN_WIDTH, RNN_WIDTH, GLA_HEADS * GLA_DK, GLA_HEADS * GLA_DK,
                   GLA_HEADS * GLA_DV, GLA_HEADS * GLA_DV, LANES)


def _even_in_body(h, g_ref, w_ref, out_refs):
    xn = _rms(h, g_ref[...]).astype(BF16)
    off = 0
    for ref in out_refs:
        n = ref.shape[-1]
        ref[...] = jnp.dot(xn, w_ref[:, off:off + n], preferred_element_type=F32).astype(ref.dtype)
        off += n


def _embed_even_in_kernel(x_ref, meta_ref, g_ref, w_ref, h_ref, *out_refs, tm, n_tiles, seq, win0):
    k_id = pl.program_id(1)
    for k in range(n_tiles):
        @pl.when(k_id == k)
        def _(k=k):
            t0 = k * tm
            n_meta = max(0, min(N_META - t0, tm))
            x_lo = t0 + n_meta - N_META
            n_x = max(0, min(seq - x_lo, tm - n_meta))
            if n_meta:
                h_ref[0:n_meta, :] = meta_ref[t0:t0 + n_meta, :]
            if n_x:
                off = x_lo - win0[k]
                h_ref[n_meta:n_meta + n_x, :] = x_ref[off:off + n_x, :]
            if n_meta + n_x < tm:
                h_ref[n_meta + n_x:tm, :] = jnp.zeros((tm - n_meta - n_x, h_ref.shape[1]), h_ref.dtype)
    slab = _slab_rows(tm)
    for r in range(0, tm, slab):
        rows = pl.ds(r, slab)
        _even_in_body(h_ref[rows, :], g_ref, w_ref, [ref.at[rows] for ref in out_refs])


def _embed_even_in(x, meta, g, w, Tp):
    B, S, D = x.shape
    tm = _pick(Tp, tuple(c for c in (544, 512, 384, 256, 128) if c <= S))
    n_tiles = Tp // tm
    M = B * Tp
    win0 = tuple(min(max(k * tm - N_META, 0), S - tm) for k in range(n_tiles))

    def x_index(b, k):
        start = jnp.clip(k * tm - N_META, 0, S - tm)
        return (pl.multiple_of(b * S + start, N_META), 0)

    out_spec = lambda n: pl.BlockSpec((tm, n), lambda b, k: (b * n_tiles + k, 0))
    return pl.pallas_call(
        functools.partial(_embed_even_in_kernel, tm=tm, n_tiles=n_tiles, seq=S, win0=win0),
        out_shape=(jax.ShapeDtypeStruct((M, D), F32),) + _even_in_outs(M),
        grid=(B, n_tiles),
        in_specs=[pl.BlockSpec((pl.Element(tm), pl.Element(D)), x_index),
                  _const_spec(meta.shape), _const_spec(g.shape), _const_spec(w.shape)],
        out_specs=(out_spec(D),) + tuple(out_spec(n) for n in EVEN_OUT_WIDTHS),
        compiler_params=_params(("parallel", "parallel")),
        name="embed_even_in",
    )(x.reshape(B * S, D), meta, g, w)


def _slab_rows(tm):
    return next(tm // n for n in (2, 1) if (tm // n) % 16 == 0 and tm % n == 0)


def _row_tile(M):
    return _pick(M, (512, 256, 128))


def _row_spec(tm, n):
    return pl.BlockSpec((tm, n), lambda i: (i, 0))


def _even_in_outs(M):
    return tuple(jax.ShapeDtypeStruct((M, n), BF16) for n in EVEN_OUT_WIDTHS)


def _lru_pitch(tt):
    return tt if (tt // 8) % 2 else tt + 8


def _rglru_kernel(xa_ref, ga_ref, cw_ref, cb_ref, wg_ref, ba_ref, bx_ref, lam_ref, y_ref,
                  flat, u_s, a_s, hist, h_s, *, tt, pitch, nblk):
    B = xa_ref.shape[0]
    t_idx = pl.program_id(1)

    @pl.when(t_idx == 0)
    def _():
        hist[...] = jnp.zeros_like(hist)
        h_s[...] = jnp.zeros_like(h_s)

    for j in range(nblk):
        lanes = slice(j * LANES, (j + 1) * LANES)
        for b in range(B):
            flat[j, b * pitch:b * pitch + tt, :] = xa_ref[b, :, lanes].astype(F32)

    taps = [[cw_ref[k:k + 1, j * LANES:(j + 1) * LANES] for k in range(CONV_WIDTH)] for j in range(nblk)]
    bias = [cb_ref[:, j * LANES:(j + 1) * LANES] for j in range(nblk)]

    def conv_step(t, carry):
        out = []
        for j in range(nblk):
            x1, x2, x3 = carry[j]
            x0 = flat[j, pl.ds(t, B, stride=pitch), :]
            w = taps[j]
            u_s[j, pl.ds(pl.multiple_of(t * B, B), B), :] = (
                bias[j] + w[3] * x0 + w[2] * x1 + w[1] * x2 + w[0] * x3)
            out.append((x0, x1, x2))
        return tuple(out)

    last = lax.fori_loop(0, tt, conv_step,
                         tuple((hist[j, 0], hist[j, 1], hist[j, 2]) for j in range(nblk)), unroll=8)
    for j in range(nblk):
        for k in range(CONV_WIDTH - 1):
            hist[j, k] = last[j][k]

    for j in range(nblk):
        lanes = slice(j * LANES, (j + 1) * LANES)
        y = u_s[j]
        g = jnp.dot(y.astype(BF16), wg_ref[j], preferred_element_type=F32)
        r = 0.5 + 0.5 * jnp.tanh(0.5 * (g[:, :LANES] + ba_ref[:, lanes]))
        i = _sigmoid(g[:, LANES:] + bx_ref[:, lanes])
        lam = lam_ref[:, lanes]
        softplus_neg_lam = jnp.maximum(-lam, 0.0) + jnp.log(1.0 + jnp.exp(-jnp.abs(lam)))
        a = jnp.exp2(r * ((-RGLRU_C * LOG2_E) * softplus_neg_lam))
        x = 1.0 - a * a
        a_s[j] = a
        u_s[j] = (x * lax.rsqrt(jnp.maximum(x, 1e-30))) * (i * y)

    def scan_step(t, hs):
        out = []
        for j in range(nblk):
            rows = pl.ds(pl.multiple_of(t * B, B), B)
            h = a_s[j, rows, :] * hs[j] + u_s[j, rows, :]
            flat[j, pl.ds(t, B, stride=pitch), :] = h
            out.append(h)
        return tuple(out)

    hs = lax.fori_loop(0, tt, scan_step, tuple(h_s[j] for j in range(nblk)), unroll=8)
    for j in range(nblk):
        h_s[j] = hs[j]

    for j in range(nblk):
        lanes = slice(j * LANES, (j + 1) * LANES)
        for b in range(B):
            ga = ga_ref[b, :, lanes].astype(F32)
            y_ref[b, :, lanes] = (flat[j, b * pitch:b * pitch + tt, :]
                                  * (ga * _sigmoid(ga))).astype(y_ref.dtype)


def _rglru(xa, ga, cw, cb, wg, ba, bx, lam):
    B, Tp, C = xa.shape
    tt = _pick(Tp, (272, 256, 192, 128))
    pitch = _lru_pitch(tt)
    cblk = 512
    nblk = cblk // LANES
    blk = pl.BlockSpec((B, tt, cblk), lambda c, t: (0, t, c))
    vec = lambda rows: pl.BlockSpec((rows, cblk), lambda c, t: (0, c))
    return pl.pallas_call(
        functools.partial(_rglru_kernel, tt=tt, pitch=pitch, nblk=nblk),
        out_shape=jax.ShapeDtypeStruct((B, Tp, C), BF16),
        grid=(C // cblk, Tp // tt),
        in_specs=[blk, blk, vec(CONV_WIDTH), vec(1),
                  pl.BlockSpec((nblk, RNN_BLOCK, 2 * RNN_BLOCK), lambda c, t: (c, 0, 0)),
                  vec(1), vec(1), vec(1)],
        out_specs=blk,
        scratch_shapes=[pltpu.VMEM((nblk, B * pitch, LANES), F32),
                        pltpu.VMEM((nblk, tt * B, LANES), F32),
                        pltpu.VMEM((nblk, tt * B, LANES), F32),
                        pltpu.VMEM((nblk, CONV_WIDTH - 1, B, LANES), F32),
                        pltpu.VMEM((nblk, B, LANES), F32)],
        compiler_params=_params(("parallel", "arbitrary")),
        name="rglru",
    )(xa, ga, cw, cb, wg, ba, bx, lam)


def _gla_kernel(q_ref, k_ref, v_ref, gb_ref, ad_ref, aw_ref, ab_ref, gn_ref, o_ref, st_ref, *, nchunk, nb):
    C = GLA_CHUNK

    @pl.when(pl.program_id(1) == 0)
    def _():
        st_ref[...] = jnp.zeros_like(st_ref)

    row = lax.broadcasted_iota(jnp.int32, (C, C), 0)
    col = lax.broadcasted_iota(jnp.int32, (C, C), 1)
    causal = row >= col
    tril = causal.astype(BF16)
    scale = GLA_DK ** -0.5

    def chunk(c, carry):
        rows = pl.ds(pl.multiple_of(c * C, C), C)
        bs = range(nb)
        hs = range(GLA_HEADS)
        kl = [slice(h * GLA_DK, (h + 1) * GLA_DK) for h in hs]
        vl = [slice(h * GLA_DV, (h + 1) * GLA_DV) for h in hs]
        pre = [jnp.dot(ad_ref[b, rows, :], aw_ref[...], preferred_element_type=F32) + ab_ref[...]
               for b in bs]
        la = [(jnp.minimum(p, 0.0) - jnp.log(1.0 + jnp.exp2(jnp.abs(p) * (-LOG2_E)))) * (LOG2_E / GLA_TAU)
              for p in pre]
        la_hi = [x.astype(BF16) for x in la]
        la_lo = [(x - hi.astype(F32)).astype(BF16) for x, hi in zip(la, la_hi)]
        bc = [jnp.dot(tril, hi, preferred_element_type=F32) + jnp.dot(tril, lo, preferred_element_type=F32)
              for hi, lo in zip(la_hi, la_lo)]
        b_last = [x[C - 1:C, :] for x in bc]
        q = [q_ref[b, rows, :].astype(F32) for b in bs]
        k = [k_ref[b, rows, :].astype(F32) for b in bs]
        q_dec = [(q[b] * (jnp.exp2(bc[b]) * scale)).astype(BF16) for b in bs]
        k_inv = [(k[b] * jnp.exp2(-bc[b])).astype(BF16) for b in bs]
        k_end = [(k[b] * jnp.exp2(b_last[b] - bc[b])).astype(BF16) for b in bs]
        decay = [jnp.exp2(x) for x in b_last]
        s = [[jnp.where(causal,
                        lax.dot_general(q_dec[b][:, kl[h]], k_inv[b][:, kl[h]], (((1,), (1,)), ((), ())),
                                        preferred_element_type=F32), 0.0).astype(BF16)
              for h in hs] for b in bs]
        o = [[None] * GLA_HEADS for _ in bs]
        for b in bs:
            for h in hs:
                vh = v_ref[b, rows, vl[h]]
                st = st_ref[b, h]
                o[b][h] = (jnp.dot(s[b][h], vh, preferred_element_type=F32)
                           + lax.dot_general(q_dec[b][:, kl[h]], st.astype(BF16), (((1,), (1,)), ((), ())),
                                             preferred_element_type=F32))
                vk = lax.dot_general(vh, k_end[b][:, kl[h]], (((0,), (0,)), ((), ())),
                                     preferred_element_type=F32)
                st_ref[b, h] = st * decay[b][:, kl[h]] + vk
        for b in bs:
            for h in hs:
                gb = gb_ref[b, rows, vl[h]].astype(F32)
                o_ref[b, rows, vl[h]] = (_rms(o[b][h], gn_ref[...])
                                         * (gb * _sigmoid(gb))).astype(o_ref.dtype)
        return carry

    lax.fori_loop(0, nchunk, chunk, 0)


GLA_BATCH_PER_STEP = 8
GLA_TIME_TILE = 128


def _gla(q, k, v, gb, ad, aw, ab, gn):
    B, Tp, _ = q.shape
    nb = GLA_BATCH_PER_STEP if B % GLA_BATCH_PER_STEP == 0 else 1
    tc = GLA_TIME_TILE
    kw, vw = GLA_HEADS * GLA_DK, GLA_HEADS * GLA_DV
    spec = lambda w: pl.BlockSpec((nb, tc, w), lambda b, t: (b, t, 0))
    return pl.pallas_call(
        functools.partial(_gla_kernel, nchunk=tc // GLA_CHUNK, nb=nb),
        out_shape=jax.ShapeDtypeStruct((B, Tp, vw), BF16),
        grid=(B // nb, Tp // tc),
        in_specs=[spec(kw), spec(kw), spec(vw), spec(vw), spec(LANES),
                  _const_spec(aw.shape), _const_spec(ab.shape), _const_spec(gn.shape)],
        out_specs=spec(vw),
        scratch_shapes=[pltpu.VMEM((nb, GLA_HEADS, GLA_DV, GLA_DK), F32)],
        compiler_params=_params(("parallel", "arbitrary")),
        name="gla",
    )(q, k, v, gb, ad, aw, ab, gn)


def _even_out_value(h_ref, ya_ref, ob_ref, w_ref):
    n = ya_ref.shape[-1]
    return (h_ref[...]
            + jnp.dot(ya_ref[...], w_ref[:n, :], preferred_element_type=F32)
            + jnp.dot(ob_ref[...], w_ref[n:, :], preferred_element_type=F32))


def _even_out_odd_in_kernel(h_ref, ya_ref, ob_ref, wo_ref, *refs):
    hout_ref = refs[-5]
    h = _even_out_value(h_ref, ya_ref, ob_ref, wo_ref)
    hout_ref[...] = h
    _odd_in_body(h, *refs[:-5], *refs[-4:])


def _even_out_odd_in(h2, ya, ob, wo, params, cos_t, sin_t):
    M = h2.shape[0]
    tm = _row_tile(M)
    return pl.pallas_call(
        _even_out_odd_in_kernel,
        out_shape=(jax.ShapeDtypeStruct((M, D_MODEL), F32),) + _odd_in_outs(M),
        grid=(M // tm,),
        in_specs=[_row_spec(tm, D_MODEL), _row_spec(tm, ya.shape[-1]), _row_spec(tm, ob.shape[-1]),
                  _const_spec(wo.shape)] + _odd_in_specs(tm, params),
        out_specs=(_row_spec(tm, D_MODEL),) + tuple(_row_spec(tm, n) for n in ODD_IN_WIDTHS),
        compiler_params=_params(("parallel",)),
        name="even_out_odd_in",
    )(h2, ya, ob, wo, *params, cos_t, sin_t)


ROPE_HALF = MLA_ROPE // 2
ROPE_X2_LANE = HEAD_PAD // 2


def _head_lanes(nope, rope_part):
    zeros = lambda n: jnp.zeros(nope.shape[:-1] + (n,), nope.dtype)
    n1 = ROPE_X2_LANE - ROPE_HALF
    return jnp.concatenate([rope_part[..., :ROPE_HALF], nope[..., :n1], rope_part[..., ROPE_HALF:],
                            nope[..., n1:], zeros(HEAD_PAD - MLA_NOPE - MLA_ROPE)], axis=-1)


ROPE_GROUPS = LANES // MLA_ROPE


def _rope_table_kernel(pos_ref, freq_ref, phase_ref, c_ref, s_ref):
    tm = pos_ref.shape[0]
    q = tm // ROPE_GROUPS
    lane = lax.broadcasted_iota(jnp.int32, (q, LANES), 1)
    pos = pos_ref[0:q, :]
    for i in range(1, ROPE_GROUPS):
        pos = jnp.where(lane >= i * MLA_ROPE, pos_ref[i * q:(i + 1) * q, :], pos)
    t = jnp.cos(pos * freq_ref[...] - phase_ref[...])
    x1 = lane < ROPE_HALF
    x2 = (lane >= ROPE_X2_LANE) & (lane < ROPE_X2_LANE + ROPE_HALF)
    for i in range(ROPE_GROUPS):
        ti = pltpu.roll(t, (LANES - i * MLA_ROPE) % LANES, axis=1) if i else t
        cos_x2 = pltpu.roll(ti, ROPE_X2_LANE, axis=1)
        sin_x1 = pltpu.roll(ti, LANES - ROPE_HALF, axis=1)
        sin_x2 = pltpu.roll(ti, ROPE_X2_LANE - ROPE_HALF, axis=1)
        rows = slice(i * q, (i + 1) * q)
        c_ref[rows, :] = jnp.where(x1, ti, jnp.where(x2, cos_x2, 1.0))
        s_ref[rows, :] = jnp.where(x1, -sin_x1, jnp.where(x2, sin_x2, 0.0))


def _rope_tables(posf, inv_freq):
    M = posf.shape[0]
    tm = _pick(M, (1088, 1024, 512, 256, 128))
    out = jax.ShapeDtypeStruct((M, HEAD_PAD), F32)
    zeros = jnp.zeros((ROPE_HALF,), F32)
    freq = jnp.tile(jnp.concatenate([inv_freq, inv_freq]), ROPE_GROUPS).reshape(1, LANES)
    phase = jnp.tile(jnp.concatenate([zeros, zeros + np.pi / 2]), ROPE_GROUPS).reshape(1, LANES)
    return pl.pallas_call(
        _rope_table_kernel,
        out_shape=(out, out),
        grid=(M // tm,),
        in_specs=[pl.BlockSpec((tm, 1), lambda i: (i, 0)), _const_spec((1, LANES)), _const_spec((1, LANES))],
        out_specs=(pl.BlockSpec((tm, HEAD_PAD), lambda i: (i, 0)),) * 2,
        compiler_params=_params(("parallel",)),
        name="rope_tables",
    )(posf, freq, phase)


def _rope(x, cos_t, sin_t):
    heads = [pltpu.roll(x[:, c:c + HEAD_PAD], ROPE_X2_LANE, axis=1) for c in range(0, x.shape[1], HEAD_PAD)]
    partner = heads[0] if len(heads) == 1 else jnp.concatenate(heads, axis=1)
    return x * cos_t + partner * sin_t


def _odd_in_body(h, g_ref, w_ref, qn_ref, wq_ref, kvn_ref, wk_ref, wv_ref, cos_ref, sin_ref,
                 q_ref, k_ref, v_ref, gate_ref):
    xn = _rms(h, g_ref[...]).astype(BF16)
    c0, c1, c2 = MLA_Q_RANK, MLA_Q_RANK + MLA_KV_RANK, MLA_Q_RANK + MLA_KV_RANK + HEAD_PAD
    gate_ref[...] = jnp.dot(xn, w_ref[:, c2:], preferred_element_type=F32).astype(gate_ref.dtype)
    cq = jnp.dot(xn, w_ref[:, :c0], preferred_element_type=F32)
    ckv = jnp.dot(xn, w_ref[:, c0:c1], preferred_element_type=F32)
    kr = jnp.dot(xn, w_ref[:, c1:c2], preferred_element_type=F32)
    cos_t, sin_t = cos_ref[...], sin_ref[...]
    kr = _rope(kr, cos_t, sin_t)
    cqn = _rms(cq, qn_ref[...]).astype(BF16)
    ckvn = _rms(ckv, kvn_ref[...]).astype(BF16)
    v_ref[...] = jnp.dot(ckvn, wv_ref[...], preferred_element_type=F32).astype(v_ref.dtype)
    scale = LOG2_E * (MLA_NOPE + MLA_ROPE) ** -0.5
    cos2 = jnp.concatenate([cos_t, cos_t], axis=1) * scale
    sin2 = jnp.concatenate([sin_t, sin_t], axis=1) * scale
    kr2 = jnp.concatenate([kr, kr], axis=1)
    for hp in range(MLA_HEADS // 2):
        lanes = slice(hp * 2 * HEAD_PAD, (hp + 1) * 2 * HEAD_PAD)
        qh = jnp.dot(cqn, wq_ref[:, lanes], preferred_element_type=F32)
        q_ref[:, lanes] = _rope(qh, cos2, sin2).astype(q_ref.dtype)
        kh = jnp.dot(ckvn, wk_ref[:, lanes], preferred_element_type=F32)
        k_ref[:, lanes] = (kh + kr2).astype(k_ref.dtype)


ODD_IN_WIDTHS = (MLA_HEADS * HEAD_PAD, MLA_HEADS * HEAD_PAD, MLA_HEADS * MLA_V, MLA_HEADS * MLA_V)


def _odd_in_outs(M):
    return tuple(jax.ShapeDtypeStruct((M, n), BF16) for n in ODD_IN_WIDTHS)


def _odd_in_specs(tm, params):
    return [_const_spec(p.shape) for p in params] + [_row_spec(tm, HEAD_PAD), _row_spec(tm, HEAD_PAD)]


def _attn_kernel(q_ref, k_ref, v_ref, o_ref, k_scr, vt_scr, qt_scr, bias_scr, s_a, s_b, mx_a, mx_b,
                 m_scr, acc_scr, o_scr, *, nh, n_q, seq):
    T = ATT_TILE
    Tp = q_ref.shape[1]
    M0 = N_META
    Tp2 = n_q * T
    SL = SOFTMAX_SLAB
    VE = ATT_V_EXT

    k_scr[0:seq, :] = k_ref[0, M0:M0 + seq, :]
    if Tp2 > seq:
        k_scr[seq:Tp2, :] = jnp.zeros((Tp2 - seq, k_scr.shape[1]), k_scr.dtype)
    ones_row = (lax.broadcasted_iota(jnp.int32, (VE - MLA_V, Tp2), 0) == 0).astype(vt_scr.dtype)
    for h in range(nh):
        vt_scr[h * VE + MLA_V:(h + 1) * VE, :] = ones_row
        if Tp2 > seq:
            vt_scr[h * VE:h * VE + MLA_V, seq:Tp2] = jnp.zeros((MLA_V, Tp2 - seq), vt_scr.dtype)
    for c in range(0, seq, LANES):
        w = min(LANES, seq - c)
        vt = v_ref[0, M0 + c:M0 + c + w, :].T
        for h in range(nh):
            vt_scr[h * VE:h * VE + MLA_V, c:c + w] = vt[h * MLA_V:(h + 1) * MLA_V, :]
    for j in range(n_q):
        rows = min(T, seq - j * T)
        for h in range(nh):
            qh = q_ref[0, M0 + j * T:M0 + j * T + rows, h * HEAD_PAD:(h + 1) * HEAD_PAD]
            qt_scr[j, h, :, 0:rows] = qh.T
            if rows < T:
                qt_scr[j, h, :, rows:T] = jnp.zeros((HEAD_PAD, T - rows), qt_scr.dtype)
    kk = lax.broadcasted_iota(jnp.int32, (T, T), 0)
    qq = lax.broadcasted_iota(jnp.int32, (T, T), 1)
    bias_scr[...] = jnp.where(kk <= qq, 0.0, NEG_BIG)

    def qk(buf, j, i, diag=False, heads=range(nh)):
        s_ref, mx_ref = buf
        koff = pl.multiple_of(i * T, T)
        for h in heads:
            sv = jnp.dot(k_scr[pl.ds(koff, T), h * HEAD_PAD:(h + 1) * HEAD_PAD],
                         qt_scr[j, h], preferred_element_type=F32)
            if diag:
                sv = sv + bias_scr[...]
            s_ref[h] = sv
            parts = [None] * 4
            for n, r in enumerate(range(0, T, SL)):
                blk = sv[r:r + SL, :]
                parts[n % 4] = blk if parts[n % 4] is None else jnp.maximum(parts[n % 4], blk)
            mt = jnp.maximum(jnp.maximum(parts[0], parts[1]), jnp.maximum(parts[2], parts[3]))
            mx_ref[h] = jnp.max(mt, axis=0, keepdims=True)

    def softmax_pv(buf, j, i, diag, heads=range(nh)):
        s_ref, mx_ref = buf
        koff = pl.multiple_of(i * T, T)
        for h in heads:
            if diag:
                m_new = mx_ref[h]
            else:
                m_old = m_scr[j, h]
                m_new = jnp.maximum(m_old, mx_ref[h])
                alpha = jnp.exp2(m_old - m_new)
            m_scr[j, h] = m_new
            p = jnp.concatenate([jnp.exp2((s_ref[h, r:r + SL, :] - m_new).astype(BF16))
                                 for r in range(0, T, SL)], axis=0)
            rows = slice(h * VE, (h + 1) * VE)
            pv = jnp.dot(vt_scr[rows, pl.ds(koff, T)], p, preferred_element_type=F32)
            acc_scr[j, rows, :] = pv if diag else alpha * acc_scr[j, rows, :] + pv

    bufs = ((s_a, mx_a), (s_b, mx_b))
    G = ATT_STEPS_PER_ITER

    def run(n_steps, first, advance, diag):
        def steps(count, jt):
            for t in range(count):
                nxt = advance(*jt)
                for h in range(nh):
                    qk(bufs[(t + 1) % 2], jnp.minimum(nxt[0], n_q - 1), nxt[1], diag, (h,))
                    softmax_pv(bufs[t % 2], jt[0], jt[1], diag, (h,))
                jt = nxt
            return jt

        qk(bufs[0], first[0], first[1], diag)
        jt = lax.fori_loop(0, n_steps // G, lambda n, c: steps(G, c),
                           (jnp.int32(first[0]), jnp.int32(first[1])))
        steps(n_steps % G, jt)

    run(n_q, (0, 0), lambda j, i: (j + 1, jnp.minimum(i + 1, n_q - 1)), True)

    def advance(j, i):
        last = i + 1 >= j
        return jnp.where(last, j + 1, j), jnp.where(last, 0, i + 1)

    if n_q > 1:
        run(n_q * (n_q - 1) // 2, (1, 0), advance, False)

    def finish(j, c):
        hs = range(nh)
        sm = [jnp.dot(k_ref[0, 0:M0, h * HEAD_PAD:(h + 1) * HEAD_PAD], qt_scr[j, h],
                      preferred_element_type=F32) for h in hs]
        m_old = [m_scr[j, h] for h in hs]
        m_new = [jnp.maximum(m_old[h], jnp.max(sm[h], axis=0, keepdims=True)) for h in hs]
        alpha = [jnp.exp2(m_old[h] - m_new[h]) for h in hs]
        pm = [jnp.exp2((sm[h] - m_new[h]).astype(BF16)) for h in hs]
        pv = [lax.dot_general(v_ref[0, 0:M0, h * MLA_V:(h + 1) * MLA_V], pm[h], (((0,), (0,)), ((), ())),
                              preferred_element_type=F32) for h in hs]
        outs = []
        for h in hs:
            num = alpha[h] * acc_scr[j, h * VE:h * VE + MLA_V, :] + pv[h]
            den = (alpha[h] * acc_scr[j, h * VE + MLA_V:h * VE + MLA_V + 1, :]
                   + jnp.sum(pm[h].astype(F32), axis=0, keepdims=True))
            outs.append(num / den)
        o_scr[pl.ds(pl.multiple_of(j * T, T), T), :] = jnp.concatenate(outs, axis=0).T.astype(o_scr.dtype)
        return c

    lax.fori_loop(0, n_q, finish, 0, unroll=2)
    o_ref[0, M0:M0 + seq, :] = o_scr[0:seq, :]

    qi = lax.broadcasted_iota(jnp.int32, (M0, M0), 0)
    ki = lax.broadcasted_iota(jnp.int32, (M0, M0), 1)
    outs = []
    for h in range(nh):
        lanes = slice(h * HEAD_PAD, (h + 1) * HEAD_PAD)
        s = lax.dot_general(q_ref[0, 0:M0, lanes], k_ref[0, 0:M0, lanes], (((1,), (1,)), ((), ())),
                            preferred_element_type=F32)
        s = jnp.where(ki <= qi, s, NEG_BIG)
        p = jnp.exp2(s - jnp.max(s, axis=1, keepdims=True))
        den = jnp.sum(p, axis=1, keepdims=True)
        pv = jnp.dot(p.astype(BF16), v_ref[0, 0:M0, h * MLA_V:(h + 1) * MLA_V], preferred_element_type=F32)
        outs.append(pv / den)
    o_ref[0, 0:M0, :] = jnp.concatenate(outs, axis=1).astype(o_ref.dtype)
    if Tp > M0 + seq:
        o_ref[0, M0 + seq:Tp, :] = jnp.zeros((Tp - M0 - seq, o_ref.shape[2]), o_ref.dtype)


ATT_HEADS_PER_STEP = 8
ATT_STEPS_PER_ITER = 12
ATT_V_EXT = MLA_V + 16


def _attention(q, k, v, seq):
    B, Tp, _ = q.shape
    nh = ATT_HEADS_PER_STEP
    T = ATT_TILE
    n_q = -(-seq // T)
    Tp2 = n_q * T
    qk_spec = pl.BlockSpec((1, Tp, nh * HEAD_PAD), lambda b, p: (b, 0, p))
    v_spec = pl.BlockSpec((1, Tp, nh * MLA_V), lambda b, p: (b, 0, p))
    return pl.pallas_call(
        functools.partial(_attn_kernel, nh=nh, n_q=n_q, seq=seq),
        out_shape=jax.ShapeDtypeStruct((B, Tp, MLA_HEADS * MLA_V), BF16),
        grid=(B, MLA_HEADS // nh),
        in_specs=[qk_spec, qk_spec, v_spec],
        out_specs=v_spec,
        scratch_shapes=[pltpu.VMEM((Tp2, nh * HEAD_PAD), BF16),
                        pltpu.VMEM((nh * ATT_V_EXT, Tp2), BF16),
                        pltpu.VMEM((n_q, nh, HEAD_PAD, T), BF16),
                        pltpu.VMEM((T, T), F32),
                        pltpu.VMEM((nh, T, T), F32),
                        pltpu.VMEM((nh, T, T), F32),
                        pltpu.VMEM((nh, 1, T), F32),
                        pltpu.VMEM((nh, 1, T), F32),
                        pltpu.VMEM((n_q, nh, 1, T), F32),
                        pltpu.VMEM((n_q, nh * ATT_V_EXT, T), F32),
                        pltpu.VMEM((Tp2, nh * MLA_V), BF16)],
        compiler_params=_params(("parallel", "parallel")),
        name="mla_attention",
    )(q, k, v)


def _odd_out_value(h_ref, o_ref, gate_ref, w_ref):
    gate = gate_ref[...].astype(F32)
    x = (o_ref[...].astype(F32) * (gate * _sigmoid(gate))).astype(BF16)
    return h_ref[...] + jnp.dot(x, w_ref[...], preferred_element_type=F32)


def _odd_out_even_in_kernel(h_ref, o_ref, gate_ref, wo_ref, g_ref, w_ref, hout_ref, *out_refs):
    tm = h_ref.shape[0]
    slab = _slab_rows(tm)
    for r in range(0, tm, slab):
        rows = pl.ds(r, slab)
        h = _odd_out_value(h_ref.at[rows], o_ref.at[rows], gate_ref.at[rows], wo_ref)
        hout_ref[rows, :] = h
        _even_in_body(h, g_ref, w_ref, [ref.at[rows] for ref in out_refs])


def _odd_out_even_in(h2, o, gate, wo, g, w):
    M = h2.shape[0]
    tm = _row_tile(M)
    return pl.pallas_call(
        _odd_out_even_in_kernel,
        out_shape=(jax.ShapeDtypeStruct((M, D_MODEL), F32),) + _even_in_outs(M),
        grid=(M // tm,),
        in_specs=[_row_spec(tm, D_MODEL), _row_spec(tm, o.shape[-1]), _row_spec(tm, gate.shape[-1]),
                  _const_spec(wo.shape), _const_spec(g.shape), _const_spec(w.shape)],
        out_specs=(_row_spec(tm, D_MODEL),) + tuple(_row_spec(tm, n) for n in EVEN_OUT_WIDTHS),
        compiler_params=_params(("parallel",)),
        name="odd_out_even_in",
    )(h2, o, gate, wo, g, w)


def _odd_out_last_kernel(h_ref, o_ref, gate_ref, w_ref, fn_ref, out_ref):
    out_ref[...] = _rms(_odd_out_value(h_ref, o_ref, gate_ref, w_ref), fn_ref[...])


def _odd_out_last(h2, o, gate, w, fn, B, Tp, S):
    tm = _pick(S, (1024, 512, 256, 128, 64))
    win = lambda n: pl.BlockSpec((pl.Element(tm), pl.Element(n)),
                                 lambda b, i: (pl.multiple_of(b * Tp + N_META + i * tm, N_META), 0))
    return pl.pallas_call(
        _odd_out_last_kernel,
        out_shape=jax.ShapeDtypeStruct((B * S, D_MODEL), F32),
        grid=(B, S // tm),
        in_specs=[win(D_MODEL), win(o.shape[-1]), win(gate.shape[-1]), _const_spec(w.shape),
                  _const_spec(fn.shape)],
        out_specs=pl.BlockSpec((tm, D_MODEL), lambda b, i: (b * (S // tm) + i, 0)),
        compiler_params=_params(("parallel", "parallel")),
        name="odd_out_last",
    )(h2, o, gate, w, fn)


def _split_cols(w, sizes):
    idx = np.cumsum(sizes)[:-1].tolist()
    return jnp.split(w, idx, axis=-1)


def _pack_even(w_in, gate_a_w, gate_x_w, alpha_w):
    xa, ga, q, k, v, ad, gb = _split_cols(
        w_in, (RNN_WIDTH, RNN_WIDTH, GLA_HEADS * GLA_DK, GLA_HEADS * GLA_DK, GLA_HEADS * GLA_DV,
               GLA_GATE_RANK, GLA_HEADS * GLA_DV))
    ad = jnp.pad(ad, ((0, 0), (0, LANES - GLA_GATE_RANK)))
    w = jnp.concatenate([xa, ga, q, k, v, gb, ad], axis=-1).astype(BF16)
    wg = jnp.concatenate([gate_a_w, gate_x_w], axis=-1).astype(BF16)
    aw = jnp.pad(alpha_w, ((0, LANES - GLA_GATE_RANK), (0, 0))).astype(BF16)
    return w, wg, aw


def _pack_odd(w_in, w_q_up, w_kv_up):
    cq, ckv, kr, gate = _split_cols(w_in, (MLA_Q_RANK, MLA_KV_RANK, MLA_ROPE, MLA_HEADS * MLA_V))
    kr = _head_lanes(jnp.zeros((kr.shape[0], MLA_NOPE), kr.dtype), kr)
    w = jnp.concatenate([cq, ckv, kr, gate], axis=-1).astype(BF16)
    wq = w_q_up.reshape(MLA_Q_RANK, MLA_HEADS, MLA_NOPE + MLA_ROPE)
    wq = _head_lanes(wq[..., :MLA_NOPE], wq[..., MLA_NOPE:]).reshape(MLA_Q_RANK, MLA_HEADS * HEAD_PAD)
    wkv = w_kv_up.reshape(MLA_KV_RANK, MLA_HEADS, MLA_NOPE + MLA_V)
    wk = _head_lanes(wkv[..., :MLA_NOPE], jnp.zeros(wkv.shape[:-1] + (MLA_ROPE,), wkv.dtype))
    wk = wk.reshape(MLA_KV_RANK, MLA_HEADS * HEAD_PAD)
    wv = wkv[..., MLA_NOPE:].reshape(MLA_KV_RANK, MLA_HEADS * MLA_V)
    return w, wq.astype(BF16), wk.astype(BF16), wv.astype(BF16)


def kernel(x, positions, meta_tokens, ab_norm, ab_w_in, ab_conv_w, ab_conv_b, ab_gate_a_w, ab_gate_a_b, ab_gate_x_w, ab_gate_x_b, ab_lru_lambda, ab_alpha_w, ab_alpha_b, ab_gla_norm, ab_w_out, c_norm, c_w_in, c_q_norm, c_w_q_up, c_kv_norm, c_w_kv_up, c_w_out, final_norm):
    B, S, D = x.shape
    T = N_META + S
    Tp = -(-T // LANES) * LANES
    M = B * Tp
    depth = ab_norm.shape[0] + c_norm.shape[0]
    row2 = lambda a: a.reshape(1, -1).astype(F32)


    meta_pos = jnp.broadcast_to(jnp.arange(N_META, dtype=positions.dtype)[None], (B, N_META))
    pos = jnp.concatenate([meta_pos, positions + N_META,
                           jnp.zeros((B, Tp - T), positions.dtype)], axis=1)
    inv_freq = ROPE_BASE ** (-jnp.arange(0, MLA_ROPE, 2, dtype=F32) / MLA_ROPE)
    cos_t, sin_t = _rope_tables(pos.astype(F32).reshape(M, 1), inv_freq)

    if depth % 2 or ab_norm.shape[0] != c_norm.shape[0]:
        raise NotImplementedError("layers must alternate (RG-LRU || GLA), MLA and end on an MLA layer")
    r3 = lambda a: a.reshape(B, Tp, a.shape[-1])
    even_packs = [_pack_even(ab_w_in[j], ab_gate_a_w[j], ab_gate_x_w[j], ab_alpha_w[j])
                  for j in range(depth // 2)]
    h, *even_in = _embed_even_in(x, meta_tokens.astype(x.dtype), row2(ab_norm[0]), even_packs[0][0], Tp)
    for j in range(depth // 2):
        w, wg, aw = even_packs[j]
        xa, ga, q, k, v, gb, ad = even_in
        ya = _rglru(r3(xa), r3(ga), ab_conv_w[j].astype(F32), row2(ab_conv_b[j]), wg,
                    row2(ab_gate_a_b[j]), row2(ab_gate_x_b[j]), row2(ab_lru_lambda[j]))
        ob = _gla(r3(q), r3(k), r3(v), r3(gb), r3(ad), aw, row2(ab_alpha_b[j]), row2(ab_gla_norm[j]))
        w, wq, wk, wv = _pack_odd(c_w_in[j], c_w_q_up[j], c_w_kv_up[j])
        odd_params = (row2(c_norm[j]), w, row2(c_q_norm[j]), wq, row2(c_kv_norm[j]), wk, wv)
        h, q, k, v, gate = _even_out_odd_in(h, ya.reshape(M, -1), ob.reshape(M, -1),
                                            ab_w_out[j].astype(BF16), odd_params, cos_t, sin_t)
        o = _attention(r3(q), r3(k), r3(v), S).reshape(M, -1)
        wo = c_w_out[j].astype(BF16)
        if j == depth // 2 - 1:
            return _odd_out_last(h, o, gate, wo, row2(final_norm), B, Tp, S).reshape(B, S, D)
        h, *even_in = _odd_out_even_in(h, o, gate, wo, row2(ab_norm[j + 1]), even_packs[j + 1][0])
```

```python
import functools

import jax
import jax.numpy as jnp
import numpy as np
from jax import lax
from jax.experimental import pallas as pl
from jax.experimental.pallas import tpu as pltpu

F32 = jnp.float32
BF16 = jnp.bfloat16

D_MODEL = 1024
N_META = 16
EPS = 1e-6
RNN_WIDTH = D_MODEL
RNN_BLOCKS = 8
RNN_BLOCK = RNN_WIDTH // RNN_BLOCKS
CONV_WIDTH = 4
RGLRU_C = 8.0
GLA_HEADS = 4
GLA_DK = 128
GLA_DV = 256
GLA_GATE_RANK = 16
GLA_TAU = 16.0
GLA_CHUNK = 64
MLA_HEADS = 16
MLA_NOPE = 64
MLA_ROPE = 32
MLA_V = 64
MLA_Q_RANK = 512
MLA_KV_RANK = 256
ROPE_BASE = 10000.0

LANES = 128
HEAD_PAD = 128
ATT_TILE = 256
SOFTMAX_SLAB = 32
LOG2_E = 1.4426950408889634
NEG_BIG = -1e30
VMEM_LIMIT = 56 * 1024 * 1024


def _pick(n, candidates):
    for c in candidates:
        if n % c == 0:
            return c
    raise ValueError(f"no tile in {candidates} divides {n}")


def _rms(x, g):
    var = jnp.mean(x * x, axis=-1, keepdims=True)
    return x * lax.rsqrt(var + EPS) * g


def _sigmoid(x):
    return 1.0 / (1.0 + jnp.exp2(x * (-LOG2_E)))


def _const_spec(shape):
    nd = len(shape)
    return pl.BlockSpec(shape, lambda *_: (0,) * nd, pipeline_mode=pl.Buffered(1))


def _params(sem):
    return pltpu.CompilerParams(dimension_semantics=sem, vmem_limit_bytes=VMEM_LIMIT)


EVEN_OUT_WIDTHS = (RNN_WIDTH, RNN_WIDTH, GLA_HEADS * GLA_DK, GLA_HEADS * GLA_DK,
                   GLA_HEADS * GLA_DV, GLA_HEADS * GLA_DV, LANES)


def _even_in_body(h, g_ref, w_ref, out_refs):
    xn = _rms(h, g_ref[...]).astype(BF16)
    off = 0
    for ref in out_refs:
        n = ref.shape[-1]
        ref[...] = jnp.dot(xn, w_ref[:, off:off + n], preferred_element_type=F32).astype(ref.dtype)
        off += n


def _embed_even_in_kernel(x_ref, meta_ref, g_ref, w_ref, h_ref, *out_refs, tm, n_tiles, seq, win0):
    k_id = pl.program_id(1)
    for k in range(n_tiles):
        @pl.when(k_id == k)
        def _(k=k):
            t0 = k * tm
            n_meta = max(0, min(N_META - t0, tm))
            x_lo = t0 + n_meta - N_META
            n_x = max(0, min(seq - x_lo, tm - n_meta))
            if n_meta:
                h_ref[0:n_meta, :] = meta_ref[t0:t0 + n_meta, :]
            if n_x:
                off = x_lo - win0[k]
                h_ref[n_meta:n_meta + n_x, :] = x_ref[off:off + n_x, :]
            if n_meta + n_x < tm:
                h_ref[n_meta + n_x:tm, :] = jnp.zeros((tm - n_meta - n_x, h_ref.shape[1]), h_ref.dtype)
    slab = _slab_rows(tm)
    for r in range(0, tm, slab):
        rows = pl.ds(r, slab)
        _even_in_body(h_ref[rows, :], g_ref, w_ref, [ref.at[rows] for ref in out_refs])


def _embed_even_in(x, meta, g, w, Tp):
    B, S, D = x.shape
    tm = _pick(Tp, tuple(c for c in (544, 512, 384, 256, 128) if c <= S))
    n_tiles = Tp // tm
    M = B * Tp
    win0 = tuple(min(max(k * tm - N_META, 0), S - tm) for k in range(n_tiles))

    def x_index(b, k):
        start = jnp.clip(k * tm - N_META, 0, S - tm)
        return (pl.multiple_of(b * S + start, N_META), 0)

    out_spec = lambda n: pl.BlockSpec((tm, n), lambda b, k: (b * n_tiles + k, 0))
    return pl.pallas_call(
        functools.partial(_embed_even_in_kernel, tm=tm, n_tiles=n_tiles, seq=S, win0=win0),
        out_shape=(jax.ShapeDtypeStruct((M, D), F32),) + _even_in_outs(M),
        grid=(B, n_tiles),
        in_specs=[pl.BlockSpec((pl.Element(tm), pl.Element(D)), x_index),
                  _const_spec(meta.shape), _const_spec(g.shape), _const_spec(w.shape)],
        out_specs=(out_spec(D),) + tuple(out_spec(n) for n in EVEN_OUT_WIDTHS),
        compiler_params=_params(("parallel", "parallel")),
        name="embed_even_in",
    )(x.reshape(B * S, D), meta, g, w)


def _slab_rows(tm):
    return next(tm // n for n in (2, 1) if (tm // n) % 16 == 0 and tm % n == 0)


def _row_tile(M):
    return _pick(M, (512, 256, 128))


def _row_spec(tm, n):
    return pl.BlockSpec((tm, n), lambda i: (i, 0))


def _even_in_outs(M):
    return tuple(jax.ShapeDtypeStruct((M, n), BF16) for n in EVEN_OUT_WIDTHS)


def _lru_pitch(tt):
    return tt if (tt // 8) % 2 else tt + 8


def _rglru_kernel(xa_ref, ga_ref, cw_ref, cb_ref, wg_ref, ba_ref, bx_ref, lam_ref, y_ref,
                  flat, u_s, a_s, hist, h_s, *, tt, pitch, nblk):
    B = xa_ref.shape[0]
    t_idx = pl.program_id(1)

    @pl.when(t_idx == 0)
    def _():
        hist[...] = jnp.zeros_like(hist)
        h_s[...] = jnp.zeros_like(h_s)

    for j in range(nblk):
        lanes = slice(j * LANES, (j + 1) * LANES)
        for b in range(B):
            flat[j, b * pitch:b * pitch + tt, :] = xa_ref[b, :, lanes].astype(F32)

    taps = [[cw_ref[k:k + 1, j * LANES:(j + 1) * LANES] for k in range(CONV_WIDTH)] for j in range(nblk)]
    bias = [cb_ref[:, j * LANES:(j + 1) * LANES] for j in range(nblk)]

    def conv_step(t, carry):
        out = []
        for j in range(nblk):
            x1, x2, x3 = carry[j]
            x0 = flat[j, pl.ds(t, B, stride=pitch), :]
            w = taps[j]
            u_s[j, pl.ds(pl.multiple_of(t * B, B), B), :] = (
                bias[j] + w[3] * x0 + w[2] * x1 + w[1] * x2 + w[0] * x3)
            out.append((x0, x1, x2))
        return tuple(out)

    last = lax.fori_loop(0, tt, conv_step,
                         tuple((hist[j, 0], hist[j, 1], hist[j, 2]) for j in range(nblk)), unroll=8)
    for j in range(nblk):
        for k in range(CONV_WIDTH - 1):
            hist[j, k] = last[j][k]

    for j in range(nblk):
        lanes = slice(j * LANES, (j + 1) * LANES)
        y = u_s[j]
        g = jnp.dot(y.astype(BF16), wg_ref[j], preferred_element_type=F32)
        r = 0.5 + 0.5 * jnp.tanh(0.5 * (g[:, :LANES] + ba_ref[:, lanes]))
        i = _sigmoid(g[:, LANES:] + bx_ref[:, lanes])
        lam = lam_ref[:, lanes]
        softplus_neg_lam = jnp.maximum(-lam, 0.0) + jnp.log(1.0 + jnp.exp(-jnp.abs(lam)))
        a = jnp.exp2(r * ((-RGLRU_C * LOG2_E) * softplus_neg_lam))
        x = 1.0 - a * a
        a_s[j] = a
        u_s[j] = (x * lax.rsqrt(jnp.maximum(x, 1e-30))) * (i * y)

    def scan_step(t, hs):
        out = []
        for j in range(nblk):
            rows = pl.ds(pl.multiple_of(t * B, B), B)
            h = a_s[j, rows, :] * hs[j] + u_s[j, rows, :]
            flat[j, pl.ds(t, B, stride=pitch), :] = h
            out.append(h)
        return tuple(out)

    hs = lax.fori_loop(0, tt, scan_step, tuple(h_s[j] for j in range(nblk)), unroll=8)
    for j in range(nblk):
        h_s[j] = hs[j]

    for j in range(nblk):
        lanes = slice(j * LANES, (j + 1) * LANES)
        for b in range(B):
            ga = ga_ref[b, :, lanes].astype(F32)
            y_ref[b, :, lanes] = (flat[j, b * pitch:b * pitch + tt, :]
                                  * (ga * _sigmoid(ga))).astype(y_ref.dtype)


def _rglru(xa, ga, cw, cb, wg, ba, bx, lam):
    B, Tp, C = xa.shape
    tt = _pick(Tp, (272, 256, 192, 128))
    pitch = _lru_pitch(tt)
    cblk = 512
    nblk = cblk // LANES
    blk = pl.BlockSpec((B, tt, cblk), lambda c, t: (0, t, c))
    vec = lambda rows: pl.BlockSpec((rows, cblk), lambda c, t: (0, c))
    return pl.pallas_call(
        functools.partial(_rglru_kernel, tt=tt, pitch=pitch, nblk=nblk),
        out_shape=jax.ShapeDtypeStruct((B, Tp, C), BF16),
        grid=(C // cblk, Tp // tt),
        in_specs=[blk, blk, vec(CONV_WIDTH), vec(1),
                  pl.BlockSpec((nblk, RNN_BLOCK, 2 * RNN_BLOCK), lambda c, t: (c, 0, 0)),
                  vec(1), vec(1), vec(1)],
        out_specs=blk,
        scratch_shapes=[pltpu.VMEM((nblk, B * pitch, LANES), F32),
                        pltpu.VMEM((nblk, tt * B, LANES), F32),
                        pltpu.VMEM((nblk, tt * B, LANES), F32),
                        pltpu.VMEM((nblk, CONV_WIDTH - 1, B, LANES), F32),
                        pltpu.VMEM((nblk, B, LANES), F32)],
        compiler_params=_params(("parallel", "arbitrary")),
        name="rglru",
    )(xa, ga, cw, cb, wg, ba, bx, lam)


def _gla_kernel(q_ref, k_ref, v_ref, gb_ref, ad_ref, aw_ref, ab_ref, gn_ref, o_ref, st_ref, *, nchunk, nb):
    C = GLA_CHUNK

    @pl.when(pl.program_id(1) == 0)
    def _():
        st_ref[...] = jnp.zeros_like(st_ref)

    row = lax.broadcasted_iota(jnp.int32, (C, C), 0)
    col = lax.broadcasted_iota(jnp.int32, (C, C), 1)
    causal = row >= col
    tril = causal.astype(BF16)
    scale = GLA_DK ** -0.5

    def chunk(c, carry):
        rows = pl.ds(pl.multiple_of(c * C, C), C)
        bs = range(nb)
        hs = range(GLA_HEADS)
        kl = [slice(h * GLA_DK, (h + 1) * GLA_DK) for h in hs]
        vl = [slice(h * GLA_DV, (h + 1) * GLA_DV) for h in hs]
        pre = [jnp.dot(ad_ref[b, rows, :], aw_ref[...], preferred_element_type=F32) + ab_ref[...]
               for b in bs]
        la = [(jnp.minimum(p, 0.0) - jnp.log(1.0 + jnp.exp2(jnp.abs(p) * (-LOG2_E)))) * (LOG2_E / GLA_TAU)
              for p in pre]
        la_hi = [x.astype(BF16) for x in la]
        la_lo = [(x - hi.astype(F32)).astype(BF16) for x, hi in zip(la, la_hi)]
        bc = [jnp.dot(tril, hi, preferred_element_type=F32) + jnp.dot(tril, lo, preferred_element_type=F32)
              for hi, lo in zip(la_hi, la_lo)]
        b_last = [x[C - 1:C, :] for x in bc]
        q = [q_ref[b, rows, :].astype(F32) for b in bs]
        k = [k_ref[b, rows, :].astype(F32) for b in bs]
        q_dec = [(q[b] * (jnp.exp2(bc[b]) * scale)).astype(BF16) for b in bs]
        k_inv = [(k[b] * jnp.exp2(-bc[b])).astype(BF16) for b in bs]
        k_end = [(k[b] * jnp.exp2(b_last[b] - bc[b])).astype(BF16) for b in bs]
        decay = [jnp.exp2(x) for x in b_last]
        s = [[jnp.where(causal,
                        lax.dot_general(q_dec[b][:, kl[h]], k_inv[b][:, kl[h]], (((1,), (1,)), ((), ())),
                                        preferred_element_type=F32), 0.0).astype(BF16)
              for h in hs] for b in bs]
        o = [[None] * GLA_HEADS for _ in bs]
        for b in bs:
            for h in hs:
                vh = v_ref[b, rows, vl[h]]
                st = st_ref[b, h]
                o[b][h] = (jnp.dot(s[b][h], vh, preferred_element_type=F32)
                           + lax.dot_general(q_dec[b][:, kl[h]], st.astype(BF16), (((1,), (1,)), ((), ())),
                                             preferred_element_type=F32))
                vk = lax.dot_general(vh, k_end[b][:, kl[h]], (((0,), (0,)), ((), ())),
                                     preferred_element_type=F32)
                st_ref[b, h] = st * decay[b][:, kl[h]] + vk
        for b in bs:
            for h in hs:
                gb = gb_ref[b, rows, vl[h]].astype(F32)
                o_ref[b, rows, vl[h]] = (_rms(o[b][h], gn_ref[...])
                                         * (gb * _sigmoid(gb))).astype(o_ref.dtype)
        return carry

    lax.fori_loop(0, nchunk, chunk, 0)


GLA_BATCH_PER_STEP = 8
GLA_TIME_TILE = 128


def _gla(q, k, v, gb, ad, aw, ab, gn):
    B, Tp, _ = q.shape
    nb = GLA_BATCH_PER_STEP if B % GLA_BATCH_PER_STEP == 0 else 1
    tc = GLA_TIME_TILE
    kw, vw = GLA_HEADS * GLA_DK, GLA_HEADS * GLA_DV
    spec = lambda w: pl.BlockSpec((nb, tc, w), lambda b, t: (b, t, 0))
    return pl.pallas_call(
        functools.partial(_gla_kernel, nchunk=tc // GLA_CHUNK, nb=nb),
        out_shape=jax.ShapeDtypeStruct((B, Tp, vw), BF16),
        grid=(B // nb, Tp // tc),
        in_specs=[spec(kw), spec(kw), spec(vw), spec(vw), spec(LANES),
                  _const_spec(aw.shape), _const_spec(ab.shape), _const_spec(gn.shape)],
        out_specs=spec(vw),
        scratch_shapes=[pltpu.VMEM((nb, GLA_HEADS, GLA_DV, GLA_DK), F32)],
        compiler_params=_params(("parallel", "arbitrary")),
        name="gla",
    )(q, k, v, gb, ad, aw, ab, gn)


def _even_out_value(h_ref, ya_ref, ob_ref, w_ref):
    n = ya_ref.shape[-1]
    return (h_ref[...]
            + jnp.dot(ya_ref[...], w_ref[:n, :], preferred_element_type=F32)
            + jnp.dot(ob_ref[...], w_ref[n:, :], preferred_element_type=F32))


def _even_out_odd_in_kernel(h_ref, ya_ref, ob_ref, wo_ref, *refs):
    hout_ref = refs[-5]
    h = _even_out_value(h_ref, ya_ref, ob_ref, wo_ref)
    hout_ref[...] = h
    _odd_in_body(h, *refs[:-5], *refs[-4:])


def _even_out_odd_in(h2, ya, ob, wo, params, cos_t, sin_t):
    M = h2.shape[0]
    tm = _row_tile(M)
    return pl.pallas_call(
        _even_out_odd_in_kernel,
        out_shape=(jax.ShapeDtypeStruct((M, D_MODEL), F32),) + _odd_in_outs(M),
        grid=(M // tm,),
        in_specs=[_row_spec(tm, D_MODEL), _row_spec(tm, ya.shape[-1]), _row_spec(tm, ob.shape[-1]),
                  _const_spec(wo.shape)] + _odd_in_specs(tm, params),
        out_specs=(_row_spec(tm, D_MODEL),) + tuple(_row_spec(tm, n) for n in ODD_IN_WIDTHS),
        compiler_params=_params(("parallel",)),
        name="even_out_odd_in",
    )(h2, ya, ob, wo, *params, cos_t, sin_t)


ROPE_HALF = MLA_ROPE // 2
ROPE_X2_LANE = HEAD_PAD // 2


def _head_lanes(nope, rope_part):
    zeros = lambda n: jnp.zeros(nope.shape[:-1] + (n,), nope.dtype)
    n1 = ROPE_X2_LANE - ROPE_HALF
    return jnp.concatenate([rope_part[..., :ROPE_HALF], nope[..., :n1], rope_part[..., ROPE_HALF:],
                            nope[..., n1:], zeros(HEAD_PAD - MLA_NOPE - MLA_ROPE)], axis=-1)


ROPE_GROUPS = LANES // MLA_ROPE


def _rope_table_kernel(pos_ref, freq_ref, phase_ref, c_ref, s_ref):
    tm = pos_ref.shape[0]
    q = tm // ROPE_GROUPS
    lane = lax.broadcasted_iota(jnp.int32, (q, LANES), 1)
    pos = pos_ref[0:q, :]
    for i in range(1, ROPE_GROUPS):
        pos = jnp.where(lane >= i * MLA_ROPE, pos_ref[i * q:(i + 1) * q, :], pos)
    t = jnp.cos(pos * freq_ref[...] - phase_ref[...])
    x1 = lane < ROPE_HALF
    x2 = (lane >= ROPE_X2_LANE) & (lane < ROPE_X2_LANE + ROPE_HALF)
    for i in range(ROPE_GROUPS):
        ti = pltpu.roll(t, (LANES - i * MLA_ROPE) % LANES, axis=1) if i else t
        cos_x2 = pltpu.roll(ti, ROPE_X2_LANE, axis=1)
        sin_x1 = pltpu.roll(ti, LANES - ROPE_HALF, axis=1)
        sin_x2 = pltpu.roll(ti, ROPE_X2_LANE - ROPE_HALF, axis=1)
        rows = slice(i * q, (i + 1) * q)
        c_ref[rows, :] = jnp.where(x1, ti, jnp.where(x2, cos_x2, 1.0))
        s_ref[rows, :] = jnp.where(x1, -sin_x1, jnp.where(x2, sin_x2, 0.0))


def _rope_tables(posf, inv_freq):
    M = posf.shape[0]
    tm = _pick(M, (1088, 1024, 512, 256, 128))
    out = jax.ShapeDtypeStruct((M, HEAD_PAD), F32)
    zeros = jnp.zeros((ROPE_HALF,), F32)
    freq = jnp.tile(jnp.concatenate([inv_freq, inv_freq]), ROPE_GROUPS).reshape(1, LANES)
    phase = jnp.tile(jnp.concatenate([zeros, zeros + np.pi / 2]), ROPE_GROUPS).reshape(1, LANES)
    return pl.pallas_call(
        _rope_table_kernel,
        out_shape=(out, out),
        grid=(M // tm,),
        in_specs=[pl.BlockSpec((tm, 1), lambda i: (i, 0)), _const_spec((1, LANES)), _const_spec((1, LANES))],
        out_specs=(pl.BlockSpec((tm, HEAD_PAD), lambda i: (i, 0)),) * 2,
        compiler_params=_params(("parallel",)),
        name="rope_tables",
    )(posf, freq, phase)


def _rope(x, cos_t, sin_t):
    heads = [pltpu.roll(x[:, c:c + HEAD_PAD], ROPE_X2_LANE, axis=1) for c in range(0, x.shape[1], HEAD_PAD)]
    partner = heads[0] if len(heads) == 1 else jnp.concatenate(heads, axis=1)
    return x * cos_t + partner * sin_t


def _odd_in_body(h, g_ref, w_ref, qn_ref, wq_ref, kvn_ref, wk_ref, wv_ref, cos_ref, sin_ref,
                 q_ref, k_ref, v_ref, gate_ref):
    xn = _rms(h, g_ref[...]).astype(BF16)
    c0, c1, c2 = MLA_Q_RANK, MLA_Q_RANK + MLA_KV_RANK, MLA_Q_RANK + MLA_KV_RANK + HEAD_PAD
    gate_ref[...] = jnp.dot(xn, w_ref[:, c2:], preferred_element_type=F32).astype(gate_ref.dtype)
    cq = jnp.dot(xn, w_ref[:, :c0], preferred_element_type=F32)
    ckv = jnp.dot(xn, w_ref[:, c0:c1], preferred_element_type=F32)
    kr = jnp.dot(xn, w_ref[:, c1:c2], preferred_element_type=F32)
    cos_t, sin_t = cos_ref[...], sin_ref[...]
    kr = _rope(kr, cos_t, sin_t)
    cqn = _rms(cq, qn_ref[...]).astype(BF16)
    ckvn = _rms(ckv, kvn_ref[...]).astype(BF16)
    v_ref[...] = jnp.dot(ckvn, wv_ref[...], preferred_element_type=F32).astype(v_ref.dtype)
    scale = LOG2_E * (MLA_NOPE + MLA_ROPE) ** -0.5
    cos2 = jnp.concatenate([cos_t, cos_t], axis=1) * scale
    sin2 = jnp.concatenate([sin_t, sin_t], axis=1) * scale
    kr2 = jnp.concatenate([kr, kr], axis=1)
    for hp in range(MLA_HEADS // 2):
        lanes = slice(hp * 2 * HEAD_PAD, (hp + 1) * 2 * HEAD_PAD)
        qh = jnp.dot(cqn, wq_ref[:, lanes], preferred_element_type=F32)
        q_ref[:, lanes] = _rope(qh, cos2, sin2).astype(q_ref.dtype)
        kh = jnp.dot(ckvn, wk_ref[:, lanes], preferred_element_type=F32)
        k_ref[:, lanes] = (kh + kr2).astype(k_ref.dtype)


ODD_IN_WIDTHS = (MLA_HEADS * HEAD_PAD, MLA_HEADS * HEAD_PAD, MLA_HEADS * MLA_V, MLA_HEADS * MLA_V)


def _odd_in_outs(M):
    return tuple(jax.ShapeDtypeStruct((M, n), BF16) for n in ODD_IN_WIDTHS)


def _odd_in_specs(tm, params):
    return [_const_spec(p.shape) for p in params] + [_row_spec(tm, HEAD_PAD), _row_spec(tm, HEAD_PAD)]


def _attn_kernel(q_ref, k_ref, v_ref, o_ref, k_scr, vt_scr, qt_scr, bias_scr, s_a, s_b, mx_a, mx_b,
                 m_scr, acc_scr, o_scr, *, nh, n_q, seq):
    T = ATT_TILE
    Tp = q_ref.shape[1]
    M0 = N_META
    Tp2 = n_q * T
    SL = SOFTMAX_SLAB
    VE = ATT_V_EXT

    k_scr[0:seq, :] = k_ref[0, M0:M0 + seq, :]
    if Tp2 > seq:
        k_scr[seq:Tp2, :] = jnp.zeros((Tp2 - seq, k_scr.shape[1]), k_scr.dtype)
    ones_row = (lax.broadcasted_iota(jnp.int32, (VE - MLA_V, Tp2), 0) == 0).astype(vt_scr.dtype)
    for h in range(nh):
        vt_scr[h * VE + MLA_V:(h + 1) * VE, :] = ones_row
        if Tp2 > seq:
            vt_scr[h * VE:h * VE + MLA_V, seq:Tp2] = jnp.zeros((MLA_V, Tp2 - seq), vt_scr.dtype)
    for c in range(0, seq, LANES):
        w = min(LANES, seq - c)
        vt = v_ref[0, M0 + c:M0 + c + w, :].T
        for h in range(nh):
            vt_scr[h * VE:h * VE + MLA_V, c:c + w] = vt[h * MLA_V:(h + 1) * MLA_V, :]
    for j in range(n_q):
        rows = min(T, seq - j * T)
        for h in range(nh):
            qh = q_ref[0, M0 + j * T:M0 + j * T + rows, h * HEAD_PAD:(h + 1) * HEAD_PAD]
            qt_scr[j, h, :, 0:rows] = qh.T
            if rows < T:
                qt_scr[j, h, :, rows:T] = jnp.zeros((HEAD_PAD, T - rows), qt_scr.dtype)
    kk = lax.broadcasted_iota(jnp.int32, (T, T), 0)
    qq = lax.broadcasted_iota(jnp.int32, (T, T), 1)
    bias_scr[...] = jnp.where(kk <= qq, 0.0, NEG_BIG)

    def qk(buf, j, i, diag=False, heads=range(nh)):
        s_ref, mx_ref = buf
        koff = pl.multiple_of(i * T, T)
        for h in heads:
            sv = jnp.dot(k_scr[pl.ds(koff, T), h * HEAD_PAD:(h + 1) * HEAD_PAD],
                         qt_scr[j, h], preferred_element_type=F32)
            if diag:
                sv = sv + bias_scr[...]
            s_ref[h] = sv
            parts = [None] * 4
            for n, r in enumerate(range(0, T, SL)):
                blk = sv[r:r + SL, :]
                parts[n % 4] = blk if parts[n % 4] is None else jnp.maximum(parts[n % 4], blk)
            mt = jnp.maximum(jnp.maximum(parts[0], parts[1]), jnp.maximum(parts[2], parts[3]))
            mx_ref[h] = jnp.max(mt, axis=0, keepdims=True)

    def softmax_pv(buf, j, i, diag, heads=range(nh)):
        s_ref, mx_ref = buf
        koff = pl.multiple_of(i * T, T)
        for h in heads:
            if diag:
                m_new = mx_ref[h]
            else:
                m_old = m_scr[j, h]
                m_new = jnp.maximum(m_old, mx_ref[h])
                alpha = jnp.exp2(m_old - m_new)
            m_scr[j, h] = m_new
            p = jnp.concatenate([jnp.exp2((s_ref[h, r:r + SL, :] - m_new).astype(BF16))
                                 for r in range(0, T, SL)], axis=0)
            rows = slice(h * VE, (h + 1) * VE)
            pv = jnp.dot(vt_scr[rows, pl.ds(koff, T)], p, preferred_element_type=F32)
            acc_scr[j, rows, :] = pv if diag else alpha * acc_scr[j, rows, :] + pv

    bufs = ((s_a, mx_a), (s_b, mx_b))
    G = ATT_STEPS_PER_ITER

    def run(n_steps, first, advance, diag):
        def steps(count, jt):
            for t in range(count):
                nxt = advance(*jt)
                for h in range(nh):
                    qk(bufs[(t + 1) % 2], jnp.minimum(nxt[0], n_q - 1), nxt[1], diag, (h,))
                    softmax_pv(bufs[t % 2], jt[0], jt[1], diag, (h,))
                jt = nxt
            return jt

        qk(bufs[0], first[0], first[1], diag)
        jt = lax.fori_loop(0, n_steps // G, lambda n, c: steps(G, c),
                           (jnp.int32(first[0]), jnp.int32(first[1])))
        steps(n_steps % G, jt)

    run(n_q, (0, 0), lambda j, i: (j + 1, jnp.minimum(i + 1, n_q - 1)), True)

    def advance(j, i):
        last = i + 1 >= j
        return jnp.where(last, j + 1, j), jnp.where(last, 0, i + 1)

    if n_q > 1:
        run(n_q * (n_q - 1) // 2, (1, 0), advance, False)

    def finish(j, c):
        hs = range(nh)
        sm = [jnp.dot(k_ref[0, 0:M0, h * HEAD_PAD:(h + 1) * HEAD_PAD], qt_scr[j, h],
                      preferred_element_type=F32) for h in hs]
        m_old = [m_scr[j, h] for h in hs]
        m_new = [jnp.maximum(m_old[h], jnp.max(sm[h], axis=0, keepdims=True)) for h in hs]
        alpha = [jnp.exp2(m_old[h] - m_new[h]) for h in hs]
        pm = [jnp.exp2((sm[h] - m_new[h]).astype(BF16)) for h in hs]
        pv = [lax.dot_general(v_ref[0, 0:M0, h * MLA_V:(h + 1) * MLA_V], pm[h], (((0,), (0,)), ((), ())),
                              preferred_element_type=F32) for h in hs]
        outs = []
        for h in hs:
            num = alpha[h] * acc_scr[j, h * VE:h * VE + MLA_V, :] + pv[h]
            den = (alpha[h] * acc_scr[j, h * VE + MLA_V:h * VE + MLA_V + 1, :]
                   + jnp.sum(pm[h].astype(F32), axis=0, keepdims=True))
            outs.append(num / den)
        o_scr[pl.ds(pl.multiple_of(j * T, T), T), :] = jnp.concatenate(outs, axis=0).T.astype(o_scr.dtype)
        return c

    lax.fori_loop(0, n_q, finish, 0, unroll=4)
    o_ref[0, M0:M0 + seq, :] = o_scr[0:seq, :]

    qi = lax.broadcasted_iota(jnp.int32, (M0, M0), 0)
    ki = lax.broadcasted_iota(jnp.int32, (M0, M0), 1)
    outs = []
    for h in range(nh):
        lanes = slice(h * HEAD_PAD, (h + 1) * HEAD_PAD)
        s = lax.dot_general(q_ref[0, 0:M0, lanes], k_ref[0, 0:M0, lanes], (((1,), (1,)), ((), ())),
                            preferred_element_type=F32)
        s = jnp.where(ki <= qi, s, NEG_BIG)
        p = jnp.exp2(s - jnp.max(s, axis=1, keepdims=True))
        den = jnp.sum(p, axis=1, keepdims=True)
        pv = jnp.dot(p.astype(BF16), v_ref[0, 0:M0, h * MLA_V:(h + 1) * MLA_V], preferred_element_type=F32)
        outs.append(pv / den)
    o_ref[0, 0:M0, :] = jnp.concatenate(outs, axis=1).astype(o_ref.dtype)
    if Tp > M0 + seq:
        o_ref[0, M0 + seq:Tp, :] = jnp.zeros((Tp - M0 - seq, o_ref.shape[2]), o_ref.dtype)


ATT_HEADS_PER_STEP = 8
ATT_STEPS_PER_ITER = 12
ATT_V_EXT = MLA_V + 16


def _attention(q, k, v, seq):
    B, Tp, _ = q.shape
    nh = ATT_HEADS_PER_STEP
    T = ATT_TILE
    n_q = -(-seq // T)
    Tp2 = n_q * T
    qk_spec = pl.BlockSpec((1, Tp, nh * HEAD_PAD), lambda b, p: (b, 0, p))
    v_spec = pl.BlockSpec((1, Tp, nh * MLA_V), lambda b, p: (b, 0, p))
    return pl.pallas_call(
        functools.partial(_attn_kernel, nh=nh, n_q=n_q, seq=seq),
        out_shape=jax.ShapeDtypeStruct((B, Tp, MLA_HEADS * MLA_V), BF16),
        grid=(B, MLA_HEADS // nh),
        in_specs=[qk_spec, qk_spec, v_spec],
        out_specs=v_spec,
        scratch_shapes=[pltpu.VMEM((Tp2, nh * HEAD_PAD), BF16),
                        pltpu.VMEM((nh * ATT_V_EXT, Tp2), BF16),
                        pltpu.VMEM((n_q, nh, HEAD_PAD, T), BF16),
                        pltpu.VMEM((T, T), F32),
                        pltpu.VMEM((nh, T, T), F32),
                        pltpu.VMEM((nh, T, T), F32),
                        pltpu.VMEM((nh, 1, T), F32),
                        pltpu.VMEM((nh, 1, T), F32),
                        pltpu.VMEM((n_q, nh, 1, T), F32),
                        pltpu.VMEM((n_q, nh * ATT_V_EXT, T), F32),
                        pltpu.VMEM((Tp2, nh * MLA_V), BF16)],
        compiler_params=_params(("parallel", "parallel")),
        name="mla_attention",
    )(q, k, v)


def _odd_out_value(h_ref, o_ref, gate_ref, w_ref):
    gate = gate_ref[...].astype(F32)
    x = (o_ref[...].astype(F32) * (gate * _sigmoid(gate))).astype(BF16)
    return h_ref[...] + jnp.dot(x, w_ref[...], preferred_element_type=F32)


def _odd_out_even_in_kernel(h_ref, o_ref, gate_ref, wo_ref, g_ref, w_ref, hout_ref, *out_refs):
    tm = h_ref.shape[0]
    slab = _slab_rows(tm)
    for r in range(0, tm, slab):
        rows = pl.ds(r, slab)
        h = _odd_out_value(h_ref.at[rows], o_ref.at[rows], gate_ref.at[rows], wo_ref)
        hout_ref[rows, :] = h
        _even_in_body(h, g_ref, w_ref, [ref.at[rows] for ref in out_refs])


def _odd_out_even_in(h2, o, gate, wo, g, w):
    M = h2.shape[0]
    tm = _row_tile(M)
    return pl.pallas_call(
        _odd_out_even_in_kernel,
        out_shape=(jax.ShapeDtypeStruct((M, D_MODEL), F32),) + _even_in_outs(M),
        grid=(M // tm,),
        in_specs=[_row_spec(tm, D_MODEL), _row_spec(tm, o.shape[-1]), _row_spec(tm, gate.shape[-1]),
                  _const_spec(wo.shape), _const_spec(g.shape), _const_spec(w.shape)],
        out_specs=(_row_spec(tm, D_MODEL),) + tuple(_row_spec(tm, n) for n in EVEN_OUT_WIDTHS),
        compiler_params=_params(("parallel",)),
        name="odd_out_even_in",
    )(h2, o, gate, wo, g, w)


def _odd_out_last_kernel(h_ref, o_ref, gate_ref, w_ref, fn_ref, out_ref):
    out_ref[...] = _rms(_odd_out_value(h_ref, o_ref, gate_ref, w_ref), fn_ref[...])


def _odd_out_last(h2, o, gate, w, fn, B, Tp, S):
    tm = _pick(S, (1024, 512, 256, 128, 64))
    win = lambda n: pl.BlockSpec((pl.Element(tm), pl.Element(n)),
                                 lambda b, i: (pl.multiple_of(b * Tp + N_META + i * tm, N_META), 0))
    return pl.pallas_call(
        _odd_out_last_kernel,
        out_shape=jax.ShapeDtypeStruct((B * S, D_MODEL), F32),
        grid=(B, S // tm),
        in_specs=[win(D_MODEL), win(o.shape[-1]), win(gate.shape[-1]), _const_spec(w.shape),
                  _const_spec(fn.shape)],
        out_specs=pl.BlockSpec((tm, D_MODEL), lambda b, i: (b * (S // tm) + i, 0)),
        compiler_params=_params(("parallel", "parallel")),
        name="odd_out_last",
    )(h2, o, gate, w, fn)


def _split_cols(w, sizes):
    idx = np.cumsum(sizes)[:-1].tolist()
    return jnp.split(w, idx, axis=-1)


def _pack_even(w_in, gate_a_w, gate_x_w, alpha_w):
    xa, ga, q, k, v, ad, gb = _split_cols(
        w_in, (RNN_WIDTH, RNN_WIDTH, GLA_HEADS * GLA_DK, GLA_HEADS * GLA_DK, GLA_HEADS * GLA_DV,
               GLA_GATE_RANK, GLA_HEADS * GLA_DV))
    ad = jnp.pad(ad, ((0, 0), (0, LANES - GLA_GATE_RANK)))
    w = jnp.concatenate([xa, ga, q, k, v, gb, ad], axis=-1).astype(BF16)
    wg = jnp.concatenate([gate_a_w, gate_x_w], axis=-1).astype(BF16)
    aw = jnp.pad(alpha_w, ((0, LANES - GLA_GATE_RANK), (0, 0))).astype(BF16)
    return w, wg, aw


def _pack_odd(w_in, w_q_up, w_kv_up):
    cq, ckv, kr, gate = _split_cols(w_in, (MLA_Q_RANK, MLA_KV_RANK, MLA_ROPE, MLA_HEADS * MLA_V))
    kr = _head_lanes(jnp.zeros((kr.shape[0], MLA_NOPE), kr.dtype), kr)
    w = jnp.concatenate([cq, ckv, kr, gate], axis=-1).astype(BF16)
    wq = w_q_up.reshape(MLA_Q_RANK, MLA_HEADS, MLA_NOPE + MLA_ROPE)
    wq = _head_lanes(wq[..., :MLA_NOPE], wq[..., MLA_NOPE:]).reshape(MLA_Q_RANK, MLA_HEADS * HEAD_PAD)
    wkv = w_kv_up.reshape(MLA_KV_RANK, MLA_HEADS, MLA_NOPE + MLA_V)
    wk = _head_lanes(wkv[..., :MLA_NOPE], jnp.zeros(wkv.shape[:-1] + (MLA_ROPE,), wkv.dtype))
    wk = wk.reshape(MLA_KV_RANK, MLA_HEADS * HEAD_PAD)
    wv = wkv[..., MLA_NOPE:].reshape(MLA_KV_RANK, MLA_HEADS * MLA_V)
    return w, wq.astype(BF16), wk.astype(BF16), wv.astype(BF16)


def kernel(x, positions, meta_tokens, ab_norm, ab_w_in, ab_conv_w, ab_conv_b, ab_gate_a_w, ab_gate_a_b, ab_gate_x_w, ab_gate_x_b, ab_lru_lambda, ab_alpha_w, ab_alpha_b, ab_gla_norm, ab_w_out, c_norm, c_w_in, c_q_norm, c_w_q_up, c_kv_norm, c_w_kv_up, c_w_out, final_norm):
    B, S, D = x.shape
    T = N_META + S
    Tp = -(-T // LANES) * LANES
    M = B * Tp
    depth = ab_norm.shape[0] + c_norm.shape[0]
    row2 = lambda a: a.reshape(1, -1).astype(F32)


    meta_pos = jnp.broadcast_to(jnp.arange(N_META, dtype=positions.dtype)[None], (B, N_META))
    pos = jnp.concatenate([meta_pos, positions + N_META,
                           jnp.zeros((B, Tp - T), positions.dtype)], axis=1)
    inv_freq = ROPE_BASE ** (-jnp.arange(0, MLA_ROPE, 2, dtype=F32) / MLA_ROPE)
    cos_t, sin_t = _rope_tables(pos.astype(F32).reshape(M, 1), inv_freq)

    if depth % 2 or ab_norm.shape[0] != c_norm.shape[0]:
        raise NotImplementedError("layers must alternate (RG-LRU || GLA), MLA and end on an MLA layer")
    r3 = lambda a: a.reshape(B, Tp, a.shape[-1])
    even_packs = [_pack_even(ab_w_in[j], ab_gate_a_w[j], ab_gate_x_w[j], ab_alpha_w[j])
                  for j in range(depth // 2)]
    h, *even_in = _embed_even_in(x, meta_tokens.astype(x.dtype), row2(ab_norm[0]), even_packs[0][0], Tp)
    for j in range(depth // 2):
        w, wg, aw = even_packs[j]
        xa, ga, q, k, v, gb, ad = even_in
        ya = _rglru(r3(xa), r3(ga), ab_conv_w[j].astype(F32), row2(ab_conv_b[j]), wg,
                    row2(ab_gate_a_b[j]), row2(ab_gate_x_b[j]), row2(ab_lru_lambda[j]))
        ob = _gla(r3(q), r3(k), r3(v), r3(gb), r3(ad), aw, row2(ab_alpha_b[j]), row2(ab_gla_norm[j]))
        w, wq, wk, wv = _pack_odd(c_w_in[j], c_w_q_up[j], c_w_kv_up[j])
        odd_params = (row2(c_norm[j]), w, row2(c_q_norm[j]), wq, row2(c_kv_norm[j]), wk, wv)
        h, q, k, v, gate = _even_out_odd_in(h, ya.reshape(M, -1), ob.reshape(M, -1),
                                            ab_w_out[j].astype(BF16), odd_params, cos_t, sin_t)
        o = _attention(r3(q), r3(k), r3(v), S).reshape(M, -1)
        wo = c_w_out[j].astype(BF16)
        if j == depth // 2 - 1:
            return _odd_out_last(h, o, gate, wo, row2(final_norm), B, Tp, S).reshape(B, S, D)
        h, *even_in = _odd_out_even_in(h, o, gate, wo, row2(ab_norm[j + 1]), even_packs[j + 1][0])
```
